```python
import math
import jax, jax.numpy as jnp
from jax import lax
import numpy as np

D_MODEL = 1024
BATCH = 8
SEQ = 4096
DEPTH = 2

D_PLE = 256
A_HEADS = 4
A_DK = 128
A_DV = 128
A_WIDTH = A_HEADS * A_DK
B_HEADS = 8
B_DH = 64
B_WIDTH = B_HEADS * B_DH
D_MIX = A_WIDTH + B_WIDTH
IN_WIDTHS = (A_WIDTH, A_WIDTH, A_HEADS * A_DV, A_HEADS * A_DV,
             B_WIDTH, B_WIDTH, B_WIDTH, B_WIDTH)
D_IN = sum(IN_WIDTHS)
CHUNK = 64
Q_BLOCK = 128
EPS = 1e-6

kernel_name = "hymba_hgrn2_stickbreaking_trunk"


def rmsnorm(x, g):
    xf = x.astype(jnp.float32)
    y = xf * lax.rsqrt(jnp.mean(xf * xf, axis=-1, keepdims=True) + EPS)
    return (y * g.astype(jnp.float32)).astype(x.dtype)


def head_rmsnorm(o, g):
    B_, S_, H, d = o.shape
    y = o * lax.rsqrt(jnp.mean(o * o, axis=-1, keepdims=True) + EPS)
    return y.reshape(B_, S_, H * d) * g.astype(jnp.float32)


def hgrn2_chunkwise(q, k, v, log_f):
    B_, S_, H, dk = q.shape
    dv = v.shape[-1]
    n = S_ // CHUNK

    def to_chunks(t):
        return t.reshape(B_, n, CHUNK, H, t.shape[-1]).transpose(1, 0, 3, 2, 4)

    causal = jnp.tril(jnp.ones((CHUNK, CHUNK), dtype=bool))[:, :, None]

    def step(state, inp):
        qc, kc, vc, gc = inp
        b = jnp.cumsum(gc, axis=2)
        diff = b[:, :, :, None, :] - b[:, :, None, :, :]
        decay = jnp.where(causal, jnp.exp(jnp.where(causal, diff, 0.0)), 0.0)
        scores = jnp.einsum('bhtk,bhtsk,bhsk->bhts', qc, decay, kc)
        o = (jnp.einsum('bhts,bhsv->bhtv', scores, vc)
             + jnp.einsum('bhtk,bhkv->bhtv', qc * jnp.exp(b), state))
        b_last = b[:, :, -1:, :]
        state = (jnp.exp(b_last[:, :, 0, :, None]) * state
                 + jnp.einsum('bhsk,bhsv->bhkv', kc * jnp.exp(b_last - b), vc))
        return state, o

    s0 = jnp.zeros((B_, H, dk, dv), jnp.float32)
    _, o = lax.scan(step, s0, (to_chunks(q), to_chunks(k), to_chunks(v), to_chunks(log_f)))
    return o.transpose(1, 0, 3, 2, 4).reshape(B_, S_, H, dv)


def stick_breaking(q, k, v):
    S_ = q.shape[2]
    scale = B_DH ** -0.5
    outs = []
    for blk in range(S_ // Q_BLOCK):
        t0 = blk * Q_BLOCK
        t1 = t0 + Q_BLOCK
        z = jnp.einsum('bhtd,bhsd->bhts', q[:, :, t0:t1], k[:, :, :t1]) * scale
        mask = jnp.arange(t1)[None, :] < (t0 + jnp.arange(Q_BLOCK))[:, None]
        log_1m = jnp.where(mask, -jax.nn.softplus(z), 0.0)
        log_rest = lax.cumsum(log_1m, axis=3, reverse=True) - log_1m
        w = jnp.where(mask, jnp.exp(jax.nn.log_sigmoid(z) + log_rest), 0.0)
        outs.append(jnp.einsum('bhts,bhsd->bhtd', w, v[:, :, :t1]))
    return jnp.concatenate(outs, axis=2)


def mixer_layer(h, norm_g, w_in, a_norm_g, b_norm_g, w_out, lb):
    B_, S_, _ = h.shape
    f32 = jnp.float32
    u = rmsnorm(h, norm_g)
    proj = jnp.einsum('bsd,de->bse', u, w_in)
    a_q, a_f, a_i, a_g, b_q, b_k, b_v, b_g = jnp.split(
        proj, [int(c) for c in np.cumsum(IN_WIDTHS)[:-1]], axis=-1)

    lb = lb.astype(f32)
    k_a = (1.0 - lb) * jax.nn.sigmoid(-a_f.astype(f32))
    log_f = jnp.log1p(-k_a)
    q_a = jax.nn.silu(a_q.astype(f32))
    hd = lambda t, d: t.reshape(B_, S_, A_HEADS, d)
    o_a = hgrn2_chunkwise(hd(q_a, A_DK), hd(k_a, A_DK), hd(a_i.astype(f32), A_DV), hd(log_f, A_DK))
    o_a = head_rmsnorm(o_a, a_norm_g) * jax.nn.silu(a_g.astype(f32))

    to_heads = lambda t: t.astype(f32).reshape(B_, S_, B_HEADS, B_DH).transpose(0, 2, 1, 3)
    o_b = stick_breaking(to_heads(b_q), to_heads(b_k), to_heads(b_v)).transpose(0, 2, 1, 3)
    o_b = head_rmsnorm(o_b, b_norm_g) * jax.nn.silu(b_g.astype(f32))

    y = jnp.concatenate([o_a, o_b], axis=-1).astype(h.dtype)
    return h + jnp.einsum('bse,ed->bsd', y, w_out)


def _fwd_setup_inputs(seed: int = 0) -> dict:
    key = jax.random.key(seed)
    ks = jax.random.split(key, 14)
    f32 = jnp.float32
    nrm = lambda k, shape, s: jax.random.normal(k, shape, f32) * s
    return {
        "x": nrm(ks[0], (BATCH, SEQ, D_MODEL), 1.0),
        "p": nrm(ks[1], (DEPTH, BATCH, SEQ, D_PLE), 1.0),
        "norm_mix": 1.0 + nrm(ks[2], (DEPTH, D_MODEL), 0.02),
        "w_in": nrm(ks[3], (DEPTH, D_MODEL, D_IN), D_MODEL ** -0.5),
        "a_out_norm": 1.0 + nrm(ks[4], (DEPTH, A_HEADS * A_DV), 0.02),
        "b_out_norm": 1.0 + nrm(ks[5], (DEPTH, B_WIDTH), 0.02),
        "w_out": nrm(ks[6], (DEPTH, D_MIX, D_MODEL), 0.5 * D_MIX ** -0.5),
        "lb_logits": nrm(ks[7], (DEPTH, A_WIDTH), 0.1),
        "ple_gate_norm": 1.0 + nrm(ks[8], (DEPTH, D_MODEL), 0.02),
        "w_ple_gate": nrm(ks[9], (DEPTH, D_MODEL, D_MODEL), D_MODEL ** -0.5),
        "w_ple_proj": nrm(ks[10], (DEPTH, D_PLE, D_MODEL), D_PLE ** -0.5),
        "ple_post_norm": 1.0 + nrm(ks[11], (DEPTH, D_MODEL), 0.02),
        "final_norm": 1.0 + nrm(ks[12], (D_MODEL,), 0.02),
    }


def _fwd_reference(x, p, norm_mix, w_in, a_out_norm, b_out_norm, w_out, lb_logits,
              ple_gate_norm, w_ple_gate, w_ple_proj, ple_post_norm, final_norm):
    sm = jax.nn.softmax(lb_logits.astype(jnp.float32), axis=0)
    lower_bounds = jnp.cumsum(sm, axis=0) - sm[0:1]

    h = x
    for i in range(DEPTH):
        h = mixer_layer(h, norm_mix[i], w_in[i], a_out_norm[i], b_out_norm[i], w_out[i], lower_bounds[i])
        pe = rmsnorm(jnp.einsum('bsc,cd->bsd', p[i], w_ple_proj[i]), ple_post_norm[i])
        gate = jax.nn.sigmoid(jnp.einsum('bsd,de->bse', rmsnorm(h, ple_gate_norm[i]), w_ple_gate[i]))
        h = h + gate * pe
    return rmsnorm(h, final_norm)


import jax as _jax
import jax.numpy as _jnp

TWIN_FORMAT = 'train_step'
FWD_PARAMS = ['x', 'p', 'norm_mix', 'w_in', 'a_out_norm', 'b_out_norm', 'w_out', 'lb_logits', 'ple_gate_norm', 'w_ple_gate', 'w_ple_proj', 'ple_post_norm', 'final_norm']
TWIN_WEIGHTS = ['norm_mix', 'w_in', 'a_out_norm', 'b_out_norm', 'w_out', 'lb_logits', 'ple_gate_norm', 'w_ple_gate', 'w_ple_proj', 'ple_post_norm', 'final_norm']
TWIN_DIFF_INPUT = 'x'
TWIN_INPUTS = ['x', 'p', 'norm_mix', 'w_in', 'a_out_norm', 'b_out_norm', 'w_out', 'lb_logits', 'ple_gate_norm', 'w_ple_gate', 'w_ple_proj', 'ple_post_norm', 'final_norm', 'loss_target', 'm_norm_mix', 'm_w_in', 'm_a_out_norm', 'm_b_out_norm', 'm_w_out', 'm_lb_logits', 'm_ple_gate_norm', 'm_w_ple_gate', 'm_w_ple_proj', 'm_ple_post_norm', 'm_final_norm', 'v_norm_mix', 'v_w_in', 'v_a_out_norm', 'v_b_out_norm', 'v_w_out', 'v_lb_logits', 'v_ple_gate_norm', 'v_w_ple_gate', 'v_w_ple_proj', 'v_ple_post_norm', 'v_final_norm']
TWIN_OUTPUTS = ['loss', 'grad_x', 'grad_norm_mix', 'grad_w_in', 'grad_a_out_norm', 'grad_b_out_norm', 'grad_w_out', 'grad_lb_logits', 'grad_ple_gate_norm', 'grad_w_ple_gate', 'grad_w_ple_proj', 'grad_ple_post_norm', 'grad_final_norm', 'delta_norm_mix', 'delta_w_in', 'delta_a_out_norm', 'delta_b_out_norm', 'delta_w_out', 'delta_lb_logits', 'delta_ple_gate_norm', 'delta_w_ple_gate', 'delta_w_ple_proj', 'delta_ple_post_norm', 'delta_final_norm', 'new_m_norm_mix', 'new_m_w_in', 'new_m_a_out_norm', 'new_m_b_out_norm', 'new_m_w_out', 'new_m_lb_logits', 'new_m_ple_gate_norm', 'new_m_w_ple_gate', 'new_m_w_ple_proj', 'new_m_ple_post_norm', 'new_m_final_norm', 'new_v_norm_mix', 'new_v_w_in', 'new_v_a_out_norm', 'new_v_b_out_norm', 'new_v_w_out', 'new_v_lb_logits', 'new_v_ple_gate_norm', 'new_v_w_ple_gate', 'new_v_w_ple_proj', 'new_v_ple_post_norm', 'new_v_final_norm']
TWIN_LEAF_KINDS = {'loss': 'loss', 'grad_x': 'grad_x', 'grad_norm_mix': 'grad_w', 'grad_w_in': 'grad_w', 'grad_a_out_norm': 'grad_w', 'grad_b_out_norm': 'grad_w', 'grad_w_out': 'grad_w', 'grad_lb_logits': 'grad_w', 'grad_ple_gate_norm': 'grad_w', 'grad_w_ple_gate': 'grad_w', 'grad_w_ple_proj': 'grad_w', 'grad_ple_post_norm': 'grad_w', 'grad_final_norm': 'grad_w', 'delta_norm_mix': 'delta_w', 'delta_w_in': 'delta_w', 'delta_a_out_norm': 'delta_w', 'delta_b_out_norm': 'delta_w', 'delta_w_out': 'delta_w', 'delta_lb_logits': 'delta_w', 'delta_ple_gate_norm': 'delta_w', 'delta_w_ple_gate': 'delta_w', 'delta_w_ple_proj': 'delta_w', 'delta_ple_post_norm': 'delta_w', 'delta_final_norm': 'delta_w', 'new_m_norm_mix': 'new_m', 'new_m_w_in': 'new_m', 'new_m_a_out_norm': 'new_m', 'new_m_b_out_norm': 'new_m', 'new_m_w_out': 'new_m', 'new_m_lb_logits': 'new_m', 'new_m_ple_gate_norm': 'new_m', 'new_m_w_ple_gate': 'new_m', 'new_m_w_ple_proj': 'new_m', 'new_m_ple_post_norm': 'new_m', 'new_m_final_norm': 'new_m', 'new_v_norm_mix': 'new_v', 'new_v_w_in': 'new_v', 'new_v_a_out_norm': 'new_v', 'new_v_b_out_norm': 'new_v', 'new_v_w_out': 'new_v', 'new_v_lb_logits': 'new_v', 'new_v_ple_gate_norm': 'new_v', 'new_v_w_ple_gate': 'new_v', 'new_v_w_ple_proj': 'new_v', 'new_v_ple_post_norm': 'new_v', 'new_v_final_norm': 'new_v'}


def _forward(args):
    return _fwd_reference(*[args[k] for k in FWD_PARAMS])


def _output_shape():
    out = _jax.eval_shape(lambda: _forward(_fwd_setup_inputs(0)))
    return out.shape, out.dtype

N_MICROBATCH = 1
ADAM_LR = 0.001
ADAM_B1 = 0.9
ADAM_B2 = 0.999
ADAM_EPS = 1e-08
ADAM_WD = 0.01
ADAM_STEP = 10
PER_EXAMPLE_BATCH_AXIS = {'x': 0, 'p': 1, 'loss_target': 0}
SHARED_INPUTS = []
_WEIGHT_DTYPES = {'norm_mix': _jnp.float32, 'w_in': _jnp.float32, 'a_out_norm': _jnp.float32, 'b_out_norm': _jnp.float32, 'w_out': _jnp.float32, 'lb_logits': _jnp.float32, 'ple_gate_norm': _jnp.float32, 'w_ple_gate': _jnp.float32, 'w_ple_proj': _jnp.float32, 'ple_post_norm': _jnp.float32, 'final_norm': _jnp.float32}
MOMENT_SCALE = {'norm_mix': 6.436684e-02, 'w_in': 3.227780e-02, 'a_out_norm': 4.269108e-02, 'b_out_norm': 4.131343e-02, 'w_out': 8.474375e-02, 'lb_logits': 3.929865e-03, 'ple_gate_norm': 3.212698e-02, 'w_ple_gate': 2.957978e-02, 'w_ple_proj': 7.581137e-02, 'ple_post_norm': 1.433736e-01, 'final_norm': 3.199668e+01}


def _to_microbatches(a, axis):
    t = _jnp.moveaxis(a, axis, 0)
    t = t.reshape((N_MICROBATCH, t.shape[0] // N_MICROBATCH) + t.shape[1:])
    return _jnp.moveaxis(t, 1, axis + 1)


def setup_inputs(seed: int = 0) -> dict:
    inp = _fwd_setup_inputs(seed)
    key = _jax.random.fold_in(_jax.random.key(seed), 7919)
    shape, _ = _output_shape()
    out = dict(inp)
    out["loss_target"] = _jax.random.normal(_jax.random.fold_in(key, 0), shape, _jnp.float32)
    for i, name in enumerate(TWIN_WEIGHTS):
        w = inp[name].astype(_jnp.float32)
        if MOMENT_SCALE is None:
            s = _jnp.sqrt(_jnp.mean(_jnp.square(w)) + 1e-30)
        else:
            s = MOMENT_SCALE[name]
        km, kv = _jax.random.split(_jax.random.fold_in(key, i + 1))
        out[name] = w
        out["m_" + name] = s * _jax.random.normal(km, w.shape, _jnp.float32)
        out["v_" + name] = (s * s) * _jax.random.uniform(kv, w.shape, _jnp.float32, 0.5, 1.5)
    if N_MICROBATCH > 1:
        for name, axis in PER_EXAMPLE_BATCH_AXIS.items():
            out[name] = _to_microbatches(out[name], axis)
    return {'x': out['x'], 'p': out['p'], 'norm_mix': out['norm_mix'], 'w_in': out['w_in'], 'a_out_norm': out['a_out_norm'], 'b_out_norm': out['b_out_norm'], 'w_out': out['w_out'], 'lb_logits': out['lb_logits'], 'ple_gate_norm': out['ple_gate_norm'], 'w_ple_gate': out['w_ple_gate'], 'w_ple_proj': out['w_ple_proj'], 'ple_post_norm': out['ple_post_norm'], 'final_norm': out['final_norm'], 'loss_target': out['loss_target'], 'm_norm_mix': out['m_norm_mix'], 'm_w_in': out['m_w_in'], 'm_a_out_norm': out['m_a_out_norm'], 'm_b_out_norm': out['m_b_out_norm'], 'm_w_out': out['m_w_out'], 'm_lb_logits': out['m_lb_logits'], 'm_ple_gate_norm': out['m_ple_gate_norm'], 'm_w_ple_gate': out['m_w_ple_gate'], 'm_w_ple_proj': out['m_w_ple_proj'], 'm_ple_post_norm': out['m_ple_post_norm'], 'm_final_norm': out['m_final_norm'], 'v_norm_mix': out['v_norm_mix'], 'v_w_in': out['v_w_in'], 'v_a_out_norm': out['v_a_out_norm'], 'v_b_out_norm': out['v_b_out_norm'], 'v_w_out': out['v_w_out'], 'v_lb_logits': out['v_lb_logits'], 'v_ple_gate_norm': out['v_ple_gate_norm'], 'v_w_ple_gate': out['v_w_ple_gate'], 'v_w_ple_proj': out['v_w_ple_proj'], 'v_ple_post_norm': out['v_ple_post_norm'], 'v_final_norm': out['v_final_norm']}


def _loss(weights, diff, rest, loss_target):
    with _jax.named_scope("forward"):
        args = {**rest, TWIN_DIFF_INPUT: diff, **{k: w.astype(_WEIGHT_DTYPES[k]) for k, w in weights.items()}}
        y = _forward(args)
    with _jax.named_scope("loss_head"):
        err = _jnp.square(y.astype(_jnp.float32) - loss_target)
        return 0.5 * _jnp.sum(_jnp.mean(err, axis=-1)) if err.ndim else 0.5 * err


def _adamw(w, g, m, v):
    m = ADAM_B1 * m + (1.0 - ADAM_B1) * g
    v = ADAM_B2 * v + (1.0 - ADAM_B2) * _jnp.square(g)
    m_hat = m / (1.0 - ADAM_B1 ** ADAM_STEP)
    v_hat = v / (1.0 - ADAM_B2 ** ADAM_STEP)
    delta = -ADAM_LR * (m_hat / (_jnp.sqrt(v_hat) + ADAM_EPS) + ADAM_WD * w)
    return delta, m, v


def reference(x, p, norm_mix, w_in, a_out_norm, b_out_norm, w_out, lb_logits, ple_gate_norm, w_ple_gate, w_ple_proj, ple_post_norm, final_norm, loss_target, m_norm_mix, m_w_in, m_a_out_norm, m_b_out_norm, m_w_out, m_lb_logits, m_ple_gate_norm, m_w_ple_gate, m_w_ple_proj, m_ple_post_norm, m_final_norm, v_norm_mix, v_w_in, v_a_out_norm, v_b_out_norm, v_w_out, v_lb_logits, v_ple_gate_norm, v_w_ple_gate, v_w_ple_proj, v_ple_post_norm, v_final_norm):
    given = dict(x=x, p=p, norm_mix=norm_mix, w_in=w_in, a_out_norm=a_out_norm, b_out_norm=b_out_norm, w_out=w_out, lb_logits=lb_logits, ple_gate_norm=ple_gate_norm, w_ple_gate=w_ple_gate, w_ple_proj=w_ple_proj, ple_post_norm=ple_post_norm, final_norm=final_norm, loss_target=loss_target, m_norm_mix=m_norm_mix, m_w_in=m_w_in, m_a_out_norm=m_a_out_norm, m_b_out_norm=m_b_out_norm, m_w_out=m_w_out, m_lb_logits=m_lb_logits, m_ple_gate_norm=m_ple_gate_norm, m_w_ple_gate=m_w_ple_gate, m_w_ple_proj=m_w_ple_proj, m_ple_post_norm=m_ple_post_norm, m_final_norm=m_final_norm, v_norm_mix=v_norm_mix, v_w_in=v_w_in, v_a_out_norm=v_a_out_norm, v_b_out_norm=v_b_out_norm, v_w_out=v_w_out, v_lb_logits=v_lb_logits, v_ple_gate_norm=v_ple_gate_norm, v_w_ple_gate=v_w_ple_gate, v_w_ple_proj=v_w_ple_proj, v_ple_post_norm=v_ple_post_norm, v_final_norm=v_final_norm)
    weights = {n: given[n] for n in TWIN_WEIGHTS}
    shared = {n: given[n] for n in SHARED_INPUTS}
    per_example = {n: given[n] for n in ['x', 'p']}
    grad_fn = _jax.value_and_grad(_loss, argnums=(0, 1))

    def one_microbatch(ex, loss_target):
        ex = dict(ex)
        diff = ex.pop(TWIN_DIFF_INPUT)
        return grad_fn(weights, diff, {**shared, **ex}, loss_target)

    if N_MICROBATCH == 1:
        loss, (grad_w, grad_x) = one_microbatch(per_example, given["loss_target"])
    else:
        def body(carry, xs):
            loss_sum, grad_sum = carry
            l_k, (gw_k, gx_k) = one_microbatch(xs[0], xs[1])
            with _jax.named_scope("update"):
                return (loss_sum + l_k, _jax.tree.map(_jnp.add, grad_sum, gw_k)), gx_k

        init = (_jnp.zeros((), _jnp.float32), _jax.tree.map(_jnp.zeros_like, weights))
        (loss, grad_w), grad_x = _jax.lax.scan(body, init, (per_example, given["loss_target"]))
    with _jax.named_scope("update"):
        delta_w, new_m, new_v = {}, {}, {}
        for n in TWIN_WEIGHTS:
            delta_w[n], new_m[n], new_v[n] = _adamw(weights[n], grad_w[n], given["m_" + n], given["v_" + n])
    return (loss, grad_x, *[grad_w[n] for n in TWIN_WEIGHTS], *[delta_w[n] for n in TWIN_WEIGHTS],
            *[new_m[n] for n in TWIN_WEIGHTS], *[new_v[n] for n in TWIN_WEIGHTS])
```

```python
import jax
import jax.numpy as jnp
from jax import lax
from jax.experimental import pallas as pl
from jax.experimental.pallas import tpu as pltpu

F32 = jnp.float32
BF16 = jnp.bfloat16

N_DEV = 8
EPS = 1e-6
A_HEADS = 4
HEAD_A = 128
B_PAIRS = 4
HEAD_B = 64
CHUNK = 64
SUB = 16
LANES = 128
NEG_BIG = -1e30

ADAM_LR = 0.001
ADAM_B1 = 0.9
ADAM_B2 = 0.999
ADAM_EPS = 1e-08
ADAM_WD = 0.01
ADAM_STEP = 10

VMEM_LIMIT = 56 * 1024 * 1024

NN = ((1,), (0,))
NT = ((1,), (1,))
TN = ((0,), (0,))


def _dot(a, b, dims):
    return lax.dot_general(a.astype(BF16), b.astype(BF16), (dims, ((), ())), preferred_element_type=F32)


def _split_bf16(x, parts):
    out = []
    r = x
    for i in range(parts):
        p = r.astype(BF16)
        out.append(p)
        if i + 1 < parts:
            r = r - p.astype(F32)
    return out


def _dot_exact_rhs(x, t, parts):
    acc = None
    for p in _split_bf16(x, parts):
        d = lax.dot_general(p, t, (NN, ((), ())), preferred_element_type=F32)
        acc = d if acc is None else acc + d
    return acc


def _dot_exact_lhs(t, x, parts):
    acc = None
    for p in _split_bf16(x, parts):
        d = lax.dot_general(t, p, (NN, ((), ())), preferred_element_type=F32)
        acc = d if acc is None else acc + d
    return acc


def _params(**kw):
    return pltpu.CompilerParams(vmem_limit_bytes=VMEM_LIMIT, **kw)


def _row_block(m, want):
    t = min(want, m)
    assert m % t == 0
    return t


def _mm_nn(a, w, add=None, *, name):
    m, k = a.shape
    nb, _, nc = w.shape
    tm = _row_block(m, 512)

    def body(*refs):
        if add is None:
            a_ref, w_ref, o_ref = refs
            o_ref[...] = _dot(a_ref[...], w_ref[...], NN)
        else:
            a_ref, w_ref, add_ref, o_ref = refs
            o_ref[...] = _dot(a_ref[...], w_ref[...], NN) + add_ref[...]

    in_specs = [pl.BlockSpec((tm, k), lambda i, j: (i, 0)), pl.BlockSpec((None, k, nc), lambda i, j: (j, 0, 0))]
    args = [a, w]
    if add is not None:
        in_specs.append(pl.BlockSpec((tm, nc), lambda i, j: (i, j)))
        args.append(add)
    return pl.pallas_call(
        body, name=name, grid=(m // tm, nb), in_specs=in_specs,
        out_specs=pl.BlockSpec((tm, nc), lambda i, j: (i, j)),
        out_shape=jax.ShapeDtypeStruct((m, nb * nc), F32), compiler_params=_params(),
    )(*args)


def _mm_nt(g, w, *, name):
    m = g.shape[0]
    nb, k, nc = w.shape
    tm = _row_block(m, 512)

    def body(g_ref, w_ref, o_ref):
        r = _dot(g_ref[...], w_ref[...], NT)

        @pl.when(pl.program_id(1) == 0)
        def _():
            o_ref[...] = r

        @pl.when(pl.program_id(1) != 0)
        def _():
            o_ref[...] += r

    return pl.pallas_call(
        body, name=name, grid=(m // tm, nb),
        in_specs=[pl.BlockSpec((tm, nc), lambda i, j: (i, j)), pl.BlockSpec((None, k, nc), lambda i, j: (j, 0, 0))],
        out_specs=pl.BlockSpec((tm, k), lambda i, j: (i, 0)),
        out_shape=jax.ShapeDtypeStruct((m, k), F32), compiler_params=_params(),
    )(g, w)


def _mm_tn(a, g, nb, *, name):
    m, k = a.shape
    nc = g.shape[1] // nb
    tm = _row_block(m, 512)

    def body(a_ref, g_ref, o_ref):
        r = _dot(a_ref[...], g_ref[...], TN)

        @pl.when(pl.program_id(1) == 0)
        def _():
            o_ref[...] = r

        @pl.when(pl.program_id(1) != 0)
        def _():
            o_ref[...] += r

    return pl.pallas_call(
        body, name=name, grid=(nb, m // tm),
        in_specs=[pl.BlockSpec((tm, k), lambda j, i: (i, 0)), pl.BlockSpec((tm, nc), lambda j, i: (i, j))],
        out_specs=pl.BlockSpec((None, k, nc), lambda j, i: (j, 0, 0)),
        out_shape=jax.ShapeDtypeStruct((nb, k, nc), F32), compiler_params=_params(),
    )(a, g)


def _rms(x, g):
    return x * lax.rsqrt(jnp.mean(x * x, axis=-1, keepdims=True) + EPS) * g


def _rms_bwd(x, g, dy):
    r = lax.rsqrt(jnp.mean(x * x, axis=-1, keepdims=True) + EPS)
    xh = x * r
    dxh = dy * g
    dx = r * (dxh - xh * jnp.mean(dxh * xh, axis=-1, keepdims=True))
    return dx, jnp.sum(dy * xh, axis=0, keepdims=True)


def _accumulate(ref, val):
    @pl.when(pl.program_id(0) == 0)
    def _():
        ref[...] = val

    @pl.when(pl.program_id(0) != 0)
    def _():
        ref[...] += val


def _norm_fwd(x, g, *, name):
    m, d = x.shape
    tm = _row_block(m, 512)

    def body(x_ref, g_ref, o_ref):
        o_ref[...] = _rms(x_ref[...], g_ref[...]).astype(BF16)

    return pl.pallas_call(
        body, name=name, grid=(m // tm,),
        in_specs=[pl.BlockSpec((tm, d), lambda i: (i, 0)), pl.BlockSpec((1, d), lambda i: (0, 0))],
        out_specs=pl.BlockSpec((tm, d), lambda i: (i, 0)),
        out_shape=jax.ShapeDtypeStruct((m, d), BF16), compiler_params=_params(),
    )(x, g)


def _norm_bwd(x, g, dy, add, *, name):
    m, d = x.shape
    tm = _row_block(m, 512)

    def body(x_ref, g_ref, dy_ref, add_ref, dx_ref, dg_ref):
        dx, dg = _rms_bwd(x_ref[...], g_ref[...], dy_ref[...])
        dx_ref[...] = dx + add_ref[...]
        _accumulate(dg_ref, dg)

    row = pl.BlockSpec((tm, d), lambda i: (i, 0))
    vec = pl.BlockSpec((1, d), lambda i: (0, 0))
    return pl.pallas_call(
        body, name=name, grid=(m // tm,), in_specs=[row, vec, row, row], out_specs=[row, vec],
        out_shape=[jax.ShapeDtypeStruct((m, d), F32), jax.ShapeDtypeStruct((1, d), F32)], compiler_params=_params(),
    )(x, g, dy, add)


def _silu(x):
    return x * jax.nn.sigmoid(x)


def _mix_block(oa, ob, ag, bg, ga, gb):
    parts = []
    for h in range(A_HEADS):
        o = oa[:, HEAD_A * h:HEAD_A * (h + 1)]
        parts.append(o * lax.rsqrt(jnp.mean(o * o, axis=-1, keepdims=True) + EPS))
    ya = jnp.concatenate(parts, axis=1) * ga * _silu(ag)
    low = lax.broadcasted_iota(jnp.int32, (1, LANES), 1) < HEAD_B
    parts = []
    for p in range(B_PAIRS):
        o = ob[:, LANES * p:LANES * (p + 1)]
        sq = o * o
        s_lo = jnp.sum(jnp.where(low, sq, 0.0), axis=-1, keepdims=True)
        s_hi = jnp.sum(jnp.where(low, 0.0, sq), axis=-1, keepdims=True)
        r = jnp.where(low, lax.rsqrt(s_lo * (1.0 / HEAD_B) + EPS), lax.rsqrt(s_hi * (1.0 / HEAD_B) + EPS))
        parts.append(o * r)
    yb = jnp.concatenate(parts, axis=1) * gb * _silu(bg)
    return jnp.concatenate([ya, yb], axis=1)


_GATE_A_BLOCK = 3
_GATE_B_BLOCK = 7


def _mix_fwd(oa, ob, proj, ga, gb, *, name):
    m, w = oa.shape
    tm = _row_block(m, 512)

    def body(oa_ref, ob_ref, ag_ref, bg_ref, ga_ref, gb_ref, y_ref):
        y_ref[...] = _mix_block(oa_ref[...], ob_ref[...], ag_ref[...], bg_ref[...], ga_ref[...], gb_ref[...]).astype(BF16)

    row = pl.BlockSpec((tm, w), lambda i: (i, 0))
    vec = pl.BlockSpec((1, w), lambda i: (0, 0))
    return pl.pallas_call(
        body, name=name, grid=(m // tm,),
        in_specs=[row, row, pl.BlockSpec((tm, w), lambda i: (i, _GATE_A_BLOCK)),
                  pl.BlockSpec((tm, w), lambda i: (i, _GATE_B_BLOCK)), vec, vec],
        out_specs=pl.BlockSpec((tm, 2 * w), lambda i: (i, 0)),
        out_shape=jax.ShapeDtypeStruct((m, 2 * w), BF16), compiler_params=_params(),
    )(oa, ob, proj, proj, ga, gb)


def _mix_bwd(dy, oa, ob, proj, ga, gb, *, name):
    m, w = oa.shape
    tm = _row_block(m, 256)

    def body(dy_ref, oa_ref, ob_ref, ag_ref, bg_ref, ga_ref, gb_ref, doa_ref, dob_ref, dag_ref, dbg_ref, dga_ref, dgb_ref):
        _, vjp = jax.vjp(_mix_block, oa_ref[...], ob_ref[...], ag_ref[...], bg_ref[...], ga_ref[...], gb_ref[...])
        doa, dob, dag, dbg, dga, dgb = vjp(dy_ref[...])
        doa_ref[...] = doa
        dob_ref[...] = dob
        dag_ref[...] = dag
        dbg_ref[...] = dbg
        _accumulate(dga_ref, dga)
        _accumulate(dgb_ref, dgb)

    row = pl.BlockSpec((tm, w), lambda i: (i, 0))
    vec = pl.BlockSpec((1, w), lambda i: (0, 0))
    big = jax.ShapeDtypeStruct((m, w), F32)
    small = jax.ShapeDtypeStruct((1, w), F32)
    return pl.pallas_call(
        body, name=name, grid=(m // tm,),
        in_specs=[pl.BlockSpec((tm, 2 * w), lambda i: (i, 0)), row, row,
                  pl.BlockSpec((tm, w), lambda i: (i, _GATE_A_BLOCK)), pl.BlockSpec((tm, w), lambda i: (i, _GATE_B_BLOCK)), vec, vec],
        out_specs=[row, row, row, row, vec, vec], out_shape=[big, big, big, big, small, small], compiler_params=_params(),
    )(dy, oa, ob, proj, proj, ga, gb)


def _ple_block(h1, gp, pp, gpost):
    return h1 + jax.nn.sigmoid(gp) * _rms(pp, gpost)


def _ple_fwd(h1, gp, pp, gpost, *, name):
    m, d = h1.shape
    tm = _row_block(m, 512)

    def body(h_ref, gp_ref, pp_ref, g_ref, o_ref):
        o_ref[...] = _ple_block(h_ref[...], gp_ref[...], pp_ref[...], g_ref[...])

    row = pl.BlockSpec((tm, d), lambda i: (i, 0))
    vec = pl.BlockSpec((1, d), lambda i: (0, 0))
    return pl.pallas_call(
        body, name=name, grid=(m // tm,), in_specs=[row, row, row, vec], out_specs=row,
        out_shape=jax.ShapeDtypeStruct((m, d), F32), compiler_params=_params(),
    )(h1, gp, pp, gpost)


def _ple_bwd(dh2, gp, pp, gpost, *, name):
    m, d = dh2.shape
    tm = _row_block(m, 512)

    def body(dh_ref, gp_ref, pp_ref, g_ref, dgp_ref, dpp_ref, dg_ref):
        dh = dh_ref[...]
        gate = jax.nn.sigmoid(gp_ref[...])
        pe = _rms(pp_ref[...], g_ref[...])
        dgp_ref[...] = dh * pe * (gate * (1.0 - gate))
        dpp, dg = _rms_bwd(pp_ref[...], g_ref[...], dh * gate)
        dpp_ref[...] = dpp
        _accumulate(dg_ref, dg)

    row = pl.BlockSpec((tm, d), lambda i: (i, 0))
    vec = pl.BlockSpec((1, d), lambda i: (0, 0))
    big = jax.ShapeDtypeStruct((m, d), F32)
    return pl.pallas_call(
        body, name=name, grid=(m // tm,), in_specs=[row, row, row, vec], out_specs=[row, row, vec],
        out_shape=[big, big, jax.ShapeDtypeStruct((1, d), F32)], compiler_params=_params(),
    )(dh2, gp, pp, gpost)


def _loss_head(h, gf, target, *, name):
    m, d = h.shape
    tm = _row_block(m, 512)

    def body(h_ref, g_ref, t_ref, loss_ref, dh_ref, dg_ref):
        x = h_ref[...]
        g = g_ref[...]
        err = _rms(x, g) - t_ref[...]
        part = 0.5 * jnp.sum(jnp.mean(err * err, axis=-1, keepdims=True), axis=0, keepdims=True)
        dx, dg = _rms_bwd(x, g, err * (1.0 / d))
        dh_ref[...] = dx
        _accumulate(dg_ref, dg)
        _accumulate(loss_ref, jnp.broadcast_to(part, (8, LANES)))

    row = pl.BlockSpec((tm, d), lambda i: (i, 0))
    vec = pl.BlockSpec((1, d), lambda i: (0, 0))
    return pl.pallas_call(
        body, name=name, grid=(m // tm,), in_specs=[row, vec, row],
        out_specs=[pl.BlockSpec((8, LANES), lambda i: (0, 0)), row, vec],
        out_shape=[jax.ShapeDtypeStruct((8, LANES), F32), jax.ShapeDtypeStruct((m, d), F32), jax.ShapeDtypeStruct((1, d), F32)],
        compiler_params=_params(),
    )(h, gf, target)


def _hgrn_pre(aq, af, lbv):
    sq = jax.nn.sigmoid(aq)
    sg = jax.nn.sigmoid(-af)
    k = (1.0 - lbv) * sg
    return aq * sq, sq, sg, k, jnp.log1p(-k)


def _tri(n, cmp):
    r = lax.broadcasted_iota(jnp.int32, (n, n), 0)
    c = lax.broadcasted_iota(jnp.int32, (n, n), 1)
    return cmp(r, c).astype(BF16)


def _hgrn_intra(q, k, b, b_scr):
    col = lax.broadcasted_iota(jnp.int32, (SUB, CHUNK), 1)
    rows = [jnp.zeros((SUB, CHUNK), F32)]
    facs = [None]
    for i in range(1, CHUNK // SUB):
        r = b_scr[SUB * i - 1:SUB * i, :]
        fq = jnp.exp(b[SUB * i:SUB * (i + 1)] - r)
        fk = jnp.exp(jnp.minimum(r - b, 0.0))
        qs = q[SUB * i:SUB * (i + 1)] * fq
        ks = k * fk
        rows.append(jnp.where(col < SUB * i, _dot(qs, ks, NT), 0.0))
        facs.append((fq, fk, qs, ks))
    return jnp.concatenate(rows, axis=0), facs


def _hgrn_diag_weights(i, k, b, q_scr, b_scr):
    sub = lax.broadcasted_iota(jnp.int32, (SUB, LANES), 0)
    ki = k[SUB * i:SUB * (i + 1)]
    bi = b[SUB * i:SUB * (i + 1)]
    es, ws = [], []
    for tau in range(SUB):
        t = SUB * i + tau
        e = jnp.exp(jnp.where(sub <= tau, b_scr[t:t + 1, :] - bi, NEG_BIG))
        es.append(e)
        ws.append((q_scr[t:t + 1, :] * ki) * e)
    return es, jnp.concatenate(ws, axis=0)


def _hgrn_fwd(proj, lb_row, *, name):
    s = proj.shape[0]
    tb = _row_block(s, 512)
    nch = tb // CHUNK

    def body(aq_ref, af_ref, ai_ref, lb_ref, o_ref, st_ref, s_scr, q_scr, b_scr, od_scr):
        @pl.when(pl.program_id(1) == 0)
        def _():
            s_scr[...] = jnp.zeros_like(s_scr)

        lbv = lb_ref[...]
        tril = _tri(CHUNK, lambda r, c: r >= c)
        ones = jnp.ones((LANES, LANES), BF16)

        def chunk(ci, carry):
            r0 = pl.multiple_of(ci * CHUNK, CHUNK)
            q, _, _, k, g = _hgrn_pre(aq_ref[pl.ds(r0, CHUNK), :], af_ref[pl.ds(r0, CHUNK), :], lbv)
            v = ai_ref[pl.ds(r0, CHUNK), :]
            st0 = s_scr[...]
            st_ref[ci] = st0
            b = _dot_exact_lhs(tril, g, 3)
            b_scr[...] = b
            q_scr[...] = q
            bl = b_scr[CHUNK - 1:CHUNK, :]
            o = _dot(q * jnp.exp(b), st0, NT)
            a, _ = _hgrn_intra(q, k, b, b_scr)
            o = o + _dot(a, v, NN)
            for i in range(CHUNK // SUB):
                _, wst = _hgrn_diag_weights(i, k, b, q_scr, b_scr)
                rep = _dot(wst, ones, NN)
                vi = v[SUB * i:SUB * (i + 1)]
                for tau in range(SUB):
                    t = SUB * i + tau
                    od_scr[t:t + 1, :] = jnp.sum(rep[SUB * tau:SUB * (tau + 1)] * vi, axis=0, keepdims=True)
            o_ref[pl.ds(r0, CHUNK), :] = o + od_scr[...]
            s_scr[...] = st0 * jnp.exp(bl) + _dot(v, k * jnp.exp(bl - b), TN)
            return carry

        lax.fori_loop(0, nch, chunk, 0)

    def col(block0):
        return pl.BlockSpec((tb, HEAD_A), lambda h, t: (t, block0 + h))

    return pl.pallas_call(
        body, name=name, grid=(A_HEADS, s // tb),
        in_specs=[col(0), col(A_HEADS), col(2 * A_HEADS), pl.BlockSpec((1, HEAD_A), lambda h, t: (0, h))],
        out_specs=[pl.BlockSpec((tb, HEAD_A), lambda h, t: (t, h)),
                   pl.BlockSpec((None, nch, HEAD_A, HEAD_A), lambda h, t: (h, t, 0, 0))],
        out_shape=[jax.ShapeDtypeStruct((s, A_HEADS * HEAD_A), F32),
                   jax.ShapeDtypeStruct((A_HEADS, s // CHUNK, HEAD_A, HEAD_A), F32)],
        scratch_shapes=[pltpu.VMEM((HEAD_A, HEAD_A), F32), pltpu.VMEM((CHUNK, HEAD_A), F32),
                        pltpu.VMEM((CHUNK, HEAD_A), F32), pltpu.VMEM((CHUNK, HEAD_A), F32)],
        compiler_params=_params(),
    )(proj, proj, proj, lb_row)


def _hgrn_bwd(proj, lb_row, states, do_a, *, name):
    s = proj.shape[0]
    tb = _row_block(s, 512)
    nch = tb // CHUNK
    nblk = s // tb

    def body(aq_ref, af_ref, ai_ref, lb_ref, st_ref, do_ref, daq_ref, daf_ref, dai_ref, dlb_ref,
             ds_scr, q_scr, b_scr, dq_scr, do_scr):
        @pl.when(pl.program_id(1) == 0)
        def _():
            ds_scr[...] = jnp.zeros_like(ds_scr)
            dlb_ref[...] = jnp.zeros_like(dlb_ref)

        lbv = lb_ref[...]
        tril = _tri(CHUNK, lambda r, c: r >= c)
        triu = _tri(CHUNK, lambda r, c: c >= r)
        ones = jnp.ones((LANES, LANES), BF16)
        col = lax.broadcasted_iota(jnp.int32, (SUB, CHUNK), 1)
        row = lax.broadcasted_iota(jnp.int32, (CHUNK, LANES), 0)
        nsub = CHUNK // SUB

        def chunk(cj, carry):
            ci = nch - 1 - cj
            r0 = pl.multiple_of(ci * CHUNK, CHUNK)
            aq = aq_ref[pl.ds(r0, CHUNK), :]
            q, sq, sg, k, g = _hgrn_pre(aq, af_ref[pl.ds(r0, CHUNK), :], lbv)
            v = ai_ref[pl.ds(r0, CHUNK), :]
            do = do_ref[pl.ds(r0, CHUNK), :]
            st0 = st_ref[ci]
            dst1 = ds_scr[...]
            b = _dot_exact_lhs(tril, g, 3)
            b_scr[...] = b
            q_scr[...] = q
            do_scr[...] = do
            bl = b_scr[CHUNK - 1:CHUNK, :]
            eb = jnp.exp(b)
            ebl = jnp.exp(bl)
            ekd = jnp.exp(bl - b)
            kd = k * ekd
            a, facs = _hgrn_intra(q, k, b, b_scr)

            da = _dot(do, v, NT)
            dv = _dot(a, do, TN) + _dot(kd, dst1, NT)
            dq = _dot(do, st0, NN) * eb
            dk_inter = _dot(v, dst1, NN) * ekd
            dk = dk_inter
            dq_rows = [jnp.zeros((SUB, LANES), F32)]
            for i in range(1, nsub):
                fq, fk, qs, ks = facs[i]
                da_i = jnp.where(col < SUB * i, da[SUB * i:SUB * (i + 1)], 0.0)
                dq_rows.append(_dot(da_i, ks, NN) * fq)
                dk = dk + _dot(da_i, qs, TN) * fk
            dq = dq + jnp.concatenate(dq_rows, axis=0)

            dv_rows, dk_rows = [], []
            for i in range(nsub):
                es, wst = _hgrn_diag_weights(i, k, b, q_scr, b_scr)
                ki = k[SUB * i:SUB * (i + 1)]
                vi = v[SUB * i:SUB * (i + 1)]
                rep = _dot(wst, ones, NN)
                mst = jnp.concatenate([do_scr[SUB * i + tau:SUB * i + tau + 1, :] * vi for tau in range(SUB)], axis=0)
                drep = _dot(mst, ones, NN)
                dvi = jnp.zeros((SUB, LANES), F32)
                dki = jnp.zeros((SUB, LANES), F32)
                for tau in range(SUB):
                    t = SUB * i + tau
                    gt = drep[SUB * tau:SUB * (tau + 1)] * es[tau]
                    dvi = dvi + rep[SUB * tau:SUB * (tau + 1)] * do_scr[t:t + 1, :]
                    dki = dki + gt * q_scr[t:t + 1, :]
                    dq_scr[t:t + 1, :] = jnp.sum(gt * ki, axis=0, keepdims=True)
                dv_rows.append(dvi)
                dk_rows.append(dki)
            dv = dv + jnp.concatenate(dv_rows, axis=0)
            dk = dk + jnp.concatenate(dk_rows, axis=0)
            dq = dq + dq_scr[...]

            ds_scr[...] = dst1 * ebl + _dot(do, q * eb, TN)
            last = jnp.sum(k * dk_inter, axis=0, keepdims=True) + ebl * jnp.sum(st0 * dst1, axis=0, keepdims=True)
            db = q * dq - k * dk + jnp.where(row == CHUNK - 1, last, 0.0)
            dg = _dot_exact_lhs(triu, db, 3)

            dkt = dk - dg / (1.0 - k)
            daq_ref[pl.ds(r0, CHUNK), :] = dq * (sq * (1.0 + aq * (1.0 - sq)))
            daf_ref[pl.ds(r0, CHUNK), :] = dkt * (1.0 - lbv) * (-(sg * (1.0 - sg)))
            dai_ref[pl.ds(r0, CHUNK), :] = dv
            dlb_ref[...] += jnp.sum(dkt * (-sg), axis=0, keepdims=True)
            return carry

        lax.fori_loop(0, nch, chunk, 0)

    def col_in(block0):
        return pl.BlockSpec((tb, HEAD_A), lambda h, t: (nblk - 1 - t, block0 + h))

    head_col = pl.BlockSpec((tb, HEAD_A), lambda h, t: (nblk - 1 - t, h))
    vec = pl.BlockSpec((1, HEAD_A), lambda h, t: (0, h))
    big = jax.ShapeDtypeStruct((s, A_HEADS * HEAD_A), F32)
    return pl.pallas_call(
        body, name=name, grid=(A_HEADS, nblk),
        in_specs=[col_in(0), col_in(A_HEADS), col_in(2 * A_HEADS), vec,
                  pl.BlockSpec((None, nch, HEAD_A, HEAD_A), lambda h, t: (h, nblk - 1 - t, 0, 0)), head_col],
        out_specs=[head_col, head_col, head_col, vec],
        out_shape=[big, big, big, jax.ShapeDtypeStruct((1, A_HEADS * HEAD_A), F32)],
        scratch_shapes=[pltpu.VMEM((HEAD_A, HEAD_A), F32), pltpu.VMEM((CHUNK, HEAD_A), F32),
                        pltpu.VMEM((CHUNK, HEAD_A), F32), pltpu.VMEM((CHUNK, HEAD_A), F32),
                        pltpu.VMEM((CHUNK, HEAD_A), F32)],
        compiler_params=_params(),
    )(proj, proj, proj, lb_row, states, do_a)


_BQ_BLOCK = 16
_BK_BLOCK = 20
_BV_BLOCK = 24
SB_SCALE = HEAD_B ** -0.5
SB_T = 128


def _sb_tile(qh, kb, mask, carry, upper):
    z = _dot(qh, kb, NT) * SB_SCALE
    l1p = jnp.log1p(jnp.exp(-jnp.abs(z)))
    lm = jnp.where(mask, -(jnp.maximum(z, 0.0) + l1p), 0.0)
    a = jnp.minimum(z, 0.0) - l1p
    rest = _dot_exact_rhs(lm, upper, 2) + carry
    w = jnp.where(mask, jnp.exp(a + rest), 0.0)
    return a, w, lm


def _sb_fwd(proj, *, name):
    s = proj.shape[0]
    t = min(SB_T, s)
    nq = s // t
    assert nq <= LANES

    def body(q_ref, k_ref, v_ref, o_ref, c_ref):
        qi = pl.program_id(1)
        lane = lax.broadcasted_iota(jnp.int32, (1, LANES), 1)
        low = lane < HEAD_B
        q = q_ref[...]
        qh = (jnp.where(low, q, 0.0).astype(BF16), jnp.where(low, 0.0, q).astype(BF16))
        upper = _tri(t, lambda r, c: r > c)
        trow = qi * t + lax.broadcasted_iota(jnp.int32, (t, 1), 0)
        scol0 = lax.broadcasted_iota(jnp.int32, (1, t), 1)

        o_ref[...] = jnp.zeros_like(o_ref)
        c_ref[...] = jnp.zeros_like(c_ref)

        def kblock(jj, c):
            j = qi - jj
            s0 = pl.multiple_of(j * t, t)
            kb = k_ref[pl.ds(s0, t), :].astype(BF16)
            vb = v_ref[pl.ds(s0, t), :]
            mask = (s0 + scol0) < trow
            c_new = []
            acc = jnp.zeros((t, LANES), F32)
            for h in range(2):
                _, w, lm = _sb_tile(qh[h], kb, mask, c[h], upper)
                c_ref[h] = jnp.where(lane == j, c[h], c_ref[h])
                c_new.append(c[h] + jnp.sum(lm, axis=1, keepdims=True))
                vh = jnp.where(low, vb, 0.0) if h == 0 else jnp.where(low, 0.0, vb)
                acc = acc + _dot(w, vh, NN)
            o_ref[...] += acc
            return tuple(c_new)

        zc = jnp.zeros((t, 1), F32)
        lax.fori_loop(0, qi + 1, kblock, (zc, zc))

    def whole(block0):
        return pl.BlockSpec((s, LANES), lambda p, i: (0, block0 + p))

    return pl.pallas_call(
        body, name=name, grid=(B_PAIRS, nq),
        in_specs=[pl.BlockSpec((t, LANES), lambda p, i: (i, _BQ_BLOCK + p)), whole(_BK_BLOCK), whole(_BV_BLOCK)],
        out_specs=[pl.BlockSpec((t, LANES), lambda p, i: (i, p)),
                   pl.BlockSpec((None, 2, t, LANES), lambda p, i: (p, 0, i, 0))],
        out_shape=[jax.ShapeDtypeStruct((s, B_PAIRS * LANES), F32), jax.ShapeDtypeStruct((B_PAIRS, 2, s, LANES), F32)],
        compiler_params=_params(),
    )(proj, proj, proj)


def _sb_bwd(proj, carries, do_b, *, name):
    s = proj.shape[0]
    t = min(SB_T, s)
    nq = s // t

    def body(q_ref, k_ref, v_ref, c_ref, do_ref, dq_ref, dk_ref, dv_ref):
        qi = pl.program_id(1)

        @pl.when(qi == 0)
        def _():
            dk_ref[...] = jnp.zeros_like(dk_ref)
            dv_ref[...] = jnp.zeros_like(dv_ref)

        lane = lax.broadcasted_iota(jnp.int32, (1, LANES), 1)
        low = lane < HEAD_B

        def heads(x):
            return jnp.where(low, x, 0.0).astype(BF16), jnp.where(low, 0.0, x).astype(BF16)

        qh = heads(q_ref[...])
        doh = heads(do_ref[...])
        cst = (c_ref[0], c_ref[1])
        upper = _tri(t, lambda r, c: r > c)
        lower = _tri(t, lambda r, c: r < c)
        trow = qi * t + lax.broadcasted_iota(jnp.int32, (t, 1), 0)
        scol0 = lax.broadcasted_iota(jnp.int32, (1, t), 1)

        def kblock(j, carry):
            ec, dq = carry
            s0 = pl.multiple_of(j * t, t)
            kf = k_ref[pl.ds(s0, t), :]
            kb = kf.astype(BF16)
            kh = heads(kf)
            vb = v_ref[pl.ds(s0, t), :].astype(BF16)
            mask = (s0 + scol0) < trow
            ec_new = []
            dk_add = jnp.zeros((t, LANES), F32)
            dv_add = jnp.zeros((t, LANES), F32)
            for h in range(2):
                cin = jnp.sum(jnp.where(lane == j, cst[h], 0.0), axis=1, keepdims=True)
                a, w, _ = _sb_tile(qh[h], kb, mask, cin, upper)
                e = w * _dot(doh[h], vb, NT)
                before = ec[h] + _dot_exact_rhs(e, lower, 2)
                beta = jnp.exp(a)
                dz = jnp.where(mask, (e - beta * (e + before)) * SB_SCALE, 0.0)
                ec_new.append(ec[h] + jnp.sum(e, axis=1, keepdims=True))
                dq = dq + _dot(dz, kh[h], NN)
                dk_add = dk_add + _dot(dz, qh[h], TN)
                dv_add = dv_add + _dot(w, doh[h], TN)
            dk_ref[pl.ds(s0, t), :] += dk_add
            dv_ref[pl.ds(s0, t), :] += dv_add
            return tuple(ec_new), dq

        zc = jnp.zeros((t, 1), F32)
        _, dq = lax.fori_loop(0, qi + 1, kblock, ((zc, zc), jnp.zeros((t, LANES), F32)))
        dq_ref[...] = dq

    def whole_in(block0):
        return pl.BlockSpec((s, LANES), lambda p, i: (0, block0 + p))

    blk = pl.BlockSpec((t, LANES), lambda p, i: (i, p))
    whole_out = pl.BlockSpec((s, LANES), lambda p, i: (0, p))
    big = jax.ShapeDtypeStruct((s, B_PAIRS * LANES), F32)
    return pl.pallas_call(
        body, name=name, grid=(B_PAIRS, nq),
        in_specs=[pl.BlockSpec((t, LANES), lambda p, i: (i, _BQ_BLOCK + p)), whole_in(_BK_BLOCK), whole_in(_BV_BLOCK),
                  pl.BlockSpec((None, 2, t, LANES), lambda p, i: (p, 0, i, 0)), blk],
        out_specs=[blk, whole_out, whole_out], out_shape=[big, big, big], compiler_params=_params(),
    )(proj, proj, proj, carries, do_b)


def _my_id():
    return 4 * lax.axis_index("x") + 2 * lax.axis_index("y") + lax.axis_index("c")


def _mesh_pos(d):
    return (d // 4, (d // 2) % 2, d % 2)


def _exchange(arrays, scatter, *, name):
    n = len(arrays)
    any_spec = pl.BlockSpec(memory_space=pl.ANY)

    def body(*refs):
        ins, outs = refs[:n], refs[n:2 * n]
        send_sems, recv_sems, local_sems = refs[2 * n:]
        me = _my_id()

        def src(i, to):
            return ins[i].at[to] if scatter else ins[i]

        local = [pltpu.make_async_copy(src(i, me), outs[i].at[me], local_sems.at[i]) for i in range(n)]
        for cp in local:
            cp.start()
        sent = []
        for step in range(1, N_DEV):
            to = (me + step) % N_DEV
            for i in range(n):
                cp = pltpu.make_async_remote_copy(
                    src_ref=src(i, to), dst_ref=outs[i].at[me],
                    send_sem=send_sems.at[step - 1, i], recv_sem=recv_sems.at[step - 1, i],
                    device_id=_mesh_pos(to), device_id_type=pl.DeviceIdType.MESH)
                cp.start()
                sent.append(cp)
        for step in range(1, N_DEV):
            frm = (me + N_DEV - step) % N_DEV
            for i in range(n):
                pltpu.make_async_remote_copy(
                    src_ref=src(i, frm), dst_ref=outs[i].at[frm],
                    send_sem=send_sems.at[step - 1, i], recv_sem=recv_sems.at[step - 1, i],
                    device_id=_mesh_pos(frm), device_id_type=pl.DeviceIdType.MESH).wait_recv()
        for cp in sent:
            cp.wait_send()
        for cp in local:
            cp.wait()

    def out_shape(a):
        return jax.ShapeDtypeStruct(((N_DEV,) + a.shape[1:]) if scatter else ((N_DEV,) + a.shape), a.dtype)

    return pl.pallas_call(
        body, name=name, in_specs=[any_spec] * n, out_specs=[any_spec] * n,
        out_shape=[out_shape(a) for a in arrays],
        scratch_shapes=[pltpu.SemaphoreType.DMA((N_DEV - 1, n)), pltpu.SemaphoreType.DMA((N_DEV - 1, n)),
                        pltpu.SemaphoreType.DMA((n,))],
    )(*arrays)


def _adamw_math(w, m, v, g):
    m2 = ADAM_B1 * m + (1.0 - ADAM_B1) * g
    v2 = ADAM_B2 * v + (1.0 - ADAM_B2) * (g * g)
    m_hat = m2 / (1.0 - ADAM_B1 ** ADAM_STEP)
    v_hat = v2 / (1.0 - ADAM_B2 ** ADAM_STEP)
    delta = -ADAM_LR * (m_hat / (jnp.sqrt(v_hat) + ADAM_EPS) + ADAM_WD * w)
    return delta, m2, v2


def _slot_sum(ref):
    g = ref[0]
    for d in range(1, N_DEV):
        g = g + ref[d]
    return g


def _adamw(w, m, v, slots, *, name):
    r, c = w.shape
    tr = _row_block(r, 256)

    def body(w_ref, m_ref, v_ref, s_ref, g_ref, d_ref, m2_ref, v2_ref):
        g = _slot_sum(s_ref)
        delta, m2, v2 = _adamw_math(w_ref[...], m_ref[...], v_ref[...], g)
        g_ref[...] = g
        d_ref[...] = delta
        m2_ref[...] = m2
        v2_ref[...] = v2

    row = pl.BlockSpec((tr, c), lambda i: (i, 0))
    shp = jax.ShapeDtypeStruct((r, c), F32)
    return pl.pallas_call(
        body, name=name, grid=(r // tr,),
        in_specs=[row, row, row, pl.BlockSpec((N_DEV, tr, c), lambda i: (0, i, 0))],
        out_specs=[row, row, row, row], out_shape=[shp, shp, shp, shp], compiler_params=_params(),
    )(w, m, v, slots)


def _layer_softmax(l_ref):
    l0 = l_ref[0:1, :]
    l1 = l_ref[1:2, :]
    mx = jnp.maximum(l0, l1)
    e0 = jnp.exp(l0 - mx)
    e1 = jnp.exp(l1 - mx)
    return e0 / (e0 + e1), e1 / (e0 + e1)


def _adamw_lb(w, m, v, slots, *, name):
    def body(w_ref, m_ref, v_ref, s_ref, g_ref, d_ref, m2_ref, v2_ref):
        sm0, sm1 = _layer_softmax(w_ref)
        dl1 = _slot_sum(s_ref) * (sm0 * sm1)
        for row, g in ((0, -dl1), (1, dl1)):
            sl = slice(row, row + 1)
            delta, m2, v2 = _adamw_math(w_ref[sl, :], m_ref[sl, :], v_ref[sl, :], g)
            g_ref[sl, :] = g
            d_ref[sl, :] = delta
            m2_ref[sl, :] = m2
            v2_ref[sl, :] = v2

    shp = jax.ShapeDtypeStruct(w.shape, F32)
    return pl.pallas_call(body, name=name, out_shape=[shp, shp, shp, shp])(w, m, v, slots)


def _lower_bound_1(lb_logits, *, name):
    def body(l_ref, o_ref):
        sm0, sm1 = _layer_softmax(l_ref)
        o_ref[...] = (sm0 + sm1) - sm0

    return pl.pallas_call(body, name=name, out_shape=jax.ShapeDtypeStruct((1, lb_logits.shape[1]), F32))(lb_logits)


_SMALL = ("norm_mix", "a_out_norm", "b_out_norm", "ple_gate_norm", "ple_post_norm")


def _pack_rows(vectors):
    offs, rows = [], 0
    for vec in vectors:
        offs.append(rows)
        rows += vec.shape[0] // LANES
    flat = jnp.concatenate([vec.reshape(-1, LANES) for vec in vectors], axis=0)
    pad = (-rows) % 8
    if pad:
        flat = jnp.concatenate([flat, jnp.zeros((pad, LANES), F32)], axis=0)
    return flat, offs


def kernel(x, p, norm_mix, w_in, a_out_norm, b_out_norm, w_out, lb_logits, ple_gate_norm, w_ple_gate, w_ple_proj, ple_post_norm, final_norm, loss_target, m_norm_mix, m_w_in, m_a_out_norm, m_b_out_norm, m_w_out, m_lb_logits, m_ple_gate_norm, m_w_ple_gate, m_w_ple_proj, m_ple_post_norm, m_final_norm, v_norm_mix, v_w_in, v_a_out_norm, v_b_out_norm, v_w_out, v_lb_logits, v_ple_gate_norm, v_w_ple_gate, v_w_ple_proj, v_ple_post_norm, v_final_norm):
    depth = w_in.shape[0]
    assert depth == 2
    s, d = x.shape[1], x.shape[2]
    h = x.reshape(s, d)
    target = loss_target.reshape(s, d)

    gathered = _exchange([w_in.astype(BF16), w_out.astype(BF16), w_ple_gate.astype(BF16), w_ple_proj.astype(BF16)],
                         scatter=False, name="gather_weights")
    win_g, wout_g, wpg_g, wpp_g = gathered
    lb1 = _lower_bound_1(lb_logits, name="lower_bound")
    lbs = [jnp.zeros_like(lb1), lb1]

    saved = []
    for l in range(depth):
        win = win_g[:, l]
        wout = wout_g[:, l].reshape(1, d, d)
        wpg = wpg_g[:, l].reshape(1, d, d)
        wpp = wpp_g[:, l]
        u = _norm_fwd(h, norm_mix[l:l + 1], name="norm_mix")
        proj = _mm_nn(u, win, name="mm_in")
        o_a, states = _hgrn_fwd(proj, lbs[l], name="hgrn_fwd")
        o_b, carries = _sb_fwd(proj, name="sb_fwd")
        y = _mix_fwd(o_a, o_b, proj, a_out_norm[l:l + 1], b_out_norm[l:l + 1], name="mix_fwd")
        h1 = _mm_nn(y, wout, add=h, name="mm_out")
        r1 = _norm_fwd(h1, ple_gate_norm[l:l + 1], name="norm_gate")
        gp = _mm_nn(r1, wpg, name="mm_gate")
        pl_in = p[l].reshape(s, p.shape[-1])
        pp = _mm_nn(pl_in, wpp, name="mm_ple")
        h2 = _ple_fwd(h1, gp, pp, ple_post_norm[l:l + 1], name="ple_fwd")
        saved.append((h, u, proj, o_a, states, o_b, carries, y, h1, r1, gp, pp, pl_in, win, wout, wpg, wpp))
        h = h2

    loss_part, dh, dg_final = _loss_head(h, final_norm.reshape(1, d), target, name="loss_head")

    grads = [None] * depth
    for l in reversed(range(depth)):
        h0, u, proj, o_a, states, o_b, carries, y, h1, r1, gp, pp, pl_in, win, wout, wpg, wpp = saved[l]
        dgp, dpp, dg_post = _ple_bwd(dh, gp, pp, ple_post_norm[l:l + 1], name="ple_bwd")
        dwpp = _mm_tn(pl_in, dpp, N_DEV, name="mm_ple_dw")
        dwpg = _mm_tn(r1, dgp, 1, name="mm_gate_dw")
        dr1 = _mm_nt(dgp, wpg, name="mm_gate_dx")
        dh1, dg_pg = _norm_bwd(h1, ple_gate_norm[l:l + 1], dr1, dh, name="norm_gate_bwd")
        dwout = _mm_tn(y, dh1, 1, name="mm_out_dw")
        dy = _mm_nt(dh1, wout, name="mm_out_dx")
        do_a, do_b, dag, dbg, dg_a, dg_b = _mix_bwd(dy, o_a, o_b, proj, a_out_norm[l:l + 1], b_out_norm[l:l + 1], name="mix_bwd")
        dbq, dbk, dbv = _sb_bwd(proj, carries, do_b, name="sb_bwd")
        daq, daf, dai, dlb = _hgrn_bwd(proj, lbs[l], states, do_a, name="hgrn_bwd")
        dproj = jnp.concatenate([daq, daf, dai, dag, dbq, dbk, dbv, dbg], axis=1)
        dwin = _mm_tn(u, dproj, N_DEV, name="mm_in_dw")
        du = _mm_nt(dproj, win, name="mm_in_dx")
        dh, dg_mix = _norm_bwd(h0, norm_mix[l:l + 1], du, dh1, name="norm_mix_bwd")
        grads[l] = dict(w_in=dwin, w_out=dwout.reshape(N_DEV, d // N_DEV, d), w_ple_gate=dwpg.reshape(N_DEV, d // N_DEV, d),
                        w_ple_proj=dwpp, norm_mix=dg_mix, a_out_norm=dg_a, b_out_norm=dg_b, ple_gate_norm=dg_pg,
                        ple_post_norm=dg_post, lb=dlb)

    grad_x = dh.reshape(x.shape)
    loss = lax.psum(loss_part[0, 0], ("x", "y", "c"))

    big = ("w_in", "w_out", "w_ple_gate", "w_ple_proj")
    stacked = [jnp.stack([grads[l][n] for l in range(depth)], axis=1) for n in big]
    slots = _exchange(stacked, scatter=True, name="scatter_grads")
    results = {}
    for n, sl, w, m, v in zip(big, slots, (w_in, w_out, w_ple_gate, w_ple_proj), (m_w_in, m_w_out, m_w_ple_gate, m_w_ple_proj),
                              (v_w_in, v_w_out, v_w_ple_gate, v_w_ple_proj)):
        c = w.shape[-1]
        outs = _adamw(w.reshape(-1, c), m.reshape(-1, c), v.reshape(-1, c), sl.reshape(N_DEV, -1, c), name="adamw_" + n)
        results[n] = [o.reshape(w.shape) for o in outs]

    small_w = dict(norm_mix=norm_mix, a_out_norm=a_out_norm, b_out_norm=b_out_norm, ple_gate_norm=ple_gate_norm,
                   ple_post_norm=ple_post_norm)
    small_m = dict(norm_mix=m_norm_mix, a_out_norm=m_a_out_norm, b_out_norm=m_b_out_norm, ple_gate_norm=m_ple_gate_norm,
                   ple_post_norm=m_ple_post_norm)
    small_v = dict(norm_mix=v_norm_mix, a_out_norm=v_a_out_norm, b_out_norm=v_b_out_norm, ple_gate_norm=v_ple_gate_norm,
                   ple_post_norm=v_ple_post_norm)
    g_vecs = [jnp.concatenate([grads[l][n].reshape(-1) for l in range(depth)]) for n in _SMALL] + [dg_final.reshape(-1)]
    g_pack, offs = _pack_rows(g_vecs)
    w_pack, _ = _pack_rows([small_w[n].reshape(-1) for n in _SMALL] + [final_norm])
    m_pack, _ = _pack_rows([small_m[n].reshape(-1) for n in _SMALL] + [m_final_norm])
    v_pack, _ = _pack_rows([small_v[n].reshape(-1) for n in _SMALL] + [v_final_norm])
    small_slots, lb_slots = _exchange([g_pack, grads[1]["lb"]], scatter=False, name="gather_small_grads")
    packed = _adamw(w_pack, m_pack, v_pack, small_slots, name="adamw_small")
    results["lb_logits"] = _adamw_lb(lb_logits, m_lb_logits, v_lb_logits, lb_slots, name="adamw_lb")
    shapes = [small_w[n].shape for n in _SMALL] + [final_norm.shape]
    for n, off, shp in zip(_SMALL + ("final_norm",), offs, shapes):
        rows = 1
        for dim in shp:
            rows *= dim
        rows //= LANES
        results[n] = [o[off:off + rows].reshape(shp) for o in packed]

    order = ("norm_mix", "w_in", "a_out_norm", "b_out_norm", "w_out", "lb_logits", "ple_gate_norm", "w_ple_gate",
             "w_ple_proj", "ple_post_norm", "final_norm")
    out = [loss, grad_x]
    for k in range(4):
        out += [results[n][k] for n in order]
    return tuple(out)
```

```python
import jax
import jax.numpy as jnp
from jax import lax
from jax.experimental import pallas as pl
from jax.experimental.pallas import tpu as pltpu

F32 = jnp.float32
BF16 = jnp.bfloat16

N_DEV = 8
EPS = 1e-6
A_HEADS = 4
HEAD_A = 128
B_PAIRS = 4
HEAD_B = 64
CHUNK = 64
SUB = 16
LANES = 128
NEG_BIG = -1e30

ADAM_LR = 0.001
ADAM_B1 = 0.9
ADAM_B2 = 0.999
ADAM_EPS = 1e-08
ADAM_WD = 0.01
ADAM_STEP = 10

VMEM_LIMIT = 56 * 1024 * 1024

NN = ((1,), (0,))
NT = ((1,), (1,))
TN = ((0,), (0,))


def _dot(a, b, dims):
    return lax.dot_general(a.astype(BF16), b.astype(BF16), (dims, ((), ())), preferred_element_type=F32)


def _split_bf16(x, parts):
    out = []
    r = x
    for i in range(parts):
        p = r.astype(BF16)
        out.append(p)
        if i + 1 < parts:
            r = r - p.astype(F32)
    return out


def _dot_exact_lhs(t, x, parts):
    acc = None
    for p in _split_bf16(x, parts):
        d = lax.dot_general(t, p, (NN, ((), ())), preferred_element_type=F32)
        acc = d if acc is None else acc + d
    return acc


def _params(**kw):
    return pltpu.CompilerParams(vmem_limit_bytes=VMEM_LIMIT, **kw)


def _row_block(m, want):
    t = min(want, m)
    assert m % t == 0
    return t


def _mm_nn(a, w, add=None, *, name):
    m, k = a.shape
    nb, _, nc = w.shape
    tm = _row_block(m, 512)

    def body(*refs):
        if add is None:
            a_ref, w_ref, o_ref = refs
            o_ref[...] = _dot(a_ref[...], w_ref[...], NN)
        else:
            a_ref, w_ref, add_ref, o_ref = refs
            o_ref[...] = _dot(a_ref[...], w_ref[...], NN) + add_ref[...]

    in_specs = [pl.BlockSpec((tm, k), lambda i, j: (i, 0)), pl.BlockSpec((None, k, nc), lambda i, j: (j, 0, 0))]
    args = [a, w]
    if add is not None:
        in_specs.append(pl.BlockSpec((tm, nc), lambda i, j: (i, j)))
        args.append(add)
    return pl.pallas_call(
        body, name=name, grid=(m // tm, nb), in_specs=in_specs,
        out_specs=pl.BlockSpec((tm, nc), lambda i, j: (i, j)),
        out_shape=jax.ShapeDtypeStruct((m, nb * nc), F32), compiler_params=_params(),
    )(*args)


def _mm_nt(g, w, *, name):
    m = g.shape[0]
    nb, k, nc = w.shape
    tm = _row_block(m, 512)

    def body(g_ref, w_ref, o_ref):
        r = _dot(g_ref[...], w_ref[...], NT)

        @pl.when(pl.program_id(1) == 0)
        def _():
            o_ref[...] = r

        @pl.when(pl.program_id(1) != 0)
        def _():
            o_ref[...] += r

    return pl.pallas_call(
        body, name=name, grid=(m // tm, nb),
        in_specs=[pl.BlockSpec((tm, nc), lambda i, j: (i, j)), pl.BlockSpec((None, k, nc), lambda i, j: (j, 0, 0))],
        out_specs=pl.BlockSpec((tm, k), lambda i, j: (i, 0)),
        out_shape=jax.ShapeDtypeStruct((m, k), F32), compiler_params=_params(),
    )(g, w)


def _mm_tn(a, g, nb, *, name):
    m, k = a.shape
    nc = g.shape[1] // nb
    tm = _row_block(m, 512)
    steps = m // tm

    def body(a_ref, g_ref, o_ref, acc_ref):
        r = _dot(a_ref[...], g_ref[...], TN)

        @pl.when(pl.program_id(1) == 0)
        def _():
            acc_ref[...] = r

        @pl.when(pl.program_id(1) != 0)
        def _():
            acc_ref[...] += r

        @pl.when(pl.program_id(1) == steps - 1)
        def _():
            o_ref[...] = acc_ref[...].astype(BF16)

    return pl.pallas_call(
        body, name=name, grid=(nb, steps),
        in_specs=[pl.BlockSpec((tm, k), lambda j, i: (i, 0)), pl.BlockSpec((tm, nc), lambda j, i: (i, j))],
        out_specs=pl.BlockSpec((None, k, nc), lambda j, i: (j, 0, 0)),
        out_shape=jax.ShapeDtypeStruct((nb, k, nc), BF16), scratch_shapes=[pltpu.VMEM((k, nc), F32)],
        compiler_params=_params(),
    )(a, g)


def _rms(x, g):
    return x * lax.rsqrt(jnp.mean(x * x, axis=-1, keepdims=True) + EPS) * g


def _rms_bwd(x, g, dy):
    r = lax.rsqrt(jnp.mean(x * x, axis=-1, keepdims=True) + EPS)
    xh = x * r
    dxh = dy * g
    dx = r * (dxh - xh * jnp.mean(dxh * xh, axis=-1, keepdims=True))
    return dx, jnp.sum(dy * xh, axis=0, keepdims=True)


def _accumulate(ref, val):
    @pl.when(pl.program_id(0) == 0)
    def _():
        ref[...] = val

    @pl.when(pl.program_id(0) != 0)
    def _():
        ref[...] += val


def _norm_fwd(x, g, *, name):
    m, d = x.shape
    tm = _row_block(m, 512)

    def body(x_ref, g_ref, o_ref):
        o_ref[...] = _rms(x_ref[...], g_ref[...]).astype(BF16)

    return pl.pallas_call(
        body, name=name, grid=(m // tm,),
        in_specs=[pl.BlockSpec((tm, d), lambda i: (i, 0)), pl.BlockSpec((1, d), lambda i: (0, 0))],
        out_specs=pl.BlockSpec((tm, d), lambda i: (i, 0)),
        out_shape=jax.ShapeDtypeStruct((m, d), BF16), compiler_params=_params(),
    )(x, g)


def _norm_bwd(x, g, dy, add, *, name):
    m, d = x.shape
    tm = _row_block(m, 512)

    def body(x_ref, g_ref, dy_ref, add_ref, dx_ref, dg_ref):
        dx, dg = _rms_bwd(x_ref[...], g_ref[...], dy_ref[...])
        dx_ref[...] = dx + add_ref[...]
        _accumulate(dg_ref, dg)

    row = pl.BlockSpec((tm, d), lambda i: (i, 0))
    vec = pl.BlockSpec((1, d), lambda i: (0, 0))
    return pl.pallas_call(
        body, name=name, grid=(m // tm,), in_specs=[row, vec, row, row], out_specs=[row, vec],
        out_shape=[jax.ShapeDtypeStruct((m, d), F32), jax.ShapeDtypeStruct((1, d), F32)], compiler_params=_params(),
    )(x, g, dy, add)


def _silu(x):
    return x * jax.nn.sigmoid(x)


def _mix_block(oa, ob, ag, bg, ga, gb):
    parts = []
    for h in range(A_HEADS):
        o = oa[:, HEAD_A * h:HEAD_A * (h + 1)]
        parts.append(o * lax.rsqrt(jnp.mean(o * o, axis=-1, keepdims=True) + EPS))
    ya = jnp.concatenate(parts, axis=1) * ga * _silu(ag)
    low = lax.broadcasted_iota(jnp.int32, (1, LANES), 1) < HEAD_B
    parts = []
    for p in range(B_PAIRS):
        o = ob[:, LANES * p:LANES * (p + 1)]
        sq = o * o
        s_lo = jnp.sum(jnp.where(low, sq, 0.0), axis=-1, keepdims=True)
        s_hi = jnp.sum(jnp.where(low, 0.0, sq), axis=-1, keepdims=True)
        r = jnp.where(low, lax.rsqrt(s_lo * (1.0 / HEAD_B) + EPS), lax.rsqrt(s_hi * (1.0 / HEAD_B) + EPS))
        parts.append(o * r)
    yb = jnp.concatenate(parts, axis=1) * gb * _silu(bg)
    return jnp.concatenate([ya, yb], axis=1)


_GATE_A_BLOCK = 3
_GATE_B_BLOCK = 7


def _mix_fwd(oa, ob, proj, ga, gb, *, name):
    m, w = oa.shape
    tm = _row_block(m, 512)

    def body(oa_ref, ob_ref, ag_ref, bg_ref, ga_ref, gb_ref, y_ref):
        y_ref[...] = _mix_block(oa_ref[...], ob_ref[...], ag_ref[...], bg_ref[...], ga_ref[...], gb_ref[...]).astype(BF16)

    row = pl.BlockSpec((tm, w), lambda i: (i, 0))
    vec = pl.BlockSpec((1, w), lambda i: (0, 0))
    return pl.pallas_call(
        body, name=name, grid=(m // tm,),
        in_specs=[row, row, pl.BlockSpec((tm, w), lambda i: (i, _GATE_A_BLOCK)),
                  pl.BlockSpec((tm, w), lambda i: (i, _GATE_B_BLOCK)), vec, vec],
        out_specs=pl.BlockSpec((tm, 2 * w), lambda i: (i, 0)),
        out_shape=jax.ShapeDtypeStruct((m, 2 * w), BF16), compiler_params=_params(),
    )(oa, ob, proj, proj, ga, gb)


def _mix_bwd(dy, oa, ob, proj, ga, gb, *, name):
    m, w = oa.shape
    tm = _row_block(m, 256)

    def body(dy_ref, oa_ref, ob_ref, ag_ref, bg_ref, ga_ref, gb_ref, doa_ref, dob_ref, dag_ref, dbg_ref, dga_ref, dgb_ref):
        _, vjp = jax.vjp(_mix_block, oa_ref[...], ob_ref[...], ag_ref[...], bg_ref[...], ga_ref[...], gb_ref[...])
        doa, dob, dag, dbg, dga, dgb = vjp(dy_ref[...])
        doa_ref[...] = doa
        dob_ref[...] = dob
        dag_ref[...] = dag
        dbg_ref[...] = dbg
        _accumulate(dga_ref, dga)
        _accumulate(dgb_ref, dgb)

    row = pl.BlockSpec((tm, w), lambda i: (i, 0))
    vec = pl.BlockSpec((1, w), lambda i: (0, 0))
    big = jax.ShapeDtypeStruct((m, w), F32)
    small = jax.ShapeDtypeStruct((1, w), F32)
    return pl.pallas_call(
        body, name=name, grid=(m // tm,),
        in_specs=[pl.BlockSpec((tm, 2 * w), lambda i: (i, 0)), row, row,
                  pl.BlockSpec((tm, w), lambda i: (i, _GATE_A_BLOCK)), pl.BlockSpec((tm, w), lambda i: (i, _GATE_B_BLOCK)), vec, vec],
        out_specs=[row, row, row, row, vec, vec], out_shape=[big, big, big, big, small, small], compiler_params=_params(),
    )(dy, oa, ob, proj, proj, ga, gb)


def _ple_block(h1, gp, pp, gpost):
    return h1 + jax.nn.sigmoid(gp) * _rms(pp, gpost)


def _ple_fwd(h1, gp, pp, gpost, *, name):
    m, d = h1.shape
    tm = _row_block(m, 512)

    def body(h_ref, gp_ref, pp_ref, g_ref, o_ref):
        o_ref[...] = _ple_block(h_ref[...], gp_ref[...], pp_ref[...], g_ref[...])

    row = pl.BlockSpec((tm, d), lambda i: (i, 0))
    vec = pl.BlockSpec((1, d), lambda i: (0, 0))
    return pl.pallas_call(
        body, name=name, grid=(m // tm,), in_specs=[row, row, row, vec], out_specs=row,
        out_shape=jax.ShapeDtypeStruct((m, d), F32), compiler_params=_params(),
    )(h1, gp, pp, gpost)


def _ple_bwd(dh2, gp, pp, gpost, *, name):
    m, d = dh2.shape
    tm = _row_block(m, 512)

    def body(dh_ref, gp_ref, pp_ref, g_ref, dgp_ref, dpp_ref, dg_ref):
        dh = dh_ref[...]
        gate = jax.nn.sigmoid(gp_ref[...])
        pe = _rms(pp_ref[...], g_ref[...])
        dgp_ref[...] = dh * pe * (gate * (1.0 - gate))
        dpp, dg = _rms_bwd(pp_ref[...], g_ref[...], dh * gate)
        dpp_ref[...] = dpp
        _accumulate(dg_ref, dg)

    row = pl.BlockSpec((tm, d), lambda i: (i, 0))
    vec = pl.BlockSpec((1, d), lambda i: (0, 0))
    big = jax.ShapeDtypeStruct((m, d), F32)
    return pl.pallas_call(
        body, name=name, grid=(m // tm,), in_specs=[row, row, row, vec], out_specs=[row, row, vec],
        out_shape=[big, big, jax.ShapeDtypeStruct((1, d), F32)], compiler_params=_params(),
    )(dh2, gp, pp, gpost)


def _loss_head(h, gf, target, *, name):
    m, d = h.shape
    tm = _row_block(m, 512)

    def body(h_ref, g_ref, t_ref, loss_ref, dh_ref, dg_ref):
        x = h_ref[...]
        g = g_ref[...]
        err = _rms(x, g) - t_ref[...]
        part = 0.5 * jnp.sum(jnp.mean(err * err, axis=-1, keepdims=True), axis=0, keepdims=True)
        dx, dg = _rms_bwd(x, g, err * (1.0 / d))
        dh_ref[...] = dx
        _accumulate(dg_ref, dg)
        _accumulate(loss_ref, jnp.broadcast_to(part, (8, LANES)))

    row = pl.BlockSpec((tm, d), lambda i: (i, 0))
    vec = pl.BlockSpec((1, d), lambda i: (0, 0))
    return pl.pallas_call(
        body, name=name, grid=(m // tm,), in_specs=[row, vec, row],
        out_specs=[pl.BlockSpec((8, LANES), lambda i: (0, 0)), row, vec],
        out_shape=[jax.ShapeDtypeStruct((8, LANES), F32), jax.ShapeDtypeStruct((m, d), F32), jax.ShapeDtypeStruct((1, d), F32)],
        compiler_params=_params(),
    )(h, gf, target)


def _hgrn_pre(aq, af, lbv):
    sq = jax.nn.sigmoid(aq)
    sg = jax.nn.sigmoid(-af)
    k = (1.0 - lbv) * sg
    return aq * sq, sq, sg, k, jnp.log1p(-k)


def _tri(n, cmp):
    r = lax.broadcasted_iota(jnp.int32, (n, n), 0)
    c = lax.broadcasted_iota(jnp.int32, (n, n), 1)
    return cmp(r, c).astype(BF16)


def _hgrn_intra(q, k, b, b_scr):
    col = lax.broadcasted_iota(jnp.int32, (SUB, CHUNK), 1)
    rows = [jnp.zeros((SUB, CHUNK), F32)]
    facs = [None]
    for i in range(1, CHUNK // SUB):
        r = b_scr[SUB * i - 1:SUB * i, :]
        fq = jnp.exp(b[SUB * i:SUB * (i + 1)] - r)
        fk = jnp.exp(jnp.minimum(r - b, 0.0))
        qs = q[SUB * i:SUB * (i + 1)] * fq
        ks = k * fk
        rows.append(jnp.where(col < SUB * i, _dot(qs, ks, NT), 0.0))
        facs.append((fq, fk, qs, ks))
    return jnp.concatenate(rows, axis=0), facs


def _hgrn_diag_weights(i, k, b, q_scr, b_scr):
    sub = lax.broadcasted_iota(jnp.int32, (SUB, LANES), 0)
    ki = k[SUB * i:SUB * (i + 1)]
    bi = b[SUB * i:SUB * (i + 1)]
    es, ws = [], []
    for tau in range(SUB):
        t = SUB * i + tau
        e = jnp.exp(jnp.where(sub <= tau, b_scr[t:t + 1, :] - bi, NEG_BIG))
        es.append(e)
        ws.append((q_scr[t:t + 1, :] * ki) * e)
    return es, jnp.concatenate(ws, axis=0)


def _hgrn_fwd(proj, lb_row, *, name):
    s = proj.shape[0]
    tb = _row_block(s, 512)
    nch = tb // CHUNK

    def body(aq_ref, af_ref, ai_ref, lb_ref, o_ref, st_ref, s_scr, q_scr, b_scr, od_scr):
        @pl.when(pl.program_id(1) == 0)
        def _():
            s_scr[...] = jnp.zeros_like(s_scr)

        lbv = lb_ref[...]
        tril = _tri(CHUNK, lambda r, c: r >= c)
        ones = jnp.ones((LANES, LANES), BF16)

        def chunk(ci, carry):
            r0 = pl.multiple_of(ci * CHUNK, CHUNK)
            q, _, _, k, g = _hgrn_pre(aq_ref[pl.ds(r0, CHUNK), :], af_ref[pl.ds(r0, CHUNK), :], lbv)
            v = ai_ref[pl.ds(r0, CHUNK), :]
            st0 = s_scr[...]
            st_ref[ci] = st0
            b = _dot_exact_lhs(tril, g, 3)
            b_scr[...] = b
            q_scr[...] = q
            bl = b_scr[CHUNK - 1:CHUNK, :]
            o = _dot(q * jnp.exp(b), st0, NT)
            a, _ = _hgrn_intra(q, k, b, b_scr)
            o = o + _dot(a, v, NN)
            for i in range(CHUNK // SUB):
                _, wst = _hgrn_diag_weights(i, k, b, q_scr, b_scr)
                rep = _dot(wst, ones, NN)
                vi = v[SUB * i:SUB * (i + 1)]
                for tau in range(SUB):
                    t = SUB * i + tau
                    od_scr[t:t + 1, :] = jnp.sum(rep[SUB * tau:SUB * (tau + 1)] * vi, axis=0, keepdims=True)
            o_ref[pl.ds(r0, CHUNK), :] = o + od_scr[...]
            s_scr[...] = st0 * jnp.exp(bl) + _dot(v, k * jnp.exp(bl - b), TN)
            return carry

        lax.fori_loop(0, nch, chunk, 0)

    def col(block0):
        return pl.BlockSpec((tb, HEAD_A), lambda h, t: (t, block0 + h))

    return pl.pallas_call(
        body, name=name, grid=(A_HEADS, s // tb),
        in_specs=[col(0), col(A_HEADS), col(2 * A_HEADS), pl.BlockSpec((1, HEAD_A), lambda h, t: (0, h))],
        out_specs=[pl.BlockSpec((tb, HEAD_A), lambda h, t: (t, h)),
                   pl.BlockSpec((None, nch, HEAD_A, HEAD_A), lambda h, t: (h, t, 0, 0))],
        out_shape=[jax.ShapeDtypeStruct((s, A_HEADS * HEAD_A), F32),
                   jax.ShapeDtypeStruct((A_HEADS, s // CHUNK, HEAD_A, HEAD_A), F32)],
        scratch_shapes=[pltpu.VMEM((HEAD_A, HEAD_A), F32), pltpu.VMEM((CHUNK, HEAD_A), F32),
                        pltpu.VMEM((CHUNK, HEAD_A), F32), pltpu.VMEM((CHUNK, HEAD_A), F32)],
        compiler_params=_params(),
    )(proj, proj, proj, lb_row)


def _hgrn_bwd(proj, lb_row, states, do_a, *, name):
    s = proj.shape[0]
    tb = _row_block(s, 512)
    nch = tb // CHUNK
    nblk = s // tb

    def body(aq_ref, af_ref, ai_ref, lb_ref, st_ref, do_ref, daq_ref, daf_ref, dai_ref, dlb_ref,
             ds_scr, q_scr, b_scr, dq_scr, do_scr):
        @pl.when(pl.program_id(1) == 0)
        def _():
            ds_scr[...] = jnp.zeros_like(ds_scr)
            dlb_ref[...] = jnp.zeros_like(dlb_ref)

        lbv = lb_ref[...]
        tril = _tri(CHUNK, lambda r, c: r >= c)
        triu = _tri(CHUNK, lambda r, c: c >= r)
        ones = jnp.ones((LANES, LANES), BF16)
        col = lax.broadcasted_iota(jnp.int32, (SUB, CHUNK), 1)
        row = lax.broadcasted_iota(jnp.int32, (CHUNK, LANES), 0)
        nsub = CHUNK // SUB

        def chunk(cj, carry):
            ci = nch - 1 - cj
            r0 = pl.multiple_of(ci * CHUNK, CHUNK)
            aq = aq_ref[pl.ds(r0, CHUNK), :]
            q, sq, sg, k, g = _hgrn_pre(aq, af_ref[pl.ds(r0, CHUNK), :], lbv)
            v = ai_ref[pl.ds(r0, CHUNK), :]
            do = do_ref[pl.ds(r0, CHUNK), :]
            st0 = st_ref[ci]
            dst1 = ds_scr[...]
            b = _dot_exact_lhs(tril, g, 3)
            b_scr[...] = b
            q_scr[...] = q
            do_scr[...] = do
            bl = b_scr[CHUNK - 1:CHUNK, :]
            eb = jnp.exp(b)
            ebl = jnp.exp(bl)
            ekd = jnp.exp(bl - b)
            kd = k * ekd
            a, facs = _hgrn_intra(q, k, b, b_scr)

            da = _dot(do, v, NT)
            dv = _dot(a, do, TN) + _dot(kd, dst1, NT)
            dq = _dot(do, st0, NN) * eb
            dk_inter = _dot(v, dst1, NN) * ekd
            dk = dk_inter
            dq_rows = [jnp.zeros((SUB, LANES), F32)]
            for i in range(1, nsub):
                fq, fk, qs, ks = facs[i]
                da_i = jnp.where(col < SUB * i, da[SUB * i:SUB * (i + 1)], 0.0)
                dq_rows.append(_dot(da_i, ks, NN) * fq)
                dk = dk + _dot(da_i, qs, TN) * fk
            dq = dq + jnp.concatenate(dq_rows, axis=0)

            dv_rows, dk_rows = [], []
            for i in range(nsub):
                es, wst = _hgrn_diag_weights(i, k, b, q_scr, b_scr)
                ki = k[SUB * i:SUB * (i + 1)]
                vi = v[SUB * i:SUB * (i + 1)]
                rep = _dot(wst, ones, NN)
                mst = jnp.concatenate([do_scr[SUB * i + tau:SUB * i + tau + 1, :] * vi for tau in range(SUB)], axis=0)
                drep = _dot(mst, ones, NN)
                dvi = jnp.zeros((SUB, LANES), F32)
                dki = jnp.zeros((SUB, LANES), F32)
                for tau in range(SUB):
                    t = SUB * i + tau
                    gt = drep[SUB * tau:SUB * (tau + 1)] * es[tau]
                    dvi = dvi + rep[SUB * tau:SUB * (tau + 1)] * do_scr[t:t + 1, :]
                    dki = dki + gt * q_scr[t:t + 1, :]
                    dq_scr[t:t + 1, :] = jnp.sum(gt * ki, axis=0, keepdims=True)
                dv_rows.append(dvi)
                dk_rows.append(dki)
            dv = dv + jnp.concatenate(dv_rows, axis=0)
            dk = dk + jnp.concatenate(dk_rows, axis=0)
            dq = dq + dq_scr[...]

            ds_scr[...] = dst1 * ebl + _dot(do, q * eb, TN)
            last = jnp.sum(k * dk_inter, axis=0, keepdims=True) + ebl * jnp.sum(st0 * dst1, axis=0, keepdims=True)
            db = q * dq - k * dk + jnp.where(row == CHUNK - 1, last, 0.0)
            dg = _dot_exact_lhs(triu, db, 3)

            dkt = dk - dg / (1.0 - k)
            daq_ref[pl.ds(r0, CHUNK), :] = dq * (sq * (1.0 + aq * (1.0 - sq)))
            daf_ref[pl.ds(r0, CHUNK), :] = dkt * (1.0 - lbv) * (-(sg * (1.0 - sg)))
            dai_ref[pl.ds(r0, CHUNK), :] = dv
            dlb_ref[...] += jnp.sum(dkt * (-sg), axis=0, keepdims=True)
            return carry

        lax.fori_loop(0, nch, chunk, 0)

    def col_in(block0):
        return pl.BlockSpec((tb, HEAD_A), lambda h, t: (nblk - 1 - t, block0 + h))

    head_col = pl.BlockSpec((tb, HEAD_A), lambda h, t: (nblk - 1 - t, h))
    vec = pl.BlockSpec((1, HEAD_A), lambda h, t: (0, h))
    big = jax.ShapeDtypeStruct((s, A_HEADS * HEAD_A), F32)
    return pl.pallas_call(
        body, name=name, grid=(A_HEADS, nblk),
        in_specs=[col_in(0), col_in(A_HEADS), col_in(2 * A_HEADS), vec,
                  pl.BlockSpec((None, nch, HEAD_A, HEAD_A), lambda h, t: (h, nblk - 1 - t, 0, 0)), head_col],
        out_specs=[head_col, head_col, head_col, vec],
        out_shape=[big, big, big, jax.ShapeDtypeStruct((1, A_HEADS * HEAD_A), F32)],
        scratch_shapes=[pltpu.VMEM((HEAD_A, HEAD_A), F32), pltpu.VMEM((CHUNK, HEAD_A), F32),
                        pltpu.VMEM((CHUNK, HEAD_A), F32), pltpu.VMEM((CHUNK, HEAD_A), F32),
                        pltpu.VMEM((CHUNK, HEAD_A), F32)],
        compiler_params=_params(),
    )(proj, proj, proj, lb_row, states, do_a)


_BQ_BLOCK = 16
_BK_BLOCK = 20
_BV_BLOCK = 24
SB_SCALE = HEAD_B ** -0.5
SB_TQ = 512
SB_TK = 256


def _sb_blocks(s):
    tq = min(SB_TQ, s)
    tk = min(SB_TK, s)
    assert s % tq == 0 and tq % tk == 0 and s // tk <= LANES
    return tq, tk


def _sb2_weights(qh, kb, carry, incl, mask):
    z = _dot(qh, kb, NT)
    sp = jnp.maximum(z, 0.0) + jnp.log(1.0 + jnp.exp(-jnp.abs(z)))
    if mask is not None:
        sp = jnp.where(mask, sp, 0.0)
    suffix = _dot(sp, incl, NN)
    w = jnp.exp(z - suffix + carry)
    if mask is not None:
        w = jnp.where(mask, w, 0.0)
    return z, sp, w, suffix[:, 0:1]


def _sb_mask(qi, j, tq, tk):
    row = qi * tq + lax.broadcasted_iota(jnp.int32, (tq, tk), 0)
    col = j * tk + lax.broadcasted_iota(jnp.int32, (tq, tk), 1)
    return col < row


def _sb2_fwd(proj, *, name):
    s = proj.shape[0]
    tq, tk = _sb_blocks(s)
    nq, ratio = s // tq, tq // tk

    def body(q_ref, k_ref, v_ref, o_ref, c_ref):
        qi = pl.program_id(1)
        lane = lax.broadcasted_iota(jnp.int32, (1, LANES), 1)
        low = lane < HEAD_B

        def heads(x):
            return jnp.where(low, x, 0.0).astype(BF16), jnp.where(low, 0.0, x).astype(BF16)

        qh = heads(q_ref[...] * SB_SCALE)
        incl = _tri(tk, lambda r, c: r >= c)
        o_ref[...] = jnp.zeros_like(o_ref)
        c_ref[...] = jnp.zeros_like(c_ref)

        def tile(j, c, masked):
            s0 = pl.multiple_of(j * tk, tk)
            kb = k_ref[pl.ds(s0, tk), :].astype(BF16)
            vh = heads(v_ref[pl.ds(s0, tk), :])
            mask = _sb_mask(qi, j, tq, tk) if masked else None
            c_new = []
            acc = jnp.zeros((tq, LANES), F32)
            for h in range(2):
                _, _, w, rowsum = _sb2_weights(qh[h], kb, c[h], incl, mask)
                c_ref[h] = jnp.where(lane == j, c[h], c_ref[h])
                c_new.append(c[h] - rowsum)
                acc = acc + _dot(w, vh[h], NN)
            o_ref[...] += acc
            return tuple(c_new)

        zc = jnp.zeros((tq, 1), F32)
        c = (zc, zc)
        for r in reversed(range(ratio)):
            c = tile(qi * ratio + r, c, True)
        lax.fori_loop(0, qi * ratio, lambda jj, c: tile(qi * ratio - 1 - jj, c, False), c)

    def whole(block0):
        return pl.BlockSpec((s, LANES), lambda p, i: (0, block0 + p))

    return pl.pallas_call(
        body, name=name, grid=(B_PAIRS, nq),
        in_specs=[pl.BlockSpec((tq, LANES), lambda p, i: (i, _BQ_BLOCK + p)), whole(_BK_BLOCK), whole(_BV_BLOCK)],
        out_specs=[pl.BlockSpec((tq, LANES), lambda p, i: (i, p)),
                   pl.BlockSpec((None, 2, tq, LANES), lambda p, i: (p, 0, i, 0))],
        out_shape=[jax.ShapeDtypeStruct((s, B_PAIRS * LANES), F32), jax.ShapeDtypeStruct((B_PAIRS, 2, s, LANES), F32)],
        compiler_params=_params(),
    )(proj, proj, proj)


def _sb2_bwd(proj, carries, do_b, *, name):
    s = proj.shape[0]
    tq, tk = _sb_blocks(s)
    nq, nk, ratio = s // tq, s // tk, tq // tk

    def body(q_ref, k_ref, v_ref, c_ref, do_ref, dq_ref, dk_ref, dv_ref, dkt_scr, dvt_scr):
        qi = pl.program_id(1)

        @pl.when(qi == 0)
        def _():
            dkt_scr[...] = jnp.zeros_like(dkt_scr)
            dvt_scr[...] = jnp.zeros_like(dvt_scr)

        lane = lax.broadcasted_iota(jnp.int32, (1, LANES), 1)
        low = lane < HEAD_B
        low_row = lax.broadcasted_iota(jnp.int32, (LANES, 1), 0) < HEAD_B

        def heads(x):
            return jnp.where(low, x, 0.0).astype(BF16), jnp.where(low, 0.0, x).astype(BF16)

        def heads_t(x):
            xt = x.T
            return jnp.where(low_row, xt, 0.0).astype(BF16), jnp.where(low_row, 0.0, xt).astype(BF16)

        q = q_ref[...] * SB_SCALE
        do = do_ref[...]
        qh, qth = heads(q), heads_t(q)
        doh, doth = heads(do), heads_t(do)
        cst = (c_ref[0], c_ref[1])
        incl = _tri(tk, lambda r, c: r >= c)
        pre = _tri(tk, lambda r, c: r <= c)

        def tile(j, carry, masked):
            ec, dq = carry
            s0 = pl.multiple_of(j * tk, tk)
            kf = k_ref[pl.ds(s0, tk), :]
            kb = kf.astype(BF16)
            kh = heads(kf)
            vb = v_ref[pl.ds(s0, tk), :].astype(BF16)
            mask = _sb_mask(qi, j, tq, tk) if masked else None
            ec_new = []
            dkt = jnp.zeros((LANES, tk), F32)
            dvt = jnp.zeros((LANES, tk), F32)
            for h in range(2):
                cin = jnp.sum(jnp.where(lane == j, cst[h], 0.0), axis=1, keepdims=True)
                z, sp, w, _ = _sb2_weights(qh[h], kb, cin, incl, mask)
                e = w * _dot(doh[h], vb, NT)
                prefix = _dot(e, pre, NN)
                dz = e - jnp.exp(z - sp) * (ec[h] + prefix)
                if mask is not None:
                    dz = jnp.where(mask, dz, 0.0)
                ec_new.append(ec[h] + prefix[:, tk - 1:tk])
                dzb = dz.astype(BF16)
                wb = w.astype(BF16)
                dq = dq + _dot(dzb, kh[h], NN)
                dkt = dkt + _dot(qth[h], dzb, NN)
                dvt = dvt + _dot(doth[h], wb, NN)
            dkt_scr[j] += dkt
            dvt_scr[j] += dvt
            return tuple(ec_new), dq

        zc = jnp.zeros((tq, 1), F32)
        carry = lax.fori_loop(0, qi * ratio, lambda j, carry: tile(j, carry, False), ((zc, zc), jnp.zeros((tq, LANES), F32)))
        for r in range(ratio):
            carry = tile(qi * ratio + r, carry, True)
        dq_ref[...] = carry[1] * SB_SCALE

        @pl.when(qi == nq - 1)
        def _():
            for j in range(nk):
                dk_ref[tk * j:tk * (j + 1), :] = dkt_scr[j].T
                dv_ref[tk * j:tk * (j + 1), :] = dvt_scr[j].T

    def whole_in(block0):
        return pl.BlockSpec((s, LANES), lambda p, i: (0, block0 + p))

    blk = pl.BlockSpec((tq, LANES), lambda p, i: (i, p))
    whole_out = pl.BlockSpec((s, LANES), lambda p, i: (0, p))
    big = jax.ShapeDtypeStruct((s, B_PAIRS * LANES), F32)
    return pl.pallas_call(
        body, name=name, grid=(B_PAIRS, nq),
        in_specs=[pl.BlockSpec((tq, LANES), lambda p, i: (i, _BQ_BLOCK + p)), whole_in(_BK_BLOCK), whole_in(_BV_BLOCK),
                  pl.BlockSpec((None, 2, tq, LANES), lambda p, i: (p, 0, i, 0)), blk],
        out_specs=[blk, whole_out, whole_out], out_shape=[big, big, big],
        scratch_shapes=[pltpu.VMEM((nk, LANES, tk), F32), pltpu.VMEM((nk, LANES, tk), F32)],
        compiler_params=_params(),
    )(proj, proj, proj, carries, do_b)


def _my_id():
    return 4 * lax.axis_index("x") + 2 * lax.axis_index("y") + lax.axis_index("c")


def _mesh_pos(d):
    return (d // 4, (d // 2) % 2, d % 2)


def _exchange(arrays, scatter, *, name):
    n = len(arrays)
    any_spec = pl.BlockSpec(memory_space=pl.ANY)

    def body(*refs):
        ins, outs = refs[:n], refs[n:2 * n]
        send_sems, recv_sems, local_sems = refs[2 * n:]
        me = _my_id()

        def src(i, to):
            return ins[i].at[to] if scatter else ins[i]

        local = [pltpu.make_async_copy(src(i, me), outs[i].at[me], local_sems.at[i]) for i in range(n)]
        for cp in local:
            cp.start()
        sent = []
        for step in range(1, N_DEV):
            to = (me + step) % N_DEV
            for i in range(n):
                cp = pltpu.make_async_remote_copy(
                    src_ref=src(i, to), dst_ref=outs[i].at[me],
                    send_sem=send_sems.at[step - 1, i], recv_sem=recv_sems.at[step - 1, i],
                    device_id=_mesh_pos(to), device_id_type=pl.DeviceIdType.MESH)
                cp.start()
                sent.append(cp)
        for step in range(1, N_DEV):
            frm = (me + N_DEV - step) % N_DEV
            for i in range(n):
                pltpu.make_async_remote_copy(
                    src_ref=src(i, frm), dst_ref=outs[i].at[frm],
                    send_sem=send_sems.at[step - 1, i], recv_sem=recv_sems.at[step - 1, i],
                    device_id=_mesh_pos(frm), device_id_type=pl.DeviceIdType.MESH).wait_recv()
        for cp in sent:
            cp.wait_send()
        for cp in local:
            cp.wait()

    def out_shape(a):
        return jax.ShapeDtypeStruct(((N_DEV,) + a.shape[1:]) if scatter else ((N_DEV,) + a.shape), a.dtype)

    return pl.pallas_call(
        body, name=name, in_specs=[any_spec] * n, out_specs=[any_spec] * n,
        out_shape=[out_shape(a) for a in arrays],
        scratch_shapes=[pltpu.SemaphoreType.DMA((N_DEV - 1, n)), pltpu.SemaphoreType.DMA((N_DEV - 1, n)),
                        pltpu.SemaphoreType.DMA((n,))],
    )(*arrays)


def _adamw_math(w, m, v, g):
    m2 = ADAM_B1 * m + (1.0 - ADAM_B1) * g
    v2 = ADAM_B2 * v + (1.0 - ADAM_B2) * (g * g)
    m_hat = m2 / (1.0 - ADAM_B1 ** ADAM_STEP)
    v_hat = v2 / (1.0 - ADAM_B2 ** ADAM_STEP)
    delta = -ADAM_LR * (m_hat / (jnp.sqrt(v_hat) + ADAM_EPS) + ADAM_WD * w)
    return delta, m2, v2


def _slot_sum(ref):
    g = ref[0].astype(F32)
    for d in range(1, N_DEV):
        g = g + ref[d].astype(F32)
    return g


def _adamw(w, m, v, slots, *, name):
    r, c = w.shape
    tr = _row_block(r, 256)

    def body(w_ref, m_ref, v_ref, s_ref, g_ref, d_ref, m2_ref, v2_ref):
        g = _slot_sum(s_ref)
        delta, m2, v2 = _adamw_math(w_ref[...], m_ref[...], v_ref[...], g)
        g_ref[...] = g
        d_ref[...] = delta
        m2_ref[...] = m2
        v2_ref[...] = v2

    row = pl.BlockSpec((tr, c), lambda i: (i, 0))
    shp = jax.ShapeDtypeStruct((r, c), F32)
    return pl.pallas_call(
        body, name=name, grid=(r // tr,),
        in_specs=[row, row, row, pl.BlockSpec((N_DEV, tr, c), lambda i: (0, i, 0))],
        out_specs=[row, row, row, row], out_shape=[shp, shp, shp, shp], compiler_params=_params(),
    )(w, m, v, slots)


def _layer_softmax(l_ref):
    l0 = l_ref[0:1, :]
    l1 = l_ref[1:2, :]
    mx = jnp.maximum(l0, l1)
    e0 = jnp.exp(l0 - mx)
    e1 = jnp.exp(l1 - mx)
    return e0 / (e0 + e1), e1 / (e0 + e1)


def _adamw_lb(w, m, v, slots, *, name):
    def body(w_ref, m_ref, v_ref, s_ref, g_ref, d_ref, m2_ref, v2_ref):
        sm0, sm1 = _layer_softmax(w_ref)
        dl1 = _slot_sum(s_ref) * (sm0 * sm1)
        for row, g in ((0, -dl1), (1, dl1)):
            sl = slice(row, row + 1)
            delta, m2, v2 = _adamw_math(w_ref[sl, :], m_ref[sl, :], v_ref[sl, :], g)
            g_ref[sl, :] = g
            d_ref[sl, :] = delta
            m2_ref[sl, :] = m2
            v2_ref[sl, :] = v2

    shp = jax.ShapeDtypeStruct(w.shape, F32)
    return pl.pallas_call(body, name=name, out_shape=[shp, shp, shp, shp])(w, m, v, slots)


def _lower_bound_1(lb_logits, *, name):
    def body(l_ref, o_ref):
        sm0, sm1 = _layer_softmax(l_ref)
        o_ref[...] = (sm0 + sm1) - sm0

    return pl.pallas_call(body, name=name, out_shape=jax.ShapeDtypeStruct((1, lb_logits.shape[1]), F32))(lb_logits)


_SMALL = ("norm_mix", "a_out_norm", "b_out_norm", "ple_gate_norm", "ple_post_norm")


def _pack_rows(vectors):
    offs, rows = [], 0
    for vec in vectors:
        offs.append(rows)
        rows += vec.shape[0] // LANES
    flat = jnp.concatenate([vec.reshape(-1, LANES) for vec in vectors], axis=0)
    pad = (-rows) % 8
    if pad:
        flat = jnp.concatenate([flat, jnp.zeros((pad, LANES), F32)], axis=0)
    return flat, offs


def kernel(x, p, norm_mix, w_in, a_out_norm, b_out_norm, w_out, lb_logits, ple_gate_norm, w_ple_gate, w_ple_proj, ple_post_norm, final_norm, loss_target, m_norm_mix, m_w_in, m_a_out_norm, m_b_out_norm, m_w_out, m_lb_logits, m_ple_gate_norm, m_w_ple_gate, m_w_ple_proj, m_ple_post_norm, m_final_norm, v_norm_mix, v_w_in, v_a_out_norm, v_b_out_norm, v_w_out, v_lb_logits, v_ple_gate_norm, v_w_ple_gate, v_w_ple_proj, v_ple_post_norm, v_final_norm):
    depth = w_in.shape[0]
    assert depth == 2
    s, d = x.shape[1], x.shape[2]
    h = x.reshape(s, d)
    target = loss_target.reshape(s, d)

    gathered = _exchange([w_in.astype(BF16), w_out.astype(BF16), w_ple_gate.astype(BF16), w_ple_proj.astype(BF16)],
                         scatter=False, name="gather_weights")
    win_g, wout_g, wpg_g, wpp_g = gathered
    lb1 = _lower_bound_1(lb_logits, name="lower_bound")
    lbs = [jnp.zeros_like(lb1), lb1]

    saved = []
    for l in range(depth):
        win = win_g[:, l]
        wout = wout_g[:, l].reshape(1, d, d)
        wpg = wpg_g[:, l].reshape(1, d, d)
        wpp = wpp_g[:, l]
        u = _norm_fwd(h, norm_mix[l:l + 1], name="norm_mix")
        proj = _mm_nn(u, win, name="mm_in")
        o_a, states = _hgrn_fwd(proj, lbs[l], name="hgrn_fwd")
        o_b, carries = _sb2_fwd(proj, name="sb_fwd")
        y = _mix_fwd(o_a, o_b, proj, a_out_norm[l:l + 1], b_out_norm[l:l + 1], name="mix_fwd")
        h1 = _mm_nn(y, wout, add=h, name="mm_out")
        r1 = _norm_fwd(h1, ple_gate_norm[l:l + 1], name="norm_gate")
        gp = _mm_nn(r1, wpg, name="mm_gate")
        pl_in = p[l].reshape(s, p.shape[-1])
        pp = _mm_nn(pl_in, wpp, name="mm_ple")
        h2 = _ple_fwd(h1, gp, pp, ple_post_norm[l:l + 1], name="ple_fwd")
        saved.append((h, u, proj, o_a, states, o_b, carries, y, h1, r1, gp, pp, pl_in, win, wout, wpg, wpp))
        h = h2

    loss_part, dh, dg_final = _loss_head(h, final_norm.reshape(1, d), target, name="loss_head")

    grads = [None] * depth
    for l in reversed(range(depth)):
        h0, u, proj, o_a, states, o_b, carries, y, h1, r1, gp, pp, pl_in, win, wout, wpg, wpp = saved[l]
        dgp, dpp, dg_post = _ple_bwd(dh, gp, pp, ple_post_norm[l:l + 1], name="ple_bwd")
        dwpp = _mm_tn(pl_in, dpp, N_DEV, name="mm_ple_dw")
        dwpg = _mm_tn(r1, dgp, 1, name="mm_gate_dw")
        dr1 = _mm_nt(dgp, wpg, name="mm_gate_dx")
        dh1, dg_pg = _norm_bwd(h1, ple_gate_norm[l:l + 1], dr1, dh, name="norm_gate_bwd")
        dwout = _mm_tn(y, dh1, 1, name="mm_out_dw")
        dy = _mm_nt(dh1, wout, name="mm_out_dx")
        do_a, do_b, dag, dbg, dg_a, dg_b = _mix_bwd(dy, o_a, o_b, proj, a_out_norm[l:l + 1], b_out_norm[l:l + 1], name="mix_bwd")
        dbq, dbk, dbv = _sb2_bwd(proj, carries, do_b, name="sb_bwd")
        daq, daf, dai, dlb = _hgrn_bwd(proj, lbs[l], states, do_a, name="hgrn_bwd")
        dproj = jnp.concatenate([daq, daf, dai, dag, dbq, dbk, dbv, dbg], axis=1)
        dwin = _mm_tn(u, dproj, N_DEV, name="mm_in_dw")
        du = _mm_nt(dproj, win, name="mm_in_dx")
        dh, dg_mix = _norm_bwd(h0, norm_mix[l:l + 1], du, dh1, name="norm_mix_bwd")
        grads[l] = dict(w_in=dwin, w_out=dwout.reshape(N_DEV, d // N_DEV, d), w_ple_gate=dwpg.reshape(N_DEV, d // N_DEV, d),
                        w_ple_proj=dwpp, norm_mix=dg_mix, a_out_norm=dg_a, b_out_norm=dg_b, ple_gate_norm=dg_pg,
                        ple_post_norm=dg_post, lb=dlb)

    grad_x = dh.reshape(x.shape)
    loss = lax.psum(loss_part[0, 0], ("x", "y", "c"))

    big = ("w_in", "w_out", "w_ple_gate", "w_ple_proj")
    stacked = [jnp.stack([grads[l][n] for l in range(depth)], axis=1) for n in big]
    slots = _exchange(stacked, scatter=True, name="scatter_grads")
    results = {}
    for n, sl, w, m, v in zip(big, slots, (w_in, w_out, w_ple_gate, w_ple_proj), (m_w_in, m_w_out, m_w_ple_gate, m_w_ple_proj),
                              (v_w_in, v_w_out, v_w_ple_gate, v_w_ple_proj)):
        c = w.shape[-1]
        outs = _adamw(w.reshape(-1, c), m.reshape(-1, c), v.reshape(-1, c), sl.reshape(N_DEV, -1, c), name="adamw_" + n)
        results[n] = [o.reshape(w.shape) for o in outs]

    small_w = dict(norm_mix=norm_mix, a_out_norm=a_out_norm, b_out_norm=b_out_norm, ple_gate_norm=ple_gate_norm,
                   ple_post_norm=ple_post_norm)
    small_m = dict(norm_mix=m_norm_mix, a_out_norm=m_a_out_norm, b_out_norm=m_b_out_norm, ple_gate_norm=m_ple_gate_norm,
                   ple_post_norm=m_ple_post_norm)
    small_v = dict(norm_mix=v_norm_mix, a_out_norm=v_a_out_norm, b_out_norm=v_b_out_norm, ple_gate_norm=v_ple_gate_norm,
                   ple_post_norm=v_ple_post_norm)
    g_vecs = [jnp.concatenate([grads[l][n].reshape(-1) for l in range(depth)]) for n in _SMALL] + [dg_final.reshape(-1)]
    g_pack, offs = _pack_rows(g_vecs)
    w_pack, _ = _pack_rows([small_w[n].reshape(-1) for n in _SMALL] + [final_norm])
    m_pack, _ = _pack_rows([small_m[n].reshape(-1) for n in _SMALL] + [m_final_norm])
    v_pack, _ = _pack_rows([small_v[n].reshape(-1) for n in _SMALL] + [v_final_norm])
    small_slots, lb_slots = _exchange([g_pack, grads[1]["lb"]], scatter=False, name="gather_small_grads")
    packed = _adamw(w_pack, m_pack, v_pack, small_slots, name="adamw_small")
    results["lb_logits"] = _adamw_lb(lb_logits, m_lb_logits, v_lb_logits, lb_slots, name="adamw_lb")
    shapes = [small_w[n].shape for n in _SMALL] + [final_norm.shape]
    for n, off, shp in zip(_SMALL + ("final_norm",), offs, shapes):
        rows = 1
        for dim in shp:
            rows *= dim
        rows //= LANES
        results[n] = [o[off:off + rows].reshape(shp) for o in packed]

    order = ("norm_mix", "w_in", "a_out_norm", "b_out_norm", "w_out", "lb_logits", "ple_gate_norm", "w_ple_gate",
             "w_ple_proj", "ple_post_norm", "final_norm")
    out = [loss, grad_x]
    for k in range(4):
        out += [results[n][k] for n in order]
    return tuple(out)
```

```python
import jax
import jax.numpy as jnp
from jax import lax
from jax.experimental import pallas as pl
from jax.experimental.pallas import tpu as pltpu

F32 = jnp.float32
BF16 = jnp.bfloat16

N_DEV = 8
EPS = 1e-6
A_HEADS = 4
HEAD_A = 128
B_PAIRS = 4
HEAD_B = 64
CHUNK = 64
SUB = 16
LANES = 128
NEG_BIG = -1e30

ADAM_LR = 0.001
ADAM_B1 = 0.9
ADAM_B2 = 0.999
ADAM_EPS = 1e-08
ADAM_WD = 0.01
ADAM_STEP = 10

VMEM_LIMIT = 56 * 1024 * 1024

NN = ((1,), (0,))
NT = ((1,), (1,))
TN = ((0,), (0,))


def _dot(a, b, dims):
    return lax.dot_general(a.astype(BF16), b.astype(BF16), (dims, ((), ())), preferred_element_type=F32)


def _split_bf16(x, parts):
    out = []
    r = x
    for i in range(parts):
        p = r.astype(BF16)
        out.append(p)
        if i + 1 < parts:
            r = r - p.astype(F32)
    return out


def _dot_exact_lhs(t, x, parts):
    acc = None
    for p in _split_bf16(x, parts):
        d = lax.dot_general(t, p, (NN, ((), ())), preferred_element_type=F32)
        acc = d if acc is None else acc + d
    return acc


def _params(**kw):
    return pltpu.CompilerParams(vmem_limit_bytes=VMEM_LIMIT, **kw)


def _row_block(m, want):
    t = min(want, m)
    assert m % t == 0
    return t


def _mm_nn(a, w, add=None, *, name):
    m, k = a.shape
    nb, _, nc = w.shape
    tm = _row_block(m, 512)

    def body(*refs):
        if add is None:
            a_ref, w_ref, o_ref = refs
            o_ref[...] = _dot(a_ref[...], w_ref[...], NN)
        else:
            a_ref, w_ref, add_ref, o_ref = refs
            o_ref[...] = _dot(a_ref[...], w_ref[...], NN) + add_ref[...]

    in_specs = [pl.BlockSpec((tm, k), lambda i, j: (i, 0)), pl.BlockSpec((None, k, nc), lambda i, j: (j, 0, 0))]
    args = [a, w]
    if add is not None:
        in_specs.append(pl.BlockSpec((tm, nc), lambda i, j: (i, j)))
        args.append(add)
    return pl.pallas_call(
        body, name=name, grid=(m // tm, nb), in_specs=in_specs,
        out_specs=pl.BlockSpec((tm, nc), lambda i, j: (i, j)),
        out_shape=jax.ShapeDtypeStruct((m, nb * nc), F32), compiler_params=_params(),
    )(*args)


def _mm_nt(g, w, *, name):
    m = g.shape[0]
    nb, k, nc = w.shape
    tm = _row_block(m, 512)

    def body(g_ref, w_ref, o_ref):
        r = _dot(g_ref[...], w_ref[...], NT)

        @pl.when(pl.program_id(1) == 0)
        def _():
            o_ref[...] = r

        @pl.when(pl.program_id(1) != 0)
        def _():
            o_ref[...] += r

    return pl.pallas_call(
        body, name=name, grid=(m // tm, nb),
        in_specs=[pl.BlockSpec((tm, nc), lambda i, j: (i, j)), pl.BlockSpec((None, k, nc), lambda i, j: (j, 0, 0))],
        out_specs=pl.BlockSpec((tm, k), lambda i, j: (i, 0)),
        out_shape=jax.ShapeDtypeStruct((m, k), F32), compiler_params=_params(),
    )(g, w)


def _mm_tn(a, g, nb, *, name):
    m, k = a.shape
    nc = g.shape[1] // nb
    tm = _row_block(m, 512)
    steps = m // tm

    def body(a_ref, g_ref, o_ref, acc_ref):
        r = _dot(a_ref[...], g_ref[...], TN)

        @pl.when(pl.program_id(1) == 0)
        def _():
            acc_ref[...] = r

        @pl.when(pl.program_id(1) != 0)
        def _():
            acc_ref[...] += r

        @pl.when(pl.program_id(1) == steps - 1)
        def _():
            o_ref[...] = acc_ref[...].astype(BF16)

    return pl.pallas_call(
        body, name=name, grid=(nb, steps),
        in_specs=[pl.BlockSpec((tm, k), lambda j, i: (i, 0)), pl.BlockSpec((tm, nc), lambda j, i: (i, j))],
        out_specs=pl.BlockSpec((None, k, nc), lambda j, i: (j, 0, 0)),
        out_shape=jax.ShapeDtypeStruct((nb, k, nc), BF16), scratch_shapes=[pltpu.VMEM((k, nc), F32)],
        compiler_params=_params(),
    )(a, g)


def _rms(x, g):
    return x * lax.rsqrt(jnp.mean(x * x, axis=-1, keepdims=True) + EPS) * g


def _rms_bwd(x, g, dy):
    r = lax.rsqrt(jnp.mean(x * x, axis=-1, keepdims=True) + EPS)
    xh = x * r
    dxh = dy * g
    dx = r * (dxh - xh * jnp.mean(dxh * xh, axis=-1, keepdims=True))
    return dx, jnp.sum(dy * xh, axis=0, keepdims=True)


def _accumulate(ref, val):
    @pl.when(pl.program_id(0) == 0)
    def _():
        ref[...] = val

    @pl.when(pl.program_id(0) != 0)
    def _():
        ref[...] += val


def _norm_fwd(x, g, *, name):
    m, d = x.shape
    tm = _row_block(m, 512)

    def body(x_ref, g_ref, o_ref):
        o_ref[...] = _rms(x_ref[...], g_ref[...]).astype(BF16)

    return pl.pallas_call(
        body, name=name, grid=(m // tm,),
        in_specs=[pl.BlockSpec((tm, d), lambda i: (i, 0)), pl.BlockSpec((1, d), lambda i: (0, 0))],
        out_specs=pl.BlockSpec((tm, d), lambda i: (i, 0)),
        out_shape=jax.ShapeDtypeStruct((m, d), BF16), compiler_params=_params(),
    )(x, g)


def _norm_bwd(x, g, dy, add, *, name):
    m, d = x.shape
    tm = _row_block(m, 512)

    def body(x_ref, g_ref, dy_ref, add_ref, dx_ref, dg_ref):
        dx, dg = _rms_bwd(x_ref[...], g_ref[...], dy_ref[...])
        dx_ref[...] = dx + add_ref[...]
        _accumulate(dg_ref, dg)

    row = pl.BlockSpec((tm, d), lambda i: (i, 0))
    vec = pl.BlockSpec((1, d), lambda i: (0, 0))
    return pl.pallas_call(
        body, name=name, grid=(m // tm,), in_specs=[row, vec, row, row], out_specs=[row, vec],
        out_shape=[jax.ShapeDtypeStruct((m, d), F32), jax.ShapeDtypeStruct((1, d), F32)], compiler_params=_params(),
    )(x, g, dy, add)


def _silu(x):
    return x * jax.nn.sigmoid(x)


def _mix_block(oa, ob, ag, bg, ga, gb):
    parts = []
    for h in range(A_HEADS):
        o = oa[:, HEAD_A * h:HEAD_A * (h + 1)]
        parts.append(o * lax.rsqrt(jnp.mean(o * o, axis=-1, keepdims=True) + EPS))
    ya = jnp.concatenate(parts, axis=1) * ga * _silu(ag)
    low = lax.broadcasted_iota(jnp.int32, (1, LANES), 1) < HEAD_B
    parts = []
    for p in range(B_PAIRS):
        o = ob[:, LANES * p:LANES * (p + 1)]
        sq = o * o
        s_lo = jnp.sum(jnp.where(low, sq, 0.0), axis=-1, keepdims=True)
        s_hi = jnp.sum(jnp.where(low, 0.0, sq), axis=-1, keepdims=True)
        r = jnp.where(low, lax.rsqrt(s_lo * (1.0 / HEAD_B) + EPS), lax.rsqrt(s_hi * (1.0 / HEAD_B) + EPS))
        parts.append(o * r)
    yb = jnp.concatenate(parts, axis=1) * gb * _silu(bg)
    return jnp.concatenate([ya, yb], axis=1)


_GATE_A_BLOCK = 3
_GATE_B_BLOCK = 7


def _mix_fwd(oa, ob, proj, ga, gb, *, name):
    m, w = oa.shape
    tm = _row_block(m, 512)

    def body(oa_ref, ob_ref, ag_ref, bg_ref, ga_ref, gb_ref, y_ref):
        y_ref[...] = _mix_block(oa_ref[...], ob_ref[...], ag_ref[...], bg_ref[...], ga_ref[...], gb_ref[...]).astype(BF16)

    row = pl.BlockSpec((tm, w), lambda i: (i, 0))
    vec = pl.BlockSpec((1, w), lambda i: (0, 0))
    return pl.pallas_call(
        body, name=name, grid=(m // tm,),
        in_specs=[row, row, pl.BlockSpec((tm, w), lambda i: (i, _GATE_A_BLOCK)),
                  pl.BlockSpec((tm, w), lambda i: (i, _GATE_B_BLOCK)), vec, vec],
        out_specs=pl.BlockSpec((tm, 2 * w), lambda i: (i, 0)),
        out_shape=jax.ShapeDtypeStruct((m, 2 * w), BF16), compiler_params=_params(),
    )(oa, ob, proj, proj, ga, gb)


def _mix_bwd(dy, oa, ob, proj, ga, gb, *, name):
    m, w = oa.shape
    tm = _row_block(m, 256)

    def body(dy_ref, oa_ref, ob_ref, ag_ref, bg_ref, ga_ref, gb_ref, doa_ref, dob_ref, dag_ref, dbg_ref, dga_ref, dgb_ref):
        _, vjp = jax.vjp(_mix_block, oa_ref[...], ob_ref[...], ag_ref[...], bg_ref[...], ga_ref[...], gb_ref[...])
        doa, dob, dag, dbg, dga, dgb = vjp(dy_ref[...])
        doa_ref[...] = doa
        dob_ref[...] = dob
        dag_ref[...] = dag
        dbg_ref[...] = dbg
        _accumulate(dga_ref, dga)
        _accumulate(dgb_ref, dgb)

    row = pl.BlockSpec((tm, w), lambda i: (i, 0))
    vec = pl.BlockSpec((1, w), lambda i: (0, 0))
    big = jax.ShapeDtypeStruct((m, w), F32)
    small = jax.ShapeDtypeStruct((1, w), F32)
    return pl.pallas_call(
        body, name=name, grid=(m // tm,),
        in_specs=[pl.BlockSpec((tm, 2 * w), lambda i: (i, 0)), row, row,
                  pl.BlockSpec((tm, w), lambda i: (i, _GATE_A_BLOCK)), pl.BlockSpec((tm, w), lambda i: (i, _GATE_B_BLOCK)), vec, vec],
        out_specs=[row, row, row, row, vec, vec], out_shape=[big, big, big, big, small, small], compiler_params=_params(),
    )(dy, oa, ob, proj, proj, ga, gb)


def _ple_block(h1, gp, pp, gpost):
    return h1 + jax.nn.sigmoid(gp) * _rms(pp, gpost)


def _ple_fwd(h1, gp, pp, gpost, *, name):
    m, d = h1.shape
    tm = _row_block(m, 512)

    def body(h_ref, gp_ref, pp_ref, g_ref, o_ref):
        o_ref[...] = _ple_block(h_ref[...], gp_ref[...], pp_ref[...], g_ref[...])

    row = pl.BlockSpec((tm, d), lambda i: (i, 0))
    vec = pl.BlockSpec((1, d), lambda i: (0, 0))
    return pl.pallas_call(
        body, name=name, grid=(m // tm,), in_specs=[row, row, row, vec], out_specs=row,
        out_shape=jax.ShapeDtypeStruct((m, d), F32), compiler_params=_params(),
    )(h1, gp, pp, gpost)


def _ple_bwd(dh2, gp, pp, gpost, *, name):
    m, d = dh2.shape
    tm = _row_block(m, 512)

    def body(dh_ref, gp_ref, pp_ref, g_ref, dgp_ref, dpp_ref, dg_ref):
        dh = dh_ref[...]
        gate = jax.nn.sigmoid(gp_ref[...])
        pe = _rms(pp_ref[...], g_ref[...])
        dgp_ref[...] = dh * pe * (gate * (1.0 - gate))
        dpp, dg = _rms_bwd(pp_ref[...], g_ref[...], dh * gate)
        dpp_ref[...] = dpp
        _accumulate(dg_ref, dg)

    row = pl.BlockSpec((tm, d), lambda i: (i, 0))
    vec = pl.BlockSpec((1, d), lambda i: (0, 0))
    big = jax.ShapeDtypeStruct((m, d), F32)
    return pl.pallas_call(
        body, name=name, grid=(m // tm,), in_specs=[row, row, row, vec], out_specs=[row, row, vec],
        out_shape=[big, big, jax.ShapeDtypeStruct((1, d), F32)], compiler_params=_params(),
    )(dh2, gp, pp, gpost)


def _loss_head(h, gf, target, *, name):
    m, d = h.shape
    tm = _row_block(m, 512)

    def body(h_ref, g_ref, t_ref, loss_ref, dh_ref, dg_ref):
        x = h_ref[...]
        g = g_ref[...]
        err = _rms(x, g) - t_ref[...]
        part = 0.5 * jnp.sum(jnp.mean(err * err, axis=-1, keepdims=True), axis=0, keepdims=True)
        dx, dg = _rms_bwd(x, g, err * (1.0 / d))
        dh_ref[...] = dx
        _accumulate(dg_ref, dg)
        _accumulate(loss_ref, jnp.broadcast_to(part, (8, LANES)))

    row = pl.BlockSpec((tm, d), lambda i: (i, 0))
    vec = pl.BlockSpec((1, d), lambda i: (0, 0))
    return pl.pallas_call(
        body, name=name, grid=(m // tm,), in_specs=[row, vec, row],
        out_specs=[pl.BlockSpec((8, LANES), lambda i: (0, 0)), row, vec],
        out_shape=[jax.ShapeDtypeStruct((8, LANES), F32), jax.ShapeDtypeStruct((m, d), F32), jax.ShapeDtypeStruct((1, d), F32)],
        compiler_params=_params(),
    )(h, gf, target)


def _hgrn_pre(aq, af, lbv):
    sq = jax.nn.sigmoid(aq)
    sg = jax.nn.sigmoid(-af)
    k = (1.0 - lbv) * sg
    return aq * sq, sq, sg, k, jnp.log1p(-k)


def _tri(n, cmp):
    r = lax.broadcasted_iota(jnp.int32, (n, n), 0)
    c = lax.broadcasted_iota(jnp.int32, (n, n), 1)
    return cmp(r, c).astype(BF16)


def _hgrn_intra(q, k, b, b_scr):
    col = lax.broadcasted_iota(jnp.int32, (SUB, CHUNK), 1)
    rows = [jnp.zeros((SUB, CHUNK), F32)]
    facs = [None]
    for i in range(1, CHUNK // SUB):
        r = b_scr[SUB * i - 1:SUB * i, :]
        fq = jnp.exp(b[SUB * i:SUB * (i + 1)] - r)
        fk = jnp.exp(jnp.minimum(r - b, 0.0))
        qs = q[SUB * i:SUB * (i + 1)] * fq
        ks = k * fk
        rows.append(jnp.where(col < SUB * i, _dot(qs, ks, NT), 0.0))
        facs.append((fq, fk, qs, ks))
    return jnp.concatenate(rows, axis=0), facs


def _hgrn_diag_weights(i, k, b, q_scr, b_scr):
    sub = lax.broadcasted_iota(jnp.int32, (SUB, LANES), 0)
    ki = k[SUB * i:SUB * (i + 1)]
    bi = b[SUB * i:SUB * (i + 1)]
    es, ws = [], []
    for tau in range(SUB):
        t = SUB * i + tau
        e = jnp.exp(jnp.where(sub <= tau, b_scr[t:t + 1, :] - bi, NEG_BIG))
        es.append(e)
        ws.append((q_scr[t:t + 1, :] * ki) * e)
    return es, jnp.concatenate(ws, axis=0)


def _hgrn_fwd(proj, lb_row, *, name):
    s = proj.shape[0]
    tb = _row_block(s, 512)
    nch = tb // CHUNK

    def body(aq_ref, af_ref, ai_ref, lb_ref, o_ref, st_ref, s_scr, q_scr, b_scr, od_scr):
        @pl.when(pl.program_id(1) == 0)
        def _():
            s_scr[...] = jnp.zeros_like(s_scr)

        lbv = lb_ref[...]
        tril = _tri(CHUNK, lambda r, c: r >= c)
        ones = jnp.ones((LANES, LANES), BF16)

        def chunk(ci, carry):
            r0 = pl.multiple_of(ci * CHUNK, CHUNK)
            q, _, _, k, g = _hgrn_pre(aq_ref[pl.ds(r0, CHUNK), :], af_ref[pl.ds(r0, CHUNK), :], lbv)
            v = ai_ref[pl.ds(r0, CHUNK), :]
            st0 = s_scr[...]
            st_ref[ci] = st0
            b = _dot_exact_lhs(tril, g, 3)
            b_scr[...] = b
            q_scr[...] = q
            bl = b_scr[CHUNK - 1:CHUNK, :]
            o = _dot(q * jnp.exp(b), st0, NT)
            a, _ = _hgrn_intra(q, k, b, b_scr)
            o = o + _dot(a, v, NN)
            for i in range(CHUNK // SUB):
                _, wst = _hgrn_diag_weights(i, k, b, q_scr, b_scr)
                rep = _dot(wst, ones, NN)
                vi = v[SUB * i:SUB * (i + 1)]
                for tau in range(SUB):
                    t = SUB * i + tau
                    od_scr[t:t + 1, :] = jnp.sum(rep[SUB * tau:SUB * (tau + 1)] * vi, axis=0, keepdims=True)
            o_ref[pl.ds(r0, CHUNK), :] = o + od_scr[...]
            s_scr[...] = st0 * jnp.exp(bl) + _dot(v, k * jnp.exp(bl - b), TN)
            return carry

        lax.fori_loop(0, nch, chunk, 0)

    def col(block0):
        return pl.BlockSpec((tb, HEAD_A), lambda h, t: (t, block0 + h))

    return pl.pallas_call(
        body, name=name, grid=(A_HEADS, s // tb),
        in_specs=[col(0), col(A_HEADS), col(2 * A_HEADS), pl.BlockSpec((1, HEAD_A), lambda h, t: (0, h))],
        out_specs=[pl.BlockSpec((tb, HEAD_A), lambda h, t: (t, h)),
                   pl.BlockSpec((None, nch, HEAD_A, HEAD_A), lambda h, t: (h, t, 0, 0))],
        out_shape=[jax.ShapeDtypeStruct((s, A_HEADS * HEAD_A), F32),
                   jax.ShapeDtypeStruct((A_HEADS, s // CHUNK, HEAD_A, HEAD_A), F32)],
        scratch_shapes=[pltpu.VMEM((HEAD_A, HEAD_A), F32), pltpu.VMEM((CHUNK, HEAD_A), F32),
                        pltpu.VMEM((CHUNK, HEAD_A), F32), pltpu.VMEM((CHUNK, HEAD_A), F32)],
        compiler_params=_params(),
    )(proj, proj, proj, lb_row)


def _hgrn_bwd(proj, lb_row, states, do_a, *, name):
    s = proj.shape[0]
    tb = _row_block(s, 512)
    nch = tb // CHUNK
    nblk = s // tb

    def body(aq_ref, af_ref, ai_ref, lb_ref, st_ref, do_ref, daq_ref, daf_ref, dai_ref, dlb_ref,
             ds_scr, q_scr, b_scr, dq_scr, do_scr):
        @pl.when(pl.program_id(1) == 0)
        def _():
            ds_scr[...] = jnp.zeros_like(ds_scr)
            dlb_ref[...] = jnp.zeros_like(dlb_ref)

        lbv = lb_ref[...]
        tril = _tri(CHUNK, lambda r, c: r >= c)
        triu = _tri(CHUNK, lambda r, c: c >= r)
        ones = jnp.ones((LANES, LANES), BF16)
        col = lax.broadcasted_iota(jnp.int32, (SUB, CHUNK), 1)
        row = lax.broadcasted_iota(jnp.int32, (CHUNK, LANES), 0)
        nsub = CHUNK // SUB

        def chunk(cj, carry):
            ci = nch - 1 - cj
            r0 = pl.multiple_of(ci * CHUNK, CHUNK)
            aq = aq_ref[pl.ds(r0, CHUNK), :]
            q, sq, sg, k, g = _hgrn_pre(aq, af_ref[pl.ds(r0, CHUNK), :], lbv)
            v = ai_ref[pl.ds(r0, CHUNK), :]
            do = do_ref[pl.ds(r0, CHUNK), :]
            st0 = st_ref[ci]
            dst1 = ds_scr[...]
            b = _dot_exact_lhs(tril, g, 3)
            b_scr[...] = b
            q_scr[...] = q
            do_scr[...] = do
            bl = b_scr[CHUNK - 1:CHUNK, :]
            eb = jnp.exp(b)
            ebl = jnp.exp(bl)
            ekd = jnp.exp(bl - b)
            kd = k * ekd
            a, facs = _hgrn_intra(q, k, b, b_scr)

            da = _dot(do, v, NT)
            dv = _dot(a, do, TN) + _dot(kd, dst1, NT)
            dq = _dot(do, st0, NN) * eb
            dk_inter = _dot(v, dst1, NN) * ekd
            dk = dk_inter
            dq_rows = [jnp.zeros((SUB, LANES), F32)]
            for i in range(1, nsub):
                fq, fk, qs, ks = facs[i]
                da_i = jnp.where(col < SUB * i, da[SUB * i:SUB * (i + 1)], 0.0)
                dq_rows.append(_dot(da_i, ks, NN) * fq)
                dk = dk + _dot(da_i, qs, TN) * fk
            dq = dq + jnp.concatenate(dq_rows, axis=0)

            dv_rows, dk_rows = [], []
            for i in range(nsub):
                es, wst = _hgrn_diag_weights(i, k, b, q_scr, b_scr)
                ki = k[SUB * i:SUB * (i + 1)]
                vi = v[SUB * i:SUB * (i + 1)]
                rep = _dot(wst, ones, NN)
                mst = jnp.concatenate([do_scr[SUB * i + tau:SUB * i + tau + 1, :] * vi for tau in range(SUB)], axis=0)
                drep = _dot(mst, ones, NN)
                dvi = jnp.zeros((SUB, LANES), F32)
                dki = jnp.zeros((SUB, LANES), F32)
                for tau in range(SUB):
                    t = SUB * i + tau
                    gt = drep[SUB * tau:SUB * (tau + 1)] * es[tau]
                    dvi = dvi + rep[SUB * tau:SUB * (tau + 1)] * do_scr[t:t + 1, :]
                    dki = dki + gt * q_scr[t:t + 1, :]
                    dq_scr[t:t + 1, :] = jnp.sum(gt * ki, axis=0, keepdims=True)
                dv_rows.append(dvi)
                dk_rows.append(dki)
            dv = dv + jnp.concatenate(dv_rows, axis=0)
            dk = dk + jnp.concatenate(dk_rows, axis=0)
            dq = dq + dq_scr[...]

            ds_scr[...] = dst1 * ebl + _dot(do, q * eb, TN)
            last = jnp.sum(k * dk_inter, axis=0, keepdims=True) + ebl * jnp.sum(st0 * dst1, axis=0, keepdims=True)
            db = q * dq - k * dk + jnp.where(row == CHUNK - 1, last, 0.0)
            dg = _dot_exact_lhs(triu, db, 3)

            dkt = dk - dg / (1.0 - k)
            daq_ref[pl.ds(r0, CHUNK), :] = dq * (sq * (1.0 + aq * (1.0 - sq)))
            daf_ref[pl.ds(r0, CHUNK), :] = dkt * (1.0 - lbv) * (-(sg * (1.0 - sg)))
            dai_ref[pl.ds(r0, CHUNK), :] = dv
            dlb_ref[...] += jnp.sum(dkt * (-sg), axis=0, keepdims=True)
            return carry

        lax.fori_loop(0, nch, chunk, 0)

    def col_in(block0):
        return pl.BlockSpec((tb, HEAD_A), lambda h, t: (nblk - 1 - t, block0 + h))

    head_col = pl.BlockSpec((tb, HEAD_A), lambda h, t: (nblk - 1 - t, h))
    vec = pl.BlockSpec((1, HEAD_A), lambda h, t: (0, h))
    big = jax.ShapeDtypeStruct((s, A_HEADS * HEAD_A), F32)
    return pl.pallas_call(
        body, name=name, grid=(A_HEADS, nblk),
        in_specs=[col_in(0), col_in(A_HEADS), col_in(2 * A_HEADS), vec,
                  pl.BlockSpec((None, nch, HEAD_A, HEAD_A), lambda h, t: (h, nblk - 1 - t, 0, 0)), head_col],
        out_specs=[head_col, head_col, head_col, vec],
        out_shape=[big, big, big, jax.ShapeDtypeStruct((1, A_HEADS * HEAD_A), F32)],
        scratch_shapes=[pltpu.VMEM((HEAD_A, HEAD_A), F32), pltpu.VMEM((CHUNK, HEAD_A), F32),
                        pltpu.VMEM((CHUNK, HEAD_A), F32), pltpu.VMEM((CHUNK, HEAD_A), F32),
                        pltpu.VMEM((CHUNK, HEAD_A), F32)],
        compiler_params=_params(),
    )(proj, proj, proj, lb_row, states, do_a)


_BQ_BLOCK = 16
_BK_BLOCK = 20
_BV_BLOCK = 24
SB_SCALE = HEAD_B ** -0.5
SB_TQ = 512
SB_TK = 256


def _sb_blocks(s):
    tq = min(SB_TQ, s)
    tk = min(SB_TK, s)
    assert s % tq == 0 and tq % tk == 0 and s // tk <= LANES
    return tq, tk


def _sb2_weights(qh, kb, carry, incl, mask):
    z = _dot(qh, kb, NT)
    sp = jnp.maximum(z, 0.0) + jnp.log(1.0 + jnp.exp(-jnp.abs(z)))
    if mask is not None:
        sp = jnp.where(mask, sp, 0.0)
    suffix = _dot(sp, incl, NN)
    w = jnp.exp(z - suffix + carry)
    if mask is not None:
        w = jnp.where(mask, w, 0.0)
    return z, sp, w, suffix[:, 0:1]


def _sb_mask(qi, j, tq, tk):
    row = qi * tq + lax.broadcasted_iota(jnp.int32, (tq, tk), 0)
    col = j * tk + lax.broadcasted_iota(jnp.int32, (tq, tk), 1)
    return col < row


def _ride_along(ex, refs, n_in, n_out, first, last):
    if ex is None:
        return refs, lambda: None
    own = refs[:n_in] + refs[n_in + ex.n:n_in + ex.n + n_out]
    xin = refs[n_in:n_in + ex.n]
    xout = refs[n_in + ex.n + n_out:n_in + 2 * ex.n + n_out]
    rest = refs[n_in + 2 * ex.n + n_out:]
    sems, scratch = rest[len(rest) - 3:], rest[:len(rest) - 3]

    @pl.when(first)
    def _():
        ex.start(xin, xout, sems)

    def finish():
        @pl.when(last)
        def _():
            ex.finish(xin, xout, sems)

    return own + scratch, finish


def _sb2_fwd(proj, *, name, exchange=None):
    s = proj.shape[0]
    tq, tk = _sb_blocks(s)
    nq, ratio = s // tq, tq // tk
    ex = None if exchange is None else _Exchange(*exchange)

    def body(*refs):
        first = jnp.logical_and(pl.program_id(0) == 0, pl.program_id(1) == 0)
        last = jnp.logical_and(pl.program_id(0) == B_PAIRS - 1, pl.program_id(1) == nq - 1)
        (q_ref, k_ref, v_ref, o_ref, c_ref), finish = _ride_along(ex, refs, 3, 2, first, last)
        qi = pl.program_id(1)
        lane = lax.broadcasted_iota(jnp.int32, (1, LANES), 1)
        low = lane < HEAD_B

        def heads(x):
            return jnp.where(low, x, 0.0).astype(BF16), jnp.where(low, 0.0, x).astype(BF16)

        qh = heads(q_ref[...] * SB_SCALE)
        incl = _tri(tk, lambda r, c: r >= c)
        o_ref[...] = jnp.zeros_like(o_ref)
        c_ref[...] = jnp.zeros_like(c_ref)

        def tile(j, c, masked):
            s0 = pl.multiple_of(j * tk, tk)
            kb = k_ref[pl.ds(s0, tk), :].astype(BF16)
            vh = heads(v_ref[pl.ds(s0, tk), :])
            mask = _sb_mask(qi, j, tq, tk) if masked else None
            c_new = []
            acc = jnp.zeros((tq, LANES), F32)
            for h in range(2):
                _, _, w, rowsum = _sb2_weights(qh[h], kb, c[h], incl, mask)
                c_ref[h] = jnp.where(lane == j, c[h], c_ref[h])
                c_new.append(c[h] - rowsum)
                acc = acc + _dot(w, vh[h], NN)
            o_ref[...] += acc
            return tuple(c_new)

        zc = jnp.zeros((tq, 1), F32)
        c = (zc, zc)
        for r in reversed(range(ratio)):
            c = tile(qi * ratio + r, c, True)
        lax.fori_loop(0, qi * ratio, lambda jj, c: tile(qi * ratio - 1 - jj, c, False), c)
        finish()

    def whole(block0):
        return pl.BlockSpec((s, LANES), lambda p, i: (0, block0 + p))

    in_specs = [pl.BlockSpec((tq, LANES), lambda p, i: (i, _BQ_BLOCK + p)), whole(_BK_BLOCK), whole(_BV_BLOCK)]
    out_specs = [pl.BlockSpec((tq, LANES), lambda p, i: (i, p)), pl.BlockSpec((None, 2, tq, LANES), lambda p, i: (p, 0, i, 0))]
    out_shape = [jax.ShapeDtypeStruct((s, B_PAIRS * LANES), F32), jax.ShapeDtypeStruct((B_PAIRS, 2, s, LANES), F32)]
    args, scratch = [proj, proj, proj], []
    if ex is not None:
        in_specs, out_specs, out_shape = in_specs + ex.in_specs, out_specs + ex.out_specs, out_shape + ex.out_shape
        args, scratch = args + list(exchange[0]), ex.scratch_shapes
    out = pl.pallas_call(
        body, name=name, grid=(B_PAIRS, nq), in_specs=in_specs, out_specs=out_specs, out_shape=out_shape,
        scratch_shapes=scratch, compiler_params=_params(),
    )(*args)
    return out[0], out[1], out[2:]


def _sb2_bwd(proj, carries, do_b, *, name, exchange=None):
    s = proj.shape[0]
    tq, tk = _sb_blocks(s)
    nq, nk, ratio = s // tq, s // tk, tq // tk
    ex = None if exchange is None else _Exchange(*exchange)

    def body(*refs):
        first = jnp.logical_and(pl.program_id(0) == 0, pl.program_id(1) == 0)
        last = jnp.logical_and(pl.program_id(0) == B_PAIRS - 1, pl.program_id(1) == nq - 1)
        own, finish = _ride_along(ex, refs, 5, 3, first, last)
        q_ref, k_ref, v_ref, c_ref, do_ref, dq_ref, dk_ref, dv_ref, dkt_scr, dvt_scr = own
        qi = pl.program_id(1)

        @pl.when(qi == 0)
        def _():
            dkt_scr[...] = jnp.zeros_like(dkt_scr)
            dvt_scr[...] = jnp.zeros_like(dvt_scr)

        lane = lax.broadcasted_iota(jnp.int32, (1, LANES), 1)
        low = lane < HEAD_B
        low_row = lax.broadcasted_iota(jnp.int32, (LANES, 1), 0) < HEAD_B

        def heads(x):
            return jnp.where(low, x, 0.0).astype(BF16), jnp.where(low, 0.0, x).astype(BF16)

        def heads_t(x):
            xt = x.T
            return jnp.where(low_row, xt, 0.0).astype(BF16), jnp.where(low_row, 0.0, xt).astype(BF16)

        q = q_ref[...] * SB_SCALE
        do = do_ref[...]
        qh, qth = heads(q), heads_t(q)
        doh, doth = heads(do), heads_t(do)
        cst = (c_ref[0], c_ref[1])
        incl = _tri(tk, lambda r, c: r >= c)
        pre = _tri(tk, lambda r, c: r <= c)

        def tile(j, carry, masked):
            ec, dq = carry
            s0 = pl.multiple_of(j * tk, tk)
            kf = k_ref[pl.ds(s0, tk), :]
            kb = kf.astype(BF16)
            kh = heads(kf)
            vb = v_ref[pl.ds(s0, tk), :].astype(BF16)
            mask = _sb_mask(qi, j, tq, tk) if masked else None
            ec_new = []
            dkt = jnp.zeros((LANES, tk), F32)
            dvt = jnp.zeros((LANES, tk), F32)
            for h in range(2):
                cin = jnp.sum(jnp.where(lane == j, cst[h], 0.0), axis=1, keepdims=True)
                z, sp, w, _ = _sb2_weights(qh[h], kb, cin, incl, mask)
                e = w * _dot(doh[h], vb, NT)
                prefix = _dot(e, pre, NN)
                dz = e - jnp.exp(z - sp) * (ec[h] + prefix)
                if mask is not None:
                    dz = jnp.where(mask, dz, 0.0)
                ec_new.append(ec[h] + prefix[:, tk - 1:tk])
                dzb = dz.astype(BF16)
                wb = w.astype(BF16)
                dq = dq + _dot(dzb, kh[h], NN)
                dkt = dkt + _dot(qth[h], dzb, NN)
                dvt = dvt + _dot(doth[h], wb, NN)
            dkt_scr[j] += dkt
            dvt_scr[j] += dvt
            return tuple(ec_new), dq

        zc = jnp.zeros((tq, 1), F32)
        carry = lax.fori_loop(0, qi * ratio, lambda j, carry: tile(j, carry, False), ((zc, zc), jnp.zeros((tq, LANES), F32)))
        for r in range(ratio):
            carry = tile(qi * ratio + r, carry, True)
        dq_ref[...] = carry[1] * SB_SCALE

        @pl.when(qi == nq - 1)
        def _():
            for j in range(nk):
                dk_ref[tk * j:tk * (j + 1), :] = dkt_scr[j].T
                dv_ref[tk * j:tk * (j + 1), :] = dvt_scr[j].T

        finish()

    def whole_in(block0):
        return pl.BlockSpec((s, LANES), lambda p, i: (0, block0 + p))

    blk = pl.BlockSpec((tq, LANES), lambda p, i: (i, p))
    whole_out = pl.BlockSpec((s, LANES), lambda p, i: (0, p))
    big = jax.ShapeDtypeStruct((s, B_PAIRS * LANES), F32)
    in_specs = [pl.BlockSpec((tq, LANES), lambda p, i: (i, _BQ_BLOCK + p)), whole_in(_BK_BLOCK), whole_in(_BV_BLOCK),
                pl.BlockSpec((None, 2, tq, LANES), lambda p, i: (p, 0, i, 0)), blk]
    out_specs, out_shape = [blk, whole_out, whole_out], [big, big, big]
    args = [proj, proj, proj, carries, do_b]
    scratch = [pltpu.VMEM((nk, LANES, tk), F32), pltpu.VMEM((nk, LANES, tk), F32)]
    if ex is not None:
        in_specs, out_specs, out_shape = in_specs + ex.in_specs, out_specs + ex.out_specs, out_shape + ex.out_shape
        args, scratch = args + list(exchange[0]), scratch + ex.scratch_shapes
    out = pl.pallas_call(
        body, name=name, grid=(B_PAIRS, nq), in_specs=in_specs, out_specs=out_specs, out_shape=out_shape,
        scratch_shapes=scratch, compiler_params=_params(),
    )(*args)
    return out[0], out[1], out[2], out[3:]


def _my_id():
    return 4 * lax.axis_index("x") + 2 * lax.axis_index("y") + lax.axis_index("c")


def _mesh_pos(d):
    return (d // 4, (d // 2) % 2, d % 2)


class _Exchange:
    def __init__(self, arrays, scatter):
        self.n = len(arrays)
        self.scatter = scatter
        self.in_specs = [pl.BlockSpec(memory_space=pl.ANY)] * self.n
        self.out_specs = [pl.BlockSpec(memory_space=pl.ANY)] * self.n
        self.out_shape = [
            jax.ShapeDtypeStruct(((N_DEV,) + a.shape[1:]) if scatter else ((N_DEV,) + a.shape), a.dtype) for a in arrays]
        self.scratch_shapes = [pltpu.SemaphoreType.DMA((N_DEV - 1, self.n)), pltpu.SemaphoreType.DMA((N_DEV - 1, self.n)),
                               pltpu.SemaphoreType.DMA((self.n,))]

    def _copies(self, ins, outs, sems):
        send_sems, recv_sems, local_sems = sems
        me = _my_id()

        def src(i, to):
            return ins[i].at[to] if self.scatter else ins[i]

        local = [pltpu.make_async_copy(src(i, me), outs[i].at[me], local_sems.at[i]) for i in range(self.n)]
        sends, recvs = [], []
        for step in range(1, N_DEV):
            to = (me + step) % N_DEV
            frm = (me + N_DEV - step) % N_DEV
            for i in range(self.n):
                sends.append(pltpu.make_async_remote_copy(
                    src_ref=src(i, to), dst_ref=outs[i].at[me],
                    send_sem=send_sems.at[step - 1, i], recv_sem=recv_sems.at[step - 1, i],
                    device_id=_mesh_pos(to), device_id_type=pl.DeviceIdType.MESH))
                recvs.append(pltpu.make_async_remote_copy(
                    src_ref=src(i, frm), dst_ref=outs[i].at[frm],
                    send_sem=send_sems.at[step - 1, i], recv_sem=recv_sems.at[step - 1, i],
                    device_id=_mesh_pos(frm), device_id_type=pl.DeviceIdType.MESH))
        return local, sends, recvs

    def start(self, ins, outs, sems):
        local, sends, _ = self._copies(ins, outs, sems)
        for cp in local + sends:
            cp.start()

    def finish(self, ins, outs, sems):
        local, sends, recvs = self._copies(ins, outs, sems)
        for cp in recvs:
            cp.wait_recv()
        for cp in sends:
            cp.wait_send()
        for cp in local:
            cp.wait()


def _exchange(arrays, scatter, *, name):
    ex = _Exchange(arrays, scatter)

    def body(*refs):
        ins, outs, sems = refs[:ex.n], refs[ex.n:2 * ex.n], refs[2 * ex.n:]
        ex.start(ins, outs, sems)
        ex.finish(ins, outs, sems)

    return pl.pallas_call(
        body, name=name, in_specs=ex.in_specs, out_specs=ex.out_specs, out_shape=ex.out_shape,
        scratch_shapes=ex.scratch_shapes,
    )(*arrays)


def _adamw_math(w, m, v, g):
    m2 = ADAM_B1 * m + (1.0 - ADAM_B1) * g
    v2 = ADAM_B2 * v + (1.0 - ADAM_B2) * (g * g)
    m_hat = m2 / (1.0 - ADAM_B1 ** ADAM_STEP)
    v_hat = v2 / (1.0 - ADAM_B2 ** ADAM_STEP)
    delta = -ADAM_LR * (m_hat / (jnp.sqrt(v_hat) + ADAM_EPS) + ADAM_WD * w)
    return delta, m2, v2


def _slot_sum(ref):
    g = ref[0].astype(F32)
    for d in range(1, N_DEV):
        g = g + ref[d].astype(F32)
    return g


def _adamw(w, m, v, slots, *, name):
    r, c = w.shape
    tr = _row_block(r, 256)

    def body(w_ref, m_ref, v_ref, s_ref, g_ref, d_ref, m2_ref, v2_ref):
        g = _slot_sum(s_ref)
        delta, m2, v2 = _adamw_math(w_ref[...], m_ref[...], v_ref[...], g)
        g_ref[...] = g
        d_ref[...] = delta
        m2_ref[...] = m2
        v2_ref[...] = v2

    row = pl.BlockSpec((tr, c), lambda i: (i, 0))
    shp = jax.ShapeDtypeStruct((r, c), F32)
    return pl.pallas_call(
        body, name=name, grid=(r // tr,),
        in_specs=[row, row, row, pl.BlockSpec((N_DEV, tr, c), lambda i: (0, i, 0))],
        out_specs=[row, row, row, row], out_shape=[shp, shp, shp, shp], compiler_params=_params(),
    )(w, m, v, slots)


def _layer_softmax(l_ref):
    l0 = l_ref[0:1, :]
    l1 = l_ref[1:2, :]
    mx = jnp.maximum(l0, l1)
    e0 = jnp.exp(l0 - mx)
    e1 = jnp.exp(l1 - mx)
    return e0 / (e0 + e1), e1 / (e0 + e1)


def _adamw_lb(w, m, v, slots, *, name):
    def body(w_ref, m_ref, v_ref, s_ref, g_ref, d_ref, m2_ref, v2_ref):
        sm0, sm1 = _layer_softmax(w_ref)
        dl1 = _slot_sum(s_ref) * (sm0 * sm1)
        for row, g in ((0, -dl1), (1, dl1)):
            sl = slice(row, row + 1)
            delta, m2, v2 = _adamw_math(w_ref[sl, :], m_ref[sl, :], v_ref[sl, :], g)
            g_ref[sl, :] = g
            d_ref[sl, :] = delta
            m2_ref[sl, :] = m2
            v2_ref[sl, :] = v2

    shp = jax.ShapeDtypeStruct(w.shape, F32)
    return pl.pallas_call(body, name=name, out_shape=[shp, shp, shp, shp])(w, m, v, slots)


def _lower_bound_1(lb_logits, *, name):
    def body(l_ref, o_ref):
        sm0, sm1 = _layer_softmax(l_ref)
        o_ref[...] = (sm0 + sm1) - sm0

    return pl.pallas_call(body, name=name, out_shape=jax.ShapeDtypeStruct((1, lb_logits.shape[1]), F32))(lb_logits)


_SMALL = ("norm_mix", "a_out_norm", "b_out_norm", "ple_gate_norm", "ple_post_norm")


def _pack_rows(vectors):
    offs, rows = [], 0
    for vec in vectors:
        offs.append(rows)
        rows += vec.shape[0] // LANES
    flat = jnp.concatenate([vec.reshape(-1, LANES) for vec in vectors], axis=0)
    pad = (-rows) % 8
    if pad:
        flat = jnp.concatenate([flat, jnp.zeros((pad, LANES), F32)], axis=0)
    return flat, offs


def kernel(x, p, norm_mix, w_in, a_out_norm, b_out_norm, w_out, lb_logits, ple_gate_norm, w_ple_gate, w_ple_proj, ple_post_norm, final_norm, loss_target, m_norm_mix, m_w_in, m_a_out_norm, m_b_out_norm, m_w_out, m_lb_logits, m_ple_gate_norm, m_w_ple_gate, m_w_ple_proj, m_ple_post_norm, m_final_norm, v_norm_mix, v_w_in, v_a_out_norm, v_b_out_norm, v_w_out, v_lb_logits, v_ple_gate_norm, v_w_ple_gate, v_w_ple_proj, v_ple_post_norm, v_final_norm):
    depth = w_in.shape[0]
    assert depth == 2
    s, d = x.shape[1], x.shape[2]
    h = x.reshape(s, d)
    target = loss_target.reshape(s, d)

    (win0,) = _exchange([w_in[0].astype(BF16)], scatter=False, name="gather_w_in0")
    later = [w_in[1].astype(BF16), w_out.astype(BF16), w_ple_gate.astype(BF16), w_ple_proj.astype(BF16)]
    lb1 = _lower_bound_1(lb_logits, name="lower_bound")
    lbs = [jnp.zeros_like(lb1), lb1]

    saved = []
    for l in range(depth):
        win = win0 if l == 0 else win1
        u = _norm_fwd(h, norm_mix[l:l + 1], name="norm_mix")
        proj = _mm_nn(u, win, name="mm_in")
        o_a, states = _hgrn_fwd(proj, lbs[l], name="hgrn_fwd")
        if l == 0:
            o_b, carries, (win1, wout_g, wpg_g, wpp_g) = _sb2_fwd(proj, name="sb_fwd_gather", exchange=(later, False))
        else:
            o_b, carries, _ = _sb2_fwd(proj, name="sb_fwd")
        wout = wout_g[:, l].reshape(1, d, d)
        wpg = wpg_g[:, l].reshape(1, d, d)
        wpp = jnp.transpose(wpp_g[:, l], (1, 0, 2)).reshape(1, p.shape[-1], d)
        y = _mix_fwd(o_a, o_b, proj, a_out_norm[l:l + 1], b_out_norm[l:l + 1], name="mix_fwd")
        h1 = _mm_nn(y, wout, add=h, name="mm_out")
        r1 = _norm_fwd(h1, ple_gate_norm[l:l + 1], name="norm_gate")
        gp = _mm_nn(r1, wpg, name="mm_gate")
        pl_in = p[l].reshape(s, p.shape[-1])
        pp = _mm_nn(pl_in, wpp, name="mm_ple")
        h2 = _ple_fwd(h1, gp, pp, ple_post_norm[l:l + 1], name="ple_fwd")
        saved.append((h, u, proj, o_a, states, o_b, carries, y, h1, r1, gp, pp, pl_in, win, wout, wpg, wpp))
        h = h2

    loss_part, dh, dg_final = _loss_head(h, final_norm.reshape(1, d), target, name="loss_head")

    big = ("w_in", "w_out", "w_ple_gate", "w_ple_proj")
    grads = [None] * depth
    for l in reversed(range(depth)):
        h0, u, proj, o_a, states, o_b, carries, y, h1, r1, gp, pp, pl_in, win, wout, wpg, wpp = saved[l]
        dgp, dpp, dg_post = _ple_bwd(dh, gp, pp, ple_post_norm[l:l + 1], name="ple_bwd")
        dwpp = _mm_tn(pl_in, dpp, 1, name="mm_ple_dw")
        dwpp = jnp.transpose(dwpp.reshape(p.shape[-1], N_DEV, d // N_DEV), (1, 0, 2))
        dwpg = _mm_tn(r1, dgp, 1, name="mm_gate_dw").reshape(N_DEV, d // N_DEV, d)
        dr1 = _mm_nt(dgp, wpg, name="mm_gate_dx")
        dh1, dg_pg = _norm_bwd(h1, ple_gate_norm[l:l + 1], dr1, dh, name="norm_gate_bwd")
        dwout = _mm_tn(y, dh1, 1, name="mm_out_dw").reshape(N_DEV, d // N_DEV, d)
        dy = _mm_nt(dh1, wout, name="mm_out_dx")
        do_a, do_b, dag, dbg, dg_a, dg_b = _mix_bwd(dy, o_a, o_b, proj, a_out_norm[l:l + 1], b_out_norm[l:l + 1], name="mix_bwd")
        if l == 0:
            ready = [grads[1][n] for n in big] + [dwout, dwpg, dwpp]
            dbq, dbk, dbv, early_slots = _sb2_bwd(proj, carries, do_b, name="sb_bwd_scatter", exchange=(ready, True))
        else:
            dbq, dbk, dbv, _ = _sb2_bwd(proj, carries, do_b, name="sb_bwd")
        daq, daf, dai, dlb = _hgrn_bwd(proj, lbs[l], states, do_a, name="hgrn_bwd")
        dproj = jnp.concatenate([daq, daf, dai, dag, dbq, dbk, dbv, dbg], axis=1)
        dwin = _mm_tn(u, dproj, N_DEV, name="mm_in_dw")
        du = _mm_nt(dproj, win, name="mm_in_dx")
        dh, dg_mix = _norm_bwd(h0, norm_mix[l:l + 1], du, dh1, name="norm_mix_bwd")
        grads[l] = dict(w_in=dwin, w_out=dwout, w_ple_gate=dwpg, w_ple_proj=dwpp, norm_mix=dg_mix, a_out_norm=dg_a,
                        b_out_norm=dg_b, ple_gate_norm=dg_pg, ple_post_norm=dg_post, lb=dlb)

    grad_x = dh.reshape(x.shape)
    loss = lax.psum(loss_part[0, 0], ("x", "y", "c"))

    (win0_slots,) = _exchange([grads[0]["w_in"]], scatter=True, name="scatter_w_in0")
    slots = {("w_in", 0): win0_slots}
    for k, key in enumerate([(n, 1) for n in big] + [(n, 0) for n in big[1:]]):
        slots[key] = early_slots[k]
    results = {}
    for n, w, m, v in zip(big, (w_in, w_out, w_ple_gate, w_ple_proj), (m_w_in, m_w_out, m_w_ple_gate, m_w_ple_proj),
                          (v_w_in, v_w_out, v_w_ple_gate, v_w_ple_proj)):
        per_layer = [_adamw(w[l], m[l], v[l], slots[(n, l)], name="adamw_" + n) for l in range(depth)]
        results[n] = [jnp.stack([per_layer[l][k] for l in range(depth)]) for k in range(4)]

    small_w = dict(norm_mix=norm_mix, a_out_norm=a_out_norm, b_out_norm=b_out_norm, ple_gate_norm=ple_gate_norm,
                   ple_post_norm=ple_post_norm)
    small_m = dict(norm_mix=m_norm_mix, a_out_norm=m_a_out_norm, b_out_norm=m_b_out_norm, ple_gate_norm=m_ple_gate_norm,
                   ple_post_norm=m_ple_post_norm)
    small_v = dict(norm_mix=v_norm_mix, a_out_norm=v_a_out_norm, b_out_norm=v_b_out_norm, ple_gate_norm=v_ple_gate_norm,
                   ple_post_norm=v_ple_post_norm)
    g_vecs = [jnp.concatenate([grads[l][n].reshape(-1) for l in range(depth)]) for n in _SMALL] + [dg_final.reshape(-1)]
    g_pack, offs = _pack_rows(g_vecs)
    w_pack, _ = _pack_rows([small_w[n].reshape(-1) for n in _SMALL] + [final_norm])
    m_pack, _ = _pack_rows([small_m[n].reshape(-1) for n in _SMALL] + [m_final_norm])
    v_pack, _ = _pack_rows([small_v[n].reshape(-1) for n in _SMALL] + [v_final_norm])
    small_slots, lb_slots = _exchange([g_pack, grads[1]["lb"]], scatter=False, name="gather_small_grads")
    packed = _adamw(w_pack, m_pack, v_pack, small_slots, name="adamw_small")
    results["lb_logits"] = _adamw_lb(lb_logits, m_lb_logits, v_lb_logits, lb_slots, name="adamw_lb")
    shapes = [small_w[n].shape for n in _SMALL] + [final_norm.shape]
    for n, off, shp in zip(_SMALL + ("final_norm",), offs, shapes):
        rows = 1
        for dim in shp:
            rows *= dim
        rows //= LANES
        results[n] = [o[off:off + rows].reshape(shp) for o in packed]

    order = ("norm_mix", "w_in", "a_out_norm", "b_out_norm", "w_out", "lb_logits", "ple_gate_norm", "w_ple_gate",
             "w_ple_proj", "ple_post_norm", "final_norm")
    out = [loss, grad_x]
    for k in range(4):
        out += [results[n][k] for n in order]
    return tuple(out)
```

```python
import jax
import jax.numpy as jnp
from jax import lax
from jax.experimental import pallas as pl
from jax.experimental.pallas import tpu as pltpu

F32 = jnp.float32
BF16 = jnp.bfloat16

N_DEV = 8
EPS = 1e-6
A_HEADS = 4
HEAD_A = 128
B_PAIRS = 4
HEAD_B = 64
CHUNK = 64
SUB = 16
HGRN_HEADS_PER_STEP = 2
LANES = 128
NEG_BIG = -1e30

ADAM_LR = 0.001
ADAM_B1 = 0.9
ADAM_B2 = 0.999
ADAM_EPS = 1e-08
ADAM_WD = 0.01
ADAM_STEP = 10

VMEM_LIMIT = 56 * 1024 * 1024

NN = ((1,), (0,))
NT = ((1,), (1,))
TN = ((0,), (0,))


def _dot(a, b, dims):
    return lax.dot_general(a.astype(BF16), b.astype(BF16), (dims, ((), ())), preferred_element_type=F32)


def _split_bf16(x, parts):
    out = []
    r = x
    for i in range(parts):
        p = r.astype(BF16)
        out.append(p)
        if i + 1 < parts:
            r = r - p.astype(F32)
    return out


def _dot_exact_lhs(t, x, parts):
    acc = None
    for p in _split_bf16(x, parts):
        d = lax.dot_general(t, p, (NN, ((), ())), preferred_element_type=F32)
        acc = d if acc is None else acc + d
    return acc


def _params(**kw):
    return pltpu.CompilerParams(vmem_limit_bytes=VMEM_LIMIT, **kw)


def _row_block(m, want):
    t = min(want, m)
    assert m % t == 0
    return t


def _mm_nn(a, w, add=None, *, name):
    m, k = a.shape
    nb, _, nc = w.shape
    tm = _row_block(m, 512)

    def body(*refs):
        if add is None:
            a_ref, w_ref, o_ref = refs
            o_ref[...] = _dot(a_ref[...], w_ref[...], NN)
        else:
            a_ref, w_ref, add_ref, o_ref = refs
            o_ref[...] = _dot(a_ref[...], w_ref[...], NN) + add_ref[...]

    in_specs = [pl.BlockSpec((tm, k), lambda i, j: (i, 0)), pl.BlockSpec((None, k, nc), lambda i, j: (j, 0, 0))]
    args = [a, w]
    if add is not None:
        in_specs.append(pl.BlockSpec((tm, nc), lambda i, j: (i, j)))
        args.append(add)
    return pl.pallas_call(
        body, name=name, grid=(m // tm, nb), in_specs=in_specs,
        out_specs=pl.BlockSpec((tm, nc), lambda i, j: (i, j)),
        out_shape=jax.ShapeDtypeStruct((m, nb * nc), F32), compiler_params=_params(),
    )(*args)


def _mm_nt(g, w, *, name):
    m = g.shape[0]
    nb, k, nc = w.shape
    tm = _row_block(m, 512)

    def body(g_ref, w_ref, o_ref):
        r = _dot(g_ref[...], w_ref[...], NT)

        @pl.when(pl.program_id(1) == 0)
        def _():
            o_ref[...] = r

        @pl.when(pl.program_id(1) != 0)
        def _():
            o_ref[...] += r

    return pl.pallas_call(
        body, name=name, grid=(m // tm, nb),
        in_specs=[pl.BlockSpec((tm, nc), lambda i, j: (i, j)), pl.BlockSpec((None, k, nc), lambda i, j: (j, 0, 0))],
        out_specs=pl.BlockSpec((tm, k), lambda i, j: (i, 0)),
        out_shape=jax.ShapeDtypeStruct((m, k), F32), compiler_params=_params(),
    )(g, w)


def _mm_tn(a, g, nb, *, name):
    m, k = a.shape
    nc = g.shape[1] // nb
    tm = _row_block(m, 512)
    steps = m // tm

    def body(a_ref, g_ref, o_ref, acc_ref):
        r = _dot(a_ref[...], g_ref[...], TN)

        @pl.when(pl.program_id(1) == 0)
        def _():
            acc_ref[...] = r

        @pl.when(pl.program_id(1) != 0)
        def _():
            acc_ref[...] += r

        @pl.when(pl.program_id(1) == steps - 1)
        def _():
            o_ref[...] = acc_ref[...].astype(BF16)

    return pl.pallas_call(
        body, name=name, grid=(nb, steps),
        in_specs=[pl.BlockSpec((tm, k), lambda j, i: (i, 0)), pl.BlockSpec((tm, nc), lambda j, i: (i, j))],
        out_specs=pl.BlockSpec((None, k, nc), lambda j, i: (j, 0, 0)),
        out_shape=jax.ShapeDtypeStruct((nb, k, nc), BF16), scratch_shapes=[pltpu.VMEM((k, nc), F32)],
        compiler_params=_params(),
    )(a, g)


def _rms(x, g):
    return x * lax.rsqrt(jnp.mean(x * x, axis=-1, keepdims=True) + EPS) * g


def _rms_bwd(x, g, dy):
    r = lax.rsqrt(jnp.mean(x * x, axis=-1, keepdims=True) + EPS)
    xh = x * r
    dxh = dy * g
    dx = r * (dxh - xh * jnp.mean(dxh * xh, axis=-1, keepdims=True))
    return dx, jnp.sum(dy * xh, axis=0, keepdims=True)


def _accumulate(ref, val):
    @pl.when(pl.program_id(0) == 0)
    def _():
        ref[...] = val

    @pl.when(pl.program_id(0) != 0)
    def _():
        ref[...] += val


def _norm_fwd(x, g, *, name):
    m, d = x.shape
    tm = _row_block(m, 512)

    def body(x_ref, g_ref, o_ref):
        o_ref[...] = _rms(x_ref[...], g_ref[...]).astype(BF16)

    return pl.pallas_call(
        body, name=name, grid=(m // tm,),
        in_specs=[pl.BlockSpec((tm, d), lambda i: (i, 0)), pl.BlockSpec((1, d), lambda i: (0, 0))],
        out_specs=pl.BlockSpec((tm, d), lambda i: (i, 0)),
        out_shape=jax.ShapeDtypeStruct((m, d), BF16), compiler_params=_params(),
    )(x, g)


def _norm_bwd(x, g, dy, add, *, name):
    m, d = x.shape
    tm = _row_block(m, 512)

    def body(x_ref, g_ref, dy_ref, add_ref, dx_ref, dg_ref):
        dx, dg = _rms_bwd(x_ref[...], g_ref[...], dy_ref[...])
        dx_ref[...] = dx + add_ref[...]
        _accumulate(dg_ref, dg)

    row = pl.BlockSpec((tm, d), lambda i: (i, 0))
    vec = pl.BlockSpec((1, d), lambda i: (0, 0))
    return pl.pallas_call(
        body, name=name, grid=(m // tm,), in_specs=[row, vec, row, row], out_specs=[row, vec],
        out_shape=[jax.ShapeDtypeStruct((m, d), F32), jax.ShapeDtypeStruct((1, d), F32)], compiler_params=_params(),
    )(x, g, dy, add)


def _silu(x):
    return x * jax.nn.sigmoid(x)


def _mix_block(oa, ob, ag, bg, ga, gb):
    parts = []
    for h in range(A_HEADS):
        o = oa[:, HEAD_A * h:HEAD_A * (h + 1)]
        parts.append(o * lax.rsqrt(jnp.mean(o * o, axis=-1, keepdims=True) + EPS))
    ya = jnp.concatenate(parts, axis=1) * ga * _silu(ag)
    low = lax.broadcasted_iota(jnp.int32, (1, LANES), 1) < HEAD_B
    parts = []
    for p in range(B_PAIRS):
        o = ob[:, LANES * p:LANES * (p + 1)]
        sq = o * o
        s_lo = jnp.sum(jnp.where(low, sq, 0.0), axis=-1, keepdims=True)
        s_hi = jnp.sum(jnp.where(low, 0.0, sq), axis=-1, keepdims=True)
        r = jnp.where(low, lax.rsqrt(s_lo * (1.0 / HEAD_B) + EPS), lax.rsqrt(s_hi * (1.0 / HEAD_B) + EPS))
        parts.append(o * r)
    yb = jnp.concatenate(parts, axis=1) * gb * _silu(bg)
    return jnp.concatenate([ya, yb], axis=1)


_GATE_A_BLOCK = 3
_GATE_B_BLOCK = 7


def _mix_fwd(oa, ob, proj, ga, gb, *, name):
    m, w = oa.shape
    tm = _row_block(m, 512)

    def body(oa_ref, ob_ref, ag_ref, bg_ref, ga_ref, gb_ref, y_ref):
        y_ref[...] = _mix_block(oa_ref[...], ob_ref[...], ag_ref[...], bg_ref[...], ga_ref[...], gb_ref[...]).astype(BF16)

    row = pl.BlockSpec((tm, w), lambda i: (i, 0))
    vec = pl.BlockSpec((1, w), lambda i: (0, 0))
    return pl.pallas_call(
        body, name=name, grid=(m // tm,),
        in_specs=[row, row, pl.BlockSpec((tm, w), lambda i: (i, _GATE_A_BLOCK)),
                  pl.BlockSpec((tm, w), lambda i: (i, _GATE_B_BLOCK)), vec, vec],
        out_specs=pl.BlockSpec((tm, 2 * w), lambda i: (i, 0)),
        out_shape=jax.ShapeDtypeStruct((m, 2 * w), BF16), compiler_params=_params(),
    )(oa, ob, proj, proj, ga, gb)


def _mix_bwd(dy, oa, ob, proj, ga, gb, *, name):
    m, w = oa.shape
    tm = _row_block(m, 256)

    def body(dy_ref, oa_ref, ob_ref, ag_ref, bg_ref, ga_ref, gb_ref, doa_ref, dob_ref, dag_ref, dbg_ref, dga_ref, dgb_ref):
        _, vjp = jax.vjp(_mix_block, oa_ref[...], ob_ref[...], ag_ref[...], bg_ref[...], ga_ref[...], gb_ref[...])
        doa, dob, dag, dbg, dga, dgb = vjp(dy_ref[...])
        doa_ref[...] = doa
        dob_ref[...] = dob
        dag_ref[...] = dag
        dbg_ref[...] = dbg
        _accumulate(dga_ref, dga)
        _accumulate(dgb_ref, dgb)

    row = pl.BlockSpec((tm, w), lambda i: (i, 0))
    vec = pl.BlockSpec((1, w), lambda i: (0, 0))
    big = jax.ShapeDtypeStruct((m, w), F32)
    small = jax.ShapeDtypeStruct((1, w), F32)
    return pl.pallas_call(
        body, name=name, grid=(m // tm,),
        in_specs=[pl.BlockSpec((tm, 2 * w), lambda i: (i, 0)), row, row,
                  pl.BlockSpec((tm, w), lambda i: (i, _GATE_A_BLOCK)), pl.BlockSpec((tm, w), lambda i: (i, _GATE_B_BLOCK)), vec, vec],
        out_specs=[row, row, row, row, vec, vec], out_shape=[big, big, big, big, small, small], compiler_params=_params(),
    )(dy, oa, ob, proj, proj, ga, gb)


def _ple_block(h1, gp, pp, gpost):
    return h1 + jax.nn.sigmoid(gp) * _rms(pp, gpost)


def _ple_fwd(h1, gp, pp, gpost, *, name):
    m, d = h1.shape
    tm = _row_block(m, 512)

    def body(h_ref, gp_ref, pp_ref, g_ref, o_ref):
        o_ref[...] = _ple_block(h_ref[...], gp_ref[...], pp_ref[...], g_ref[...])

    row = pl.BlockSpec((tm, d), lambda i: (i, 0))
    vec = pl.BlockSpec((1, d), lambda i: (0, 0))
    return pl.pallas_call(
        body, name=name, grid=(m // tm,), in_specs=[row, row, row, vec], out_specs=row,
        out_shape=jax.ShapeDtypeStruct((m, d), F32), compiler_params=_params(),
    )(h1, gp, pp, gpost)


def _ple_bwd(dh2, gp, pp, gpost, *, name):
    m, d = dh2.shape
    tm = _row_block(m, 512)

    def body(dh_ref, gp_ref, pp_ref, g_ref, dgp_ref, dpp_ref, dg_ref):
        dh = dh_ref[...]
        gate = jax.nn.sigmoid(gp_ref[...])
        pe = _rms(pp_ref[...], g_ref[...])
        dgp_ref[...] = dh * pe * (gate * (1.0 - gate))
        dpp, dg = _rms_bwd(pp_ref[...], g_ref[...], dh * gate)
        dpp_ref[...] = dpp
        _accumulate(dg_ref, dg)

    row = pl.BlockSpec((tm, d), lambda i: (i, 0))
    vec = pl.BlockSpec((1, d), lambda i: (0, 0))
    big = jax.ShapeDtypeStruct((m, d), F32)
    return pl.pallas_call(
        body, name=name, grid=(m // tm,), in_specs=[row, row, row, vec], out_specs=[row, row, vec],
        out_shape=[big, big, jax.ShapeDtypeStruct((1, d), F32)], compiler_params=_params(),
    )(dh2, gp, pp, gpost)


def _loss_head(h, gf, target, *, name):
    m, d = h.shape
    tm = _row_block(m, 512)

    def body(h_ref, g_ref, t_ref, loss_ref, dh_ref, dg_ref):
        x = h_ref[...]
        g = g_ref[...]
        err = _rms(x, g) - t_ref[...]
        part = 0.5 * jnp.sum(jnp.mean(err * err, axis=-1, keepdims=True), axis=0, keepdims=True)
        dx, dg = _rms_bwd(x, g, err * (1.0 / d))
        dh_ref[...] = dx
        _accumulate(dg_ref, dg)
        _accumulate(loss_ref, jnp.broadcast_to(part, (8, LANES)))

    row = pl.BlockSpec((tm, d), lambda i: (i, 0))
    vec = pl.BlockSpec((1, d), lambda i: (0, 0))
    return pl.pallas_call(
        body, name=name, grid=(m // tm,), in_specs=[row, vec, row],
        out_specs=[pl.BlockSpec((8, LANES), lambda i: (0, 0)), row, vec],
        out_shape=[jax.ShapeDtypeStruct((8, LANES), F32), jax.ShapeDtypeStruct((m, d), F32), jax.ShapeDtypeStruct((1, d), F32)],
        compiler_params=_params(),
    )(h, gf, target)


def _hgrn_pre(aq, af, lbv):
    sq = jax.nn.sigmoid(aq)
    sg = jax.nn.sigmoid(-af)
    k = (1.0 - lbv) * sg
    return aq * sq, sq, sg, k, jnp.log1p(-k)


def _tri(n, cmp):
    r = lax.broadcasted_iota(jnp.int32, (n, n), 0)
    c = lax.broadcasted_iota(jnp.int32, (n, n), 1)
    return cmp(r, c).astype(BF16)


def _hgrn_intra(q, k, b, b_scr):
    col = lax.broadcasted_iota(jnp.int32, (SUB, CHUNK), 1)
    rows = [jnp.zeros((SUB, CHUNK), F32)]
    facs = [None]
    for i in range(1, CHUNK // SUB):
        r = b_scr[SUB * i - 1:SUB * i, :]
        fq = jnp.exp(b[SUB * i:SUB * (i + 1)] - r)
        fk = jnp.exp(jnp.minimum(r - b, 0.0))
        qs = q[SUB * i:SUB * (i + 1)] * fq
        ks = k * fk
        rows.append(jnp.where(col < SUB * i, _dot(qs, ks, NT), 0.0))
        facs.append((fq, fk, qs, ks))
    return jnp.concatenate(rows, axis=0), facs


def _hgrn_diag_weights(i, k, b, q_scr, b_scr):
    sub = lax.broadcasted_iota(jnp.int32, (SUB, LANES), 0)
    ki = k[SUB * i:SUB * (i + 1)]
    bi = b[SUB * i:SUB * (i + 1)]
    es, ws = [], []
    for tau in range(SUB):
        t = SUB * i + tau
        e = jnp.exp(jnp.where(sub <= tau, b_scr[t:t + 1, :] - bi, NEG_BIG))
        es.append(e)
        ws.append((q_scr[t:t + 1, :] * ki) * e)
    return es, jnp.concatenate(ws, axis=0)


def _hgrn_fwd(proj, lb_row, *, name):
    s = proj.shape[0]
    tb = _row_block(s, 512)
    nch = tb // CHUNK
    wide = HGRN_HEADS_PER_STEP * HEAD_A

    def body(aq_all, af_all, ai_all, lb_all, o_all, st_all, s_all, q_all, b_all, od_all):
        @pl.when(pl.program_id(1) == 0)
        def _():
            s_all[...] = jnp.zeros_like(s_all)

        tril = _tri(CHUNK, lambda r, c: r >= c)
        ones = jnp.ones((LANES, LANES), BF16)

        def head_chunk(hh, ci):
            lanes = slice(HEAD_A * hh, HEAD_A * (hh + 1))
            aq_ref, af_ref, ai_ref, o_ref = aq_all.at[:, lanes], af_all.at[:, lanes], ai_all.at[:, lanes], o_all.at[:, lanes]
            st_ref, s_scr, q_scr, b_scr, od_scr = st_all.at[hh], s_all.at[hh], q_all.at[hh], b_all.at[hh], od_all.at[hh]
            lbv = lb_all[:, lanes]
            r0 = pl.multiple_of(ci * CHUNK, CHUNK)
            q, _, _, k, g = _hgrn_pre(aq_ref[pl.ds(r0, CHUNK), :], af_ref[pl.ds(r0, CHUNK), :], lbv)
            v = ai_ref[pl.ds(r0, CHUNK), :]
            st0 = s_scr[...]
            st_ref[ci] = st0
            b = _dot_exact_lhs(tril, g, 3)
            b_scr[...] = b
            q_scr[...] = q
            bl = b_scr[CHUNK - 1:CHUNK, :]
            o = _dot(q * jnp.exp(b), st0, NT)
            a, _ = _hgrn_intra(q, k, b, b_scr)
            o = o + _dot(a, v, NN)
            for i in range(CHUNK // SUB):
                _, wst = _hgrn_diag_weights(i, k, b, q_scr, b_scr)
                rep = _dot(wst, ones, NN)
                vi = v[SUB * i:SUB * (i + 1)]
                for tau in range(SUB):
                    t = SUB * i + tau
                    od_scr[t:t + 1, :] = jnp.sum(rep[SUB * tau:SUB * (tau + 1)] * vi, axis=0, keepdims=True)
            o_ref[pl.ds(r0, CHUNK), :] = o + od_scr[...]
            s_scr[...] = st0 * jnp.exp(bl) + _dot(v, k * jnp.exp(bl - b), TN)

        def chunk(ci, carry):
            for hh in range(HGRN_HEADS_PER_STEP):
                head_chunk(hh, ci)
            return carry

        lax.fori_loop(0, nch, chunk, 0)

    def col(block0):
        return pl.BlockSpec((tb, wide), lambda h, t: (t, block0 + h))

    steps = A_HEADS // HGRN_HEADS_PER_STEP
    per_head = pltpu.VMEM((HGRN_HEADS_PER_STEP, CHUNK, HEAD_A), F32)
    return pl.pallas_call(
        body, name=name, grid=(steps, s // tb),
        in_specs=[col(0), col(steps), col(2 * steps), pl.BlockSpec((1, wide), lambda h, t: (0, h))],
        out_specs=[pl.BlockSpec((tb, wide), lambda h, t: (t, h)),
                   pl.BlockSpec((HGRN_HEADS_PER_STEP, nch, HEAD_A, HEAD_A), lambda h, t: (h, t, 0, 0))],
        out_shape=[jax.ShapeDtypeStruct((s, A_HEADS * HEAD_A), F32),
                   jax.ShapeDtypeStruct((A_HEADS, s // CHUNK, HEAD_A, HEAD_A), F32)],
        scratch_shapes=[pltpu.VMEM((HGRN_HEADS_PER_STEP, HEAD_A, HEAD_A), F32), per_head, per_head, per_head],
        compiler_params=_params(),
    )(proj, proj, proj, lb_row)


def _hgrn_bwd(proj, lb_row, states, do_a, *, name):
    s = proj.shape[0]
    tb = _row_block(s, 512)
    nch = tb // CHUNK
    nblk = s // tb

    wide = HGRN_HEADS_PER_STEP * HEAD_A

    def body(aq_all, af_all, ai_all, lb_all, st_all, do_all, daq_all, daf_all, dai_all, dlb_all,
             ds_all, q_all, b_all, dq_all, dos_all):
        @pl.when(pl.program_id(1) == 0)
        def _():
            ds_all[...] = jnp.zeros_like(ds_all)
            dlb_all[...] = jnp.zeros_like(dlb_all)

        tril = _tri(CHUNK, lambda r, c: r >= c)
        triu = _tri(CHUNK, lambda r, c: c >= r)
        ones = jnp.ones((LANES, LANES), BF16)
        col = lax.broadcasted_iota(jnp.int32, (SUB, CHUNK), 1)
        row = lax.broadcasted_iota(jnp.int32, (CHUNK, LANES), 0)
        nsub = CHUNK // SUB

        def head_chunk(hh, ci):
            lanes = slice(HEAD_A * hh, HEAD_A * (hh + 1))
            aq_ref, af_ref, ai_ref, do_ref = aq_all.at[:, lanes], af_all.at[:, lanes], ai_all.at[:, lanes], do_all.at[:, lanes]
            daq_ref, daf_ref, dai_ref, dlb_ref = daq_all.at[:, lanes], daf_all.at[:, lanes], dai_all.at[:, lanes], dlb_all.at[:, lanes]
            st_ref, ds_scr, q_scr, b_scr = st_all.at[hh], ds_all.at[hh], q_all.at[hh], b_all.at[hh]
            dq_scr, do_scr = dq_all.at[hh], dos_all.at[hh]
            lbv = lb_all[:, lanes]
            r0 = pl.multiple_of(ci * CHUNK, CHUNK)
            aq = aq_ref[pl.ds(r0, CHUNK), :]
            q, sq, sg, k, g = _hgrn_pre(aq, af_ref[pl.ds(r0, CHUNK), :], lbv)
            v = ai_ref[pl.ds(r0, CHUNK), :]
            do = do_ref[pl.ds(r0, CHUNK), :]
            st0 = st_ref[ci]
            dst1 = ds_scr[...]
            b = _dot_exact_lhs(tril, g, 3)
            b_scr[...] = b
            q_scr[...] = q
            do_scr[...] = do
            bl = b_scr[CHUNK - 1:CHUNK, :]
            eb = jnp.exp(b)
            ebl = jnp.exp(bl)
            ekd = jnp.exp(bl - b)
            kd = k * ekd
            a, facs = _hgrn_intra(q, k, b, b_scr)

            da = _dot(do, v, NT)
            dv = _dot(a, do, TN) + _dot(kd, dst1, NT)
            dq = _dot(do, st0, NN) * eb
            dk_inter = _dot(v, dst1, NN) * ekd
            dk = dk_inter
            dq_rows = [jnp.zeros((SUB, LANES), F32)]
            for i in range(1, nsub):
                fq, fk, qs, ks = facs[i]
                da_i = jnp.where(col < SUB * i, da[SUB * i:SUB * (i + 1)], 0.0)
                dq_rows.append(_dot(da_i, ks, NN) * fq)
                dk = dk + _dot(da_i, qs, TN) * fk
            dq = dq + jnp.concatenate(dq_rows, axis=0)

            dv_rows, dk_rows = [], []
            for i in range(nsub):
                es, wst = _hgrn_diag_weights(i, k, b, q_scr, b_scr)
                ki = k[SUB * i:SUB * (i + 1)]
                vi = v[SUB * i:SUB * (i + 1)]
                rep = _dot(wst, ones, NN)
                mst = jnp.concatenate([do_scr[SUB * i + tau:SUB * i + tau + 1, :] * vi for tau in range(SUB)], axis=0)
                drep = _dot(mst, ones, NN)
                dvi = jnp.zeros((SUB, LANES), F32)
                dki = jnp.zeros((SUB, LANES), F32)
                for tau in range(SUB):
                    t = SUB * i + tau
                    gt = drep[SUB * tau:SUB * (tau + 1)] * es[tau]
                    dvi = dvi + rep[SUB * tau:SUB * (tau + 1)] * do_scr[t:t + 1, :]
                    dki = dki + gt * q_scr[t:t + 1, :]
                    dq_scr[t:t + 1, :] = jnp.sum(gt * ki, axis=0, keepdims=True)
                dv_rows.append(dvi)
                dk_rows.append(dki)
            dv = dv + jnp.concatenate(dv_rows, axis=0)
            dk = dk + jnp.concatenate(dk_rows, axis=0)
            dq = dq + dq_scr[...]

            ds_scr[...] = dst1 * ebl + _dot(do, q * eb, TN)
            last = jnp.sum(k * dk_inter, axis=0, keepdims=True) + ebl * jnp.sum(st0 * dst1, axis=0, keepdims=True)
            db = q * dq - k * dk + jnp.where(row == CHUNK - 1, last, 0.0)
            dg = _dot_exact_lhs(triu, db, 3)

            dkt = dk - dg / (1.0 - k)
            daq_ref[pl.ds(r0, CHUNK), :] = dq * (sq * (1.0 + aq * (1.0 - sq)))
            daf_ref[pl.ds(r0, CHUNK), :] = dkt * (1.0 - lbv) * (-(sg * (1.0 - sg)))
            dai_ref[pl.ds(r0, CHUNK), :] = dv
            dlb_ref[...] += jnp.sum(dkt * (-sg), axis=0, keepdims=True)

        def chunk(cj, carry):
            for hh in range(HGRN_HEADS_PER_STEP):
                head_chunk(hh, nch - 1 - cj)
            return carry

        lax.fori_loop(0, nch, chunk, 0)

    def col_in(block0):
        return pl.BlockSpec((tb, wide), lambda h, t: (nblk - 1 - t, block0 + h))

    steps = A_HEADS // HGRN_HEADS_PER_STEP
    head_col = pl.BlockSpec((tb, wide), lambda h, t: (nblk - 1 - t, h))
    vec = pl.BlockSpec((1, wide), lambda h, t: (0, h))
    big = jax.ShapeDtypeStruct((s, A_HEADS * HEAD_A), F32)
    per_head = pltpu.VMEM((HGRN_HEADS_PER_STEP, CHUNK, HEAD_A), F32)
    return pl.pallas_call(
        body, name=name, grid=(steps, nblk),
        in_specs=[col_in(0), col_in(steps), col_in(2 * steps), vec,
                  pl.BlockSpec((HGRN_HEADS_PER_STEP, nch, HEAD_A, HEAD_A), lambda h, t: (h, nblk - 1 - t, 0, 0)), head_col],
        out_specs=[head_col, head_col, head_col, vec],
        out_shape=[big, big, big, jax.ShapeDtypeStruct((1, A_HEADS * HEAD_A), F32)],
        scratch_shapes=[pltpu.VMEM((HGRN_HEADS_PER_STEP, HEAD_A, HEAD_A), F32), per_head, per_head, per_head, per_head],
        compiler_params=_params(),
    )(proj, proj, proj, lb_row, states, do_a)


_BQ_BLOCK = 16
_BK_BLOCK = 20
_BV_BLOCK = 24
SB_SCALE = HEAD_B ** -0.5
SB_TQ = 512
SB_TK = 256


def _sb_blocks(s):
    tq = min(SB_TQ, s)
    tk = min(SB_TK, s)
    assert s % tq == 0 and tq % tk == 0 and s // tk <= LANES
    return tq, tk, (2 if (tq // tk) % 2 == 0 else 1)


def _sb2_weights(qh, kb, carry, incl, mask):
    z = _dot(qh, kb, NT)
    sp = jnp.maximum(z, 0.0) + jnp.log(1.0 + jnp.exp(-jnp.abs(z)))
    if mask is not None:
        sp = jnp.where(mask, sp, 0.0)
    suffix = _dot(sp, incl, NN)
    w = jnp.exp(z - suffix + carry)
    if mask is not None:
        w = jnp.where(mask, w, 0.0)
    return z, sp, w, suffix[:, 0:1]


def _sb_mask(qi, j, tq, tk, row0):
    row = qi * tq + row0 + lax.broadcasted_iota(jnp.int32, (tq - row0, tk), 0)
    col = j * tk + lax.broadcasted_iota(jnp.int32, (tq - row0, tk), 1)
    return col < row


def _below(x, row0, new_tail):
    return new_tail if row0 == 0 else jnp.concatenate([x[:row0], new_tail], axis=0)


def _ride_along(ex, refs, n_in, n_out, first, last):
    if ex is None:
        return refs, lambda: None
    own = refs[:n_in] + refs[n_in + ex.n:n_in + ex.n + n_out]
    xin = refs[n_in:n_in + ex.n]
    xout = refs[n_in + ex.n + n_out:n_in + 2 * ex.n + n_out]
    rest = refs[n_in + 2 * ex.n + n_out:]
    sems, scratch = rest[len(rest) - 3:], rest[:len(rest) - 3]

    @pl.when(first)
    def _():
        ex.start(xin, xout, sems)

    def finish():
        @pl.when(last)
        def _():
            ex.finish(xin, xout, sems)

    return own + scratch, finish


def _sb2_fwd(proj, *, name, exchange=None):
    s = proj.shape[0]
    tq, tk, unroll = _sb_blocks(s)
    nq, ratio = s // tq, tq // tk
    ex = None if exchange is None else _Exchange(*exchange)

    def body(*refs):
        first = jnp.logical_and(pl.program_id(0) == 0, pl.program_id(1) == 0)
        last = jnp.logical_and(pl.program_id(0) == B_PAIRS - 1, pl.program_id(1) == nq - 1)
        (q_ref, k_ref, v_ref, o_ref, c_ref), finish = _ride_along(ex, refs, 3, 2, first, last)
        qi = pl.program_id(1)
        lane = lax.broadcasted_iota(jnp.int32, (1, LANES), 1)
        low = lane < HEAD_B

        def heads(x):
            return jnp.where(low, x, 0.0).astype(BF16), jnp.where(low, 0.0, x).astype(BF16)

        qh = heads(q_ref[...] * SB_SCALE)
        incl = _tri(tk, lambda r, c: r >= c)
        o_ref[...] = jnp.zeros_like(o_ref)
        c_ref[...] = jnp.zeros_like(c_ref)

        def tile(j, c, masked, row0=0):
            s0 = pl.multiple_of(j * tk, tk)
            kb = k_ref[pl.ds(s0, tk), :].astype(BF16)
            vh = heads(v_ref[pl.ds(s0, tk), :])
            mask = _sb_mask(qi, j, tq, tk, row0) if masked else None
            c_new = []
            acc = jnp.zeros((tq - row0, LANES), F32)
            for h in range(2):
                _, _, w, rowsum = _sb2_weights(qh[h][row0:], kb, c[h][row0:], incl, mask)
                c_ref[h] = jnp.where(lane == j, c[h], c_ref[h])
                c_new.append(_below(c[h], row0, c[h][row0:] - rowsum))
                acc = acc + _dot(w, vh[h], NN)
            o_ref[row0:, :] += acc
            return tuple(c_new)

        zc = jnp.zeros((tq, 1), F32)
        c = (zc, zc)
        for r in reversed(range(ratio)):
            c = tile(qi * ratio + r, c, True, r * tk)
        def several(jj, c):
            for r in range(unroll):
                c = tile(qi * ratio - 1 - (jj * unroll + r), c, False)
            return c

        lax.fori_loop(0, qi * (ratio // unroll), several, c)
        finish()

    def whole(block0):
        return pl.BlockSpec((s, LANES), lambda p, i: (0, block0 + p))

    in_specs = [pl.BlockSpec((tq, LANES), lambda p, i: (i, _BQ_BLOCK + p)), whole(_BK_BLOCK), whole(_BV_BLOCK)]
    out_specs = [pl.BlockSpec((tq, LANES), lambda p, i: (i, p)), pl.BlockSpec((None, 2, tq, LANES), lambda p, i: (p, 0, i, 0))]
    out_shape = [jax.ShapeDtypeStruct((s, B_PAIRS * LANES), F32), jax.ShapeDtypeStruct((B_PAIRS, 2, s, LANES), F32)]
    args, scratch = [proj, proj, proj], []
    if ex is not None:
        in_specs, out_specs, out_shape = in_specs + ex.in_specs, out_specs + ex.out_specs, out_shape + ex.out_shape
        args, scratch = args + list(exchange[0]), ex.scratch_shapes
    out = pl.pallas_call(
        body, name=name, grid=(B_PAIRS, nq), in_specs=in_specs, out_specs=out_specs, out_shape=out_shape,
        scratch_shapes=scratch, compiler_params=_params(),
    )(*args)
    return out[0], out[1], out[2:]


def _sb2_bwd(proj, carries, do_b, *, name, exchange=None):
    s = proj.shape[0]
    tq, tk, unroll = _sb_blocks(s)
    nq, nk, ratio = s // tq, s // tk, tq // tk
    ex = None if exchange is None else _Exchange(*exchange)

    def body(*refs):
        first = jnp.logical_and(pl.program_id(0) == 0, pl.program_id(1) == 0)
        last = jnp.logical_and(pl.program_id(0) == B_PAIRS - 1, pl.program_id(1) == nq - 1)
        own, finish = _ride_along(ex, refs, 5, 3, first, last)
        q_ref, k_ref, v_ref, c_ref, do_ref, dq_ref, dk_ref, dv_ref, dkt_scr, dvt_scr = own
        qi = pl.program_id(1)

        @pl.when(qi == 0)
        def _():
            dkt_scr[...] = jnp.zeros_like(dkt_scr)
            dvt_scr[...] = jnp.zeros_like(dvt_scr)

        lane = lax.broadcasted_iota(jnp.int32, (1, LANES), 1)
        low = lane < HEAD_B
        low_row = lax.broadcasted_iota(jnp.int32, (LANES, 1), 0) < HEAD_B

        def heads(x):
            return jnp.where(low, x, 0.0).astype(BF16), jnp.where(low, 0.0, x).astype(BF16)

        def heads_t(x):
            xt = x.T
            return jnp.where(low_row, xt, 0.0).astype(BF16), jnp.where(low_row, 0.0, xt).astype(BF16)

        q = q_ref[...] * SB_SCALE
        do = do_ref[...]
        qh, qth = heads(q), heads_t(q)
        doh, doth = heads(do), heads_t(do)
        cst = (c_ref[0], c_ref[1])
        incl = _tri(tk, lambda r, c: r >= c)
        pre = _tri(tk, lambda r, c: r <= c)

        def tile(j, carry, masked, row0=0):
            ec, dq = carry
            s0 = pl.multiple_of(j * tk, tk)
            kf = k_ref[pl.ds(s0, tk), :]
            kb = kf.astype(BF16)
            kh = heads(kf)
            vb = v_ref[pl.ds(s0, tk), :].astype(BF16)
            mask = _sb_mask(qi, j, tq, tk, row0) if masked else None
            ec_new = []
            dkt = jnp.zeros((LANES, tk), F32)
            dvt = jnp.zeros((LANES, tk), F32)
            dq_add = jnp.zeros((tq - row0, LANES), F32)
            for h in range(2):
                cin = jnp.sum(jnp.where(lane == j, cst[h][row0:], 0.0), axis=1, keepdims=True)
                z, sp, w, _ = _sb2_weights(qh[h][row0:], kb, cin, incl, mask)
                e = w * _dot(doh[h][row0:], vb, NT)
                prefix = _dot(e, pre, NN)
                dz = e - jnp.exp(z - sp) * (ec[h][row0:] + prefix)
                if mask is not None:
                    dz = jnp.where(mask, dz, 0.0)
                ec_new.append(_below(ec[h], row0, ec[h][row0:] + prefix[:, tk - 1:tk]))
                dzb = dz.astype(BF16)
                wb = w.astype(BF16)
                dq_add = dq_add + _dot(dzb, kh[h], NN)
                dkt = dkt + _dot(qth[h][:, row0:], dzb, NN)
                dvt = dvt + _dot(doth[h][:, row0:], wb, NN)
            dkt_scr[j] += dkt
            dvt_scr[j] += dvt
            return tuple(ec_new), _below(dq, row0, dq[row0:] + dq_add)

        zc = jnp.zeros((tq, 1), F32)
        def several(jj, carry):
            for r in range(unroll):
                carry = tile(jj * unroll + r, carry, False)
            return carry

        carry = lax.fori_loop(0, qi * (ratio // unroll), several, ((zc, zc), jnp.zeros((tq, LANES), F32)))
        for r in range(ratio):
            carry = tile(qi * ratio + r, carry, True, r * tk)
        dq_ref[...] = carry[1] * SB_SCALE

        @pl.when(qi == nq - 1)
        def _():
            for j in range(nk):
                dk_ref[tk * j:tk * (j + 1), :] = dkt_scr[j].T
                dv_ref[tk * j:tk * (j + 1), :] = dvt_scr[j].T

        finish()

    def whole_in(block0):
        return pl.BlockSpec((s, LANES), lambda p, i: (0, block0 + p))

    blk = pl.BlockSpec((tq, LANES), lambda p, i: (i, p))
    whole_out = pl.BlockSpec((s, LANES), lambda p, i: (0, p))
    big = jax.ShapeDtypeStruct((s, B_PAIRS * LANES), F32)
    in_specs = [pl.BlockSpec((tq, LANES), lambda p, i: (i, _BQ_BLOCK + p)), whole_in(_BK_BLOCK), whole_in(_BV_BLOCK),
                pl.BlockSpec((None, 2, tq, LANES), lambda p, i: (p, 0, i, 0)), blk]
    out_specs, out_shape = [blk, whole_out, whole_out], [big, big, big]
    args = [proj, proj, proj, carries, do_b]
    scratch = [pltpu.VMEM((nk, LANES, tk), F32), pltpu.VMEM((nk, LANES, tk), F32)]
    if ex is not None:
        in_specs, out_specs, out_shape = in_specs + ex.in_specs, out_specs + ex.out_specs, out_shape + ex.out_shape
        args, scratch = args + list(exchange[0]), scratch + ex.scratch_shapes
    out = pl.pallas_call(
        body, name=name, grid=(B_PAIRS, nq), in_specs=in_specs, out_specs=out_specs, out_shape=out_shape,
        scratch_shapes=scratch, compiler_params=_params(),
    )(*args)
    return out[0], out[1], out[2], out[3:]


def _my_id():
    return 4 * lax.axis_index("x") + 2 * lax.axis_index("y") + lax.axis_index("c")


def _mesh_pos(d):
    return (d // 4, (d // 2) % 2, d % 2)


class _Exchange:
    def __init__(self, arrays, scatter):
        self.n = len(arrays)
        self.scatter = scatter
        self.in_specs = [pl.BlockSpec(memory_space=pl.ANY)] * self.n
        self.out_specs = [pl.BlockSpec(memory_space=pl.ANY)] * self.n
        self.out_shape = [
            jax.ShapeDtypeStruct(((N_DEV,) + a.shape[1:]) if scatter else ((N_DEV,) + a.shape), a.dtype) for a in arrays]
        self.scratch_shapes = [pltpu.SemaphoreType.DMA((N_DEV - 1, self.n)), pltpu.SemaphoreType.DMA((N_DEV - 1, self.n)),
                               pltpu.SemaphoreType.DMA((self.n,))]

    def _copies(self, ins, outs, sems):
        send_sems, recv_sems, local_sems = sems
        me = _my_id()

        def src(i, to):
            return ins[i].at[to] if self.scatter else ins[i]

        local = [pltpu.make_async_copy(src(i, me), outs[i].at[me], local_sems.at[i]) for i in range(self.n)]
        sends, recvs = [], []
        for step in range(1, N_DEV):
            to = (me + step) % N_DEV
            frm = (me + N_DEV - step) % N_DEV
            for i in range(self.n):
                sends.append(pltpu.make_async_remote_copy(
                    src_ref=src(i, to), dst_ref=outs[i].at[me],
                    send_sem=send_sems.at[step - 1, i], recv_sem=recv_sems.at[step - 1, i],
                    device_id=_mesh_pos(to), device_id_type=pl.DeviceIdType.MESH))
                recvs.append(pltpu.make_async_remote_copy(
                    src_ref=src(i, frm), dst_ref=outs[i].at[frm],
                    send_sem=send_sems.at[step - 1, i], recv_sem=recv_sems.at[step - 1, i],
                    device_id=_mesh_pos(frm), device_id_type=pl.DeviceIdType.MESH))
        return local, sends, recvs

    def start(self, ins, outs, sems):
        local, sends, _ = self._copies(ins, outs, sems)
        for cp in local + sends:
            cp.start()

    def finish(self, ins, outs, sems):
        local, sends, recvs = self._copies(ins, outs, sems)
        for cp in recvs:
            cp.wait_recv()
        for cp in sends:
            cp.wait_send()
        for cp in local:
            cp.wait()


def _exchange(arrays, scatter, *, name):
    ex = _Exchange(arrays, scatter)

    def body(*refs):
        ins, outs, sems = refs[:ex.n], refs[ex.n:2 * ex.n], refs[2 * ex.n:]
        ex.start(ins, outs, sems)
        ex.finish(ins, outs, sems)

    return pl.pallas_call(
        body, name=name, in_specs=ex.in_specs, out_specs=ex.out_specs, out_shape=ex.out_shape,
        scratch_shapes=ex.scratch_shapes,
    )(*arrays)


def _adamw_math(w, m, v, g):
    m2 = ADAM_B1 * m + (1.0 - ADAM_B1) * g
    v2 = ADAM_B2 * v + (1.0 - ADAM_B2) * (g * g)
    m_hat = m2 / (1.0 - ADAM_B1 ** ADAM_STEP)
    v_hat = v2 / (1.0 - ADAM_B2 ** ADAM_STEP)
    delta = -ADAM_LR * (m_hat / (jnp.sqrt(v_hat) + ADAM_EPS) + ADAM_WD * w)
    return delta, m2, v2


def _slot_sum(ref):
    g = ref[0].astype(F32)
    for d in range(1, N_DEV):
        g = g + ref[d].astype(F32)
    return g


def _adamw(w, m, v, slots, *, name):
    r, c = w.shape
    tr = _row_block(r, 256)

    def body(w_ref, m_ref, v_ref, s_ref, g_ref, d_ref, m2_ref, v2_ref):
        g = _slot_sum(s_ref)
        delta, m2, v2 = _adamw_math(w_ref[...], m_ref[...], v_ref[...], g)
        g_ref[...] = g
        d_ref[...] = delta
        m2_ref[...] = m2
        v2_ref[...] = v2

    row = pl.BlockSpec((tr, c), lambda i: (i, 0))
    shp = jax.ShapeDtypeStruct((r, c), F32)
    return pl.pallas_call(
        body, name=name, grid=(r // tr,),
        in_specs=[row, row, row, pl.BlockSpec((N_DEV, tr, c), lambda i: (0, i, 0))],
        out_specs=[row, row, row, row], out_shape=[shp, shp, shp, shp], compiler_params=_params(),
    )(w, m, v, slots)


def _layer_softmax(l_ref):
    l0 = l_ref[0:1, :]
    l1 = l_ref[1:2, :]
    mx = jnp.maximum(l0, l1)
    e0 = jnp.exp(l0 - mx)
    e1 = jnp.exp(l1 - mx)
    return e0 / (e0 + e1), e1 / (e0 + e1)


def _adamw_lb(w, m, v, slots, *, name):
    def body(w_ref, m_ref, v_ref, s_ref, g_ref, d_ref, m2_ref, v2_ref):
        sm0, sm1 = _layer_softmax(w_ref)
        dl1 = _slot_sum(s_ref) * (sm0 * sm1)
        for row, g in ((0, -dl1), (1, dl1)):
            sl = slice(row, row + 1)
            delta, m2, v2 = _adamw_math(w_ref[sl, :], m_ref[sl, :], v_ref[sl, :], g)
            g_ref[sl, :] = g
            d_ref[sl, :] = delta
            m2_ref[sl, :] = m2
            v2_ref[sl, :] = v2

    shp = jax.ShapeDtypeStruct(w.shape, F32)
    return pl.pallas_call(body, name=name, out_shape=[shp, shp, shp, shp])(w, m, v, slots)


def _lower_bound_1(lb_logits, *, name):
    def body(l_ref, o_ref):
        sm0, sm1 = _layer_softmax(l_ref)
        o_ref[...] = (sm0 + sm1) - sm0

    return pl.pallas_call(body, name=name, out_shape=jax.ShapeDtypeStruct((1, lb_logits.shape[1]), F32))(lb_logits)


_SMALL = ("norm_mix", "a_out_norm", "b_out_norm", "ple_gate_norm", "ple_post_norm")


def _pack_rows(vectors):
    offs, rows = [], 0
    for vec in vectors:
        offs.append(rows)
        rows += vec.shape[0] // LANES
    flat = jnp.concatenate([vec.reshape(-1, LANES) for vec in vectors], axis=0)
    pad = (-rows) % 8
    if pad:
        flat = jnp.concatenate([flat, jnp.zeros((pad, LANES), F32)], axis=0)
    return flat, offs


def kernel(x, p, norm_mix, w_in, a_out_norm, b_out_norm, w_out, lb_logits, ple_gate_norm, w_ple_gate, w_ple_proj, ple_post_norm, final_norm, loss_target, m_norm_mix, m_w_in, m_a_out_norm, m_b_out_norm, m_w_out, m_lb_logits, m_ple_gate_norm, m_w_ple_gate, m_w_ple_proj, m_ple_post_norm, m_final_norm, v_norm_mix, v_w_in, v_a_out_norm, v_b_out_norm, v_w_out, v_lb_logits, v_ple_gate_norm, v_w_ple_gate, v_w_ple_proj, v_ple_post_norm, v_final_norm):
    depth = w_in.shape[0]
    assert depth == 2
    s, d = x.shape[1], x.shape[2]
    h = x.reshape(s, d)
    target = loss_target.reshape(s, d)

    (win0,) = _exchange([w_in[0].astype(BF16)], scatter=False, name="gather_w_in0")
    later = [w_in[1].astype(BF16), w_out.astype(BF16), w_ple_gate.astype(BF16), w_ple_proj.astype(BF16)]
    lb1 = _lower_bound_1(lb_logits, name="lower_bound")
    lbs = [jnp.zeros_like(lb1), lb1]

    saved = []
    for l in range(depth):
        win = win0 if l == 0 else win1
        u = _norm_fwd(h, norm_mix[l:l + 1], name="norm_mix")
        proj = _mm_nn(u, win, name="mm_in")
        o_a, states = _hgrn_fwd(proj, lbs[l], name="hgrn_fwd")
        if l == 0:
            o_b, carries, (win1, wout_g, wpg_g, wpp_g) = _sb2_fwd(proj, name="sb_fwd_gather", exchange=(later, False))
        else:
            o_b, carries, _ = _sb2_fwd(proj, name="sb_fwd")
        wout = wout_g[:, l].reshape(1, d, d)
        wpg = wpg_g[:, l].reshape(1, d, d)
        wpp = jnp.transpose(wpp_g[:, l], (1, 0, 2)).reshape(1, p.shape[-1], d)
        y = _mix_fwd(o_a, o_b, proj, a_out_norm[l:l + 1], b_out_norm[l:l + 1], name="mix_fwd")
        h1 = _mm_nn(y, wout, add=h, name="mm_out")
        r1 = _norm_fwd(h1, ple_gate_norm[l:l + 1], name="norm_gate")
        gp = _mm_nn(r1, wpg, name="mm_gate")
        pl_in = p[l].reshape(s, p.shape[-1])
        pp = _mm_nn(pl_in, wpp, name="mm_ple")
        h2 = _ple_fwd(h1, gp, pp, ple_post_norm[l:l + 1], name="ple_fwd")
        saved.append((h, u, proj, o_a, states, o_b, carries, y, h1, r1, gp, pp, pl_in, win, wout, wpg, wpp))
        h = h2

    loss_part, dh, dg_final = _loss_head(h, final_norm.reshape(1, d), target, name="loss_head")

    big = ("w_in", "w_out", "w_ple_gate", "w_ple_proj")
    grads = [None] * depth
    for l in reversed(range(depth)):
        h0, u, proj, o_a, states, o_b, carries, y, h1, r1, gp, pp, pl_in, win, wout, wpg, wpp = saved[l]
        dgp, dpp, dg_post = _ple_bwd(dh, gp, pp, ple_post_norm[l:l + 1], name="ple_bwd")
        dwpp = _mm_tn(pl_in, dpp, 1, name="mm_ple_dw")
        dwpp = jnp.transpose(dwpp.reshape(p.shape[-1], N_DEV, d // N_DEV), (1, 0, 2))
        dwpg = _mm_tn(r1, dgp, 1, name="mm_gate_dw").reshape(N_DEV, d // N_DEV, d)
        dr1 = _mm_nt(dgp, wpg, name="mm_gate_dx")
        dh1, dg_pg = _norm_bwd(h1, ple_gate_norm[l:l + 1], dr1, dh, name="norm_gate_bwd")
        dwout = _mm_tn(y, dh1, 1, name="mm_out_dw").reshape(N_DEV, d // N_DEV, d)
        dy = _mm_nt(dh1, wout, name="mm_out_dx")
        do_a, do_b, dag, dbg, dg_a, dg_b = _mix_bwd(dy, o_a, o_b, proj, a_out_norm[l:l + 1], b_out_norm[l:l + 1], name="mix_bwd")
        if l == 0:
            ready = [grads[1][n] for n in big] + [dwout, dwpg, dwpp]
            dbq, dbk, dbv, early_slots = _sb2_bwd(proj, carries, do_b, name="sb_bwd_scatter", exchange=(ready, True))
        else:
            dbq, dbk, dbv, _ = _sb2_bwd(proj, carries, do_b, name="sb_bwd")
        daq, daf, dai, dlb = _hgrn_bwd(proj, lbs[l], states, do_a, name="hgrn_bwd")
        dproj = jnp.concatenate([daq, daf, dai, dag, dbq, dbk, dbv, dbg], axis=1)
        dwin = _mm_tn(u, dproj, N_DEV, name="mm_in_dw")
        du = _mm_nt(dproj, win, name="mm_in_dx")
        dh, dg_mix = _norm_bwd(h0, norm_mix[l:l + 1], du, dh1, name="norm_mix_bwd")
        grads[l] = dict(w_in=dwin, w_out=dwout, w_ple_gate=dwpg, w_ple_proj=dwpp, norm_mix=dg_mix, a_out_norm=dg_a,
                        b_out_norm=dg_b, ple_gate_norm=dg_pg, ple_post_norm=dg_post, lb=dlb)

    grad_x = dh.reshape(x.shape)
    loss = lax.psum(loss_part[0, 0], ("x", "y", "c"))

    (win0_slots,) = _exchange([grads[0]["w_in"]], scatter=True, name="scatter_w_in0")
    slots = {("w_in", 0): win0_slots}
    for k, key in enumerate([(n, 1) for n in big] + [(n, 0) for n in big[1:]]):
        slots[key] = early_slots[k]
    results = {}
    for n, w, m, v in zip(big, (w_in, w_out, w_ple_gate, w_ple_proj), (m_w_in, m_w_out, m_w_ple_gate, m_w_ple_proj),
                          (v_w_in, v_w_out, v_w_ple_gate, v_w_ple_proj)):
        per_layer = [_adamw(w[l], m[l], v[l], slots[(n, l)], name="adamw_" + n) for l in range(depth)]
        results[n] = [jnp.stack([per_layer[l][k] for l in range(depth)]) for k in range(4)]

    small_w = dict(norm_mix=norm_mix, a_out_norm=a_out_norm, b_out_norm=b_out_norm, ple_gate_norm=ple_gate_norm,
                   ple_post_norm=ple_post_norm)
    small_m = dict(norm_mix=m_norm_mix, a_out_norm=m_a_out_norm, b_out_norm=m_b_out_norm, ple_gate_norm=m_ple_gate_norm,
                   ple_post_norm=m_ple_post_norm)
    small_v = dict(norm_mix=v_norm_mix, a_out_norm=v_a_out_norm, b_out_norm=v_b_out_norm, ple_gate_norm=v_ple_gate_norm,
                   ple_post_norm=v_ple_post_norm)
    g_vecs = [jnp.concatenate([grads[l][n].reshape(-1) for l in range(depth)]) for n in _SMALL] + [dg_final.reshape(-1)]
    g_pack, offs = _pack_rows(g_vecs)
    w_pack, _ = _pack_rows([small_w[n].reshape(-1) for n in _SMALL] + [final_norm])
    m_pack, _ = _pack_rows([small_m[n].reshape(-1) for n in _SMALL] + [m_final_norm])
    v_pack, _ = _pack_rows([small_v[n].reshape(-1) for n in _SMALL] + [v_final_norm])
    small_slots, lb_slots = _exchange([g_pack, grads[1]["lb"]], scatter=False, name="gather_small_grads")
    packed = _adamw(w_pack, m_pack, v_pack, small_slots, name="adamw_small")
    results["lb_logits"] = _adamw_lb(lb_logits, m_lb_logits, v_lb_logits, lb_slots, name="adamw_lb")
    shapes = [small_w[n].shape for n in _SMALL] + [final_norm.shape]
    for n, off, shp in zip(_SMALL + ("final_norm",), offs, shapes):
        rows = 1
        for dim in shp:
            rows *= dim
        rows //= LANES
        results[n] = [o[off:off + rows].reshape(shp) for o in packed]

    order = ("norm_mix", "w_in", "a_out_norm", "b_out_norm", "w_out", "lb_logits", "ple_gate_norm", "w_ple_gate",
             "w_ple_proj", "ple_post_norm", "final_norm")
    out = [loss, grad_x]
    for k in range(4):
        out += [results[n][k] for n in order]
    return tuple(out)
```

```python
import jax
import jax.numpy as jnp
from jax import lax
from jax.experimental import pallas as pl
from jax.experimental.pallas import tpu as pltpu

F32 = jnp.float32
BF16 = jnp.bfloat16

N_DEV = 8
EPS = 1e-6
A_HEADS = 4
HEAD_A = 128
B_PAIRS = 4
HEAD_B = 64
CHUNK = 64
SUB = 16
HGRN_HEADS_PER_STEP = 2
LANES = 128
NEG_BIG = -1e30

ADAM_LR = 0.001
ADAM_B1 = 0.9
ADAM_B2 = 0.999
ADAM_EPS = 1e-08
ADAM_WD = 0.01
ADAM_STEP = 10

VMEM_LIMIT = 56 * 1024 * 1024

NN = ((1,), (0,))
NT = ((1,), (1,))
TN = ((0,), (0,))


def _dot(a, b, dims):
    return lax.dot_general(a.astype(BF16), b.astype(BF16), (dims, ((), ())), preferred_element_type=F32)


def _split_bf16(x, parts):
    out = []
    r = x
    for i in range(parts):
        p = r.astype(BF16)
        out.append(p)
        if i + 1 < parts:
            r = r - p.astype(F32)
    return out


def _dot_exact_lhs(t, x, parts):
    acc = None
    for p in _split_bf16(x, parts):
        d = lax.dot_general(t, p, (NN, ((), ())), preferred_element_type=F32)
        acc = d if acc is None else acc + d
    return acc


def _params(**kw):
    return pltpu.CompilerParams(vmem_limit_bytes=VMEM_LIMIT, **kw)


def _row_block(m, want):
    t = min(want, m)
    assert m % t == 0
    return t


def _mm_nn(a, w, add=None, *, name):
    m, k = a.shape
    nb, _, nc = w.shape
    tm = _row_block(m, 512)

    def body(*refs):
        rows = pl.ds(pl.multiple_of(pl.program_id(1) * tm, tm), tm)
        if add is None:
            a_ref, w_ref, o_ref = refs
            o_ref[...] = _dot(a_ref[rows, :], w_ref[...], NN)
        else:
            a_ref, w_ref, add_ref, o_ref = refs
            o_ref[...] = _dot(a_ref[rows, :], w_ref[...], NN) + add_ref[...]

    in_specs = [pl.BlockSpec((m, k), lambda j, i: (0, 0)), pl.BlockSpec((None, k, nc), lambda j, i: (j, 0, 0))]
    args = [a, w]
    if add is not None:
        in_specs.append(pl.BlockSpec((tm, nc), lambda j, i: (i, j)))
        args.append(add)
    return pl.pallas_call(
        body, name=name, grid=(nb, m // tm), in_specs=in_specs,
        out_specs=pl.BlockSpec((tm, nc), lambda j, i: (i, j)),
        out_shape=jax.ShapeDtypeStruct((m, nb * nc), F32), compiler_params=_params(),
    )(*args)


def _mm_nt(pieces, w, *, name, exchange=None):
    m = pieces[0].shape[0]
    nb, k, nc = w.shape
    assert len(pieces) == nb
    tm = _row_block(m, 512)
    steps = m // tm
    ex = None if exchange is None else _Exchange(*exchange)

    def body(*refs):
        first = jnp.logical_and(pl.program_id(0) == 0, pl.program_id(1) == 0)
        last = jnp.logical_and(pl.program_id(0) == steps - 1, pl.program_id(1) == nb - 1)
        own, finish = _ride_along(ex, refs, nb + 1, 1, first, last)
        g_refs, w_ref, o_ref = own[:nb], own[nb], own[nb + 1]
        j = pl.program_id(1)
        for p in range(nb):
            @pl.when(j == p)
            def _(p=p):
                r = _dot(g_refs[p][...], w_ref[p], NT)
                if p == 0:
                    o_ref[...] = r
                else:
                    o_ref[...] += r

        finish()

    in_specs = [pl.BlockSpec((tm, nc), lambda i, j: (i, 0)) for _ in range(nb)]
    in_specs.append(pl.BlockSpec((nb, k, nc), lambda i, j: (0, 0, 0)))
    out_specs, out_shape = [pl.BlockSpec((tm, k), lambda i, j: (i, 0))], [jax.ShapeDtypeStruct((m, k), F32)]
    args, scratch = list(pieces) + [w], []
    if ex is not None:
        in_specs, out_specs, out_shape = in_specs + ex.in_specs, out_specs + ex.out_specs, out_shape + ex.out_shape
        args, scratch = args + list(exchange[0]), ex.scratch_shapes
    out = pl.pallas_call(
        body, name=name, grid=(steps, nb), in_specs=in_specs, out_specs=out_specs, out_shape=out_shape,
        scratch_shapes=scratch, compiler_params=_params(),
    )(*args)
    return out[0], out[1:]


def _mm_tn(a, pieces, *, name):
    m, k = a.shape
    nb, nc = len(pieces), pieces[0].shape[1]
    tm = _row_block(m, 512)
    steps = m // tm

    def body(*refs):
        a_ref, g_refs, o_ref, acc_ref = refs[0], refs[1:1 + nb], refs[1 + nb], refs[2 + nb]
        j, i = pl.program_id(0), pl.program_id(1)
        rows = pl.ds(pl.multiple_of(i * tm, tm), tm)
        for p in range(nb):
            @pl.when(j == p)
            def _(p=p):
                r = _dot(a_ref[rows, :], g_refs[p][...], TN)

                @pl.when(i == 0)
                def _():
                    acc_ref[...] = r

                @pl.when(i != 0)
                def _():
                    acc_ref[...] += r

        @pl.when(i == steps - 1)
        def _():
            o_ref[...] = acc_ref[...].astype(BF16)

    def piece_spec(p):
        return pl.BlockSpec((tm, nc), lambda j, i: (jnp.where(j == p, i, 0), 0))

    return pl.pallas_call(
        body, name=name, grid=(nb, steps),
        in_specs=[pl.BlockSpec((m, k), lambda j, i: (0, 0))] + [piece_spec(p) for p in range(nb)],
        out_specs=pl.BlockSpec((None, k, nc), lambda j, i: (j, 0, 0)),
        out_shape=jax.ShapeDtypeStruct((nb, k, nc), BF16), scratch_shapes=[pltpu.VMEM((k, nc), F32)],
        compiler_params=_params(),
    )(a, *pieces)


def _rms(x, g):
    return x * lax.rsqrt(jnp.mean(x * x, axis=-1, keepdims=True) + EPS) * g


def _rms_bwd(x, g, dy):
    r = lax.rsqrt(jnp.mean(x * x, axis=-1, keepdims=True) + EPS)
    xh = x * r
    dxh = dy * g
    dx = r * (dxh - xh * jnp.mean(dxh * xh, axis=-1, keepdims=True))
    return dx, jnp.sum(dy * xh, axis=0, keepdims=True)


def _accumulate(ref, val):
    @pl.when(pl.program_id(0) == 0)
    def _():
        ref[...] = val

    @pl.when(pl.program_id(0) != 0)
    def _():
        ref[...] += val


def _norm_fwd(x, g, *, name):
    m, d = x.shape
    tm = _row_block(m, 512)

    def body(x_ref, g_ref, o_ref):
        o_ref[...] = _rms(x_ref[...], g_ref[...]).astype(BF16)

    return pl.pallas_call(
        body, name=name, grid=(m // tm,),
        in_specs=[pl.BlockSpec((tm, d), lambda i: (i, 0)), pl.BlockSpec((1, d), lambda i: (0, 0))],
        out_specs=pl.BlockSpec((tm, d), lambda i: (i, 0)),
        out_shape=jax.ShapeDtypeStruct((m, d), BF16), compiler_params=_params(),
    )(x, g)


def _norm_bwd(x, g, dy, add, *, name):
    m, d = x.shape
    tm = _row_block(m, 512)

    def body(x_ref, g_ref, dy_ref, add_ref, dx_ref, dg_ref):
        dx, dg = _rms_bwd(x_ref[...], g_ref[...], dy_ref[...])
        dx_ref[...] = dx + add_ref[...]
        _accumulate(dg_ref, dg)

    row = pl.BlockSpec((tm, d), lambda i: (i, 0))
    vec = pl.BlockSpec((1, d), lambda i: (0, 0))
    return pl.pallas_call(
        body, name=name, grid=(m // tm,), in_specs=[row, vec, row, row], out_specs=[row, vec],
        out_shape=[jax.ShapeDtypeStruct((m, d), F32), jax.ShapeDtypeStruct((1, d), F32)], compiler_params=_params(),
    )(x, g, dy, add)


def _silu(x):
    return x * jax.nn.sigmoid(x)


def _mix_block(oa, ob, ag, bg, ga, gb):
    parts = []
    for h in range(A_HEADS):
        o = oa[:, HEAD_A * h:HEAD_A * (h + 1)]
        parts.append(o * lax.rsqrt(jnp.mean(o * o, axis=-1, keepdims=True) + EPS))
    ya = jnp.concatenate(parts, axis=1) * ga * _silu(ag)
    low = lax.broadcasted_iota(jnp.int32, (1, LANES), 1) < HEAD_B
    parts = []
    for p in range(B_PAIRS):
        o = ob[:, LANES * p:LANES * (p + 1)]
        sq = o * o
        s_lo = jnp.sum(jnp.where(low, sq, 0.0), axis=-1, keepdims=True)
        s_hi = jnp.sum(jnp.where(low, 0.0, sq), axis=-1, keepdims=True)
        r = jnp.where(low, lax.rsqrt(s_lo * (1.0 / HEAD_B) + EPS), lax.rsqrt(s_hi * (1.0 / HEAD_B) + EPS))
        parts.append(o * r)
    yb = jnp.concatenate(parts, axis=1) * gb * _silu(bg)
    return jnp.concatenate([ya, yb], axis=1)


_GATE_A_BLOCK = 3
_GATE_B_BLOCK = 7


def _mix_fwd(oa, ob, proj, ga, gb, *, name):
    m, w = oa.shape
    tm = _row_block(m, 512)

    def body(oa_ref, ob_ref, ag_ref, bg_ref, ga_ref, gb_ref, y_ref):
        y_ref[...] = _mix_block(oa_ref[...], ob_ref[...], ag_ref[...], bg_ref[...], ga_ref[...], gb_ref[...]).astype(BF16)

    row = pl.BlockSpec((tm, w), lambda i: (i, 0))
    vec = pl.BlockSpec((1, w), lambda i: (0, 0))
    return pl.pallas_call(
        body, name=name, grid=(m // tm,),
        in_specs=[row, row, pl.BlockSpec((tm, w), lambda i: (i, _GATE_A_BLOCK)),
                  pl.BlockSpec((tm, w), lambda i: (i, _GATE_B_BLOCK)), vec, vec],
        out_specs=pl.BlockSpec((tm, 2 * w), lambda i: (i, 0)),
        out_shape=jax.ShapeDtypeStruct((m, 2 * w), BF16), compiler_params=_params(),
    )(oa, ob, proj, proj, ga, gb)


def _mix_bwd(dy, oa, ob, proj, ga, gb, *, name):
    m, w = oa.shape
    tm = _row_block(m, 256)

    def body(dy_ref, oa_ref, ob_ref, ag_ref, bg_ref, ga_ref, gb_ref, doa_ref, dob_ref, dag_ref, dbg_ref, dga_ref, dgb_ref):
        _, vjp = jax.vjp(_mix_block, oa_ref[...], ob_ref[...], ag_ref[...], bg_ref[...], ga_ref[...], gb_ref[...])
        doa, dob, dag, dbg, dga, dgb = vjp(dy_ref[...])
        doa_ref[...] = doa
        dob_ref[...] = dob
        dag_ref[...] = dag
        dbg_ref[...] = dbg
        _accumulate(dga_ref, dga)
        _accumulate(dgb_ref, dgb)

    row = pl.BlockSpec((tm, w), lambda i: (i, 0))
    vec = pl.BlockSpec((1, w), lambda i: (0, 0))
    big = jax.ShapeDtypeStruct((m, w), F32)
    small = jax.ShapeDtypeStruct((1, w), F32)
    return pl.pallas_call(
        body, name=name, grid=(m // tm,),
        in_specs=[pl.BlockSpec((tm, 2 * w), lambda i: (i, 0)), row, row,
                  pl.BlockSpec((tm, w), lambda i: (i, _GATE_A_BLOCK)), pl.BlockSpec((tm, w), lambda i: (i, _GATE_B_BLOCK)), vec, vec],
        out_specs=[row, row, row, row, vec, vec], out_shape=[big, big, big, big, small, small], compiler_params=_params(),
    )(dy, oa, ob, proj, proj, ga, gb)


def _ple_block(h1, gp, pp, gpost):
    return h1 + jax.nn.sigmoid(gp) * _rms(pp, gpost)


def _ple_fwd(h1, gp, pp, gpost, *, name):
    m, d = h1.shape
    tm = _row_block(m, 512)

    def body(h_ref, gp_ref, pp_ref, g_ref, o_ref):
        o_ref[...] = _ple_block(h_ref[...], gp_ref[...], pp_ref[...], g_ref[...])

    row = pl.BlockSpec((tm, d), lambda i: (i, 0))
    vec = pl.BlockSpec((1, d), lambda i: (0, 0))
    return pl.pallas_call(
        body, name=name, grid=(m // tm,), in_specs=[row, row, row, vec], out_specs=row,
        out_shape=jax.ShapeDtypeStruct((m, d), F32), compiler_params=_params(),
    )(h1, gp, pp, gpost)


def _ple_bwd(dh2, gp, pp, gpost, *, name):
    m, d = dh2.shape
    tm = _row_block(m, 512)

    def body(dh_ref, gp_ref, pp_ref, g_ref, dgp_ref, dpp_ref, dg_ref):
        dh = dh_ref[...]
        gate = jax.nn.sigmoid(gp_ref[...])
        pe = _rms(pp_ref[...], g_ref[...])
        dgp_ref[...] = dh * pe * (gate * (1.0 - gate))
        dpp, dg = _rms_bwd(pp_ref[...], g_ref[...], dh * gate)
        dpp_ref[...] = dpp
        _accumulate(dg_ref, dg)

    row = pl.BlockSpec((tm, d), lambda i: (i, 0))
    vec = pl.BlockSpec((1, d), lambda i: (0, 0))
    big = jax.ShapeDtypeStruct((m, d), F32)
    return pl.pallas_call(
        body, name=name, grid=(m // tm,), in_specs=[row, row, row, vec], out_specs=[row, row, vec],
        out_shape=[big, big, jax.ShapeDtypeStruct((1, d), F32)], compiler_params=_params(),
    )(dh2, gp, pp, gpost)


def _loss_head(h, gf, target, *, name):
    m, d = h.shape
    tm = _row_block(m, 512)

    def body(h_ref, g_ref, t_ref, loss_ref, dh_ref, dg_ref):
        x = h_ref[...]
        g = g_ref[...]
        err = _rms(x, g) - t_ref[...]
        part = 0.5 * jnp.sum(jnp.mean(err * err, axis=-1, keepdims=True), axis=0, keepdims=True)
        dx, dg = _rms_bwd(x, g, err * (1.0 / d))
        dh_ref[...] = dx
        _accumulate(dg_ref, dg)
        _accumulate(loss_ref, jnp.broadcast_to(part, (8, LANES)))

    row = pl.BlockSpec((tm, d), lambda i: (i, 0))
    vec = pl.BlockSpec((1, d), lambda i: (0, 0))
    return pl.pallas_call(
        body, name=name, grid=(m // tm,), in_specs=[row, vec, row],
        out_specs=[pl.BlockSpec((8, LANES), lambda i: (0, 0)), row, vec],
        out_shape=[jax.ShapeDtypeStruct((8, LANES), F32), jax.ShapeDtypeStruct((m, d), F32), jax.ShapeDtypeStruct((1, d), F32)],
        compiler_params=_params(),
    )(h, gf, target)


def _hgrn_pre(aq, af, lbv):
    sq = jax.nn.sigmoid(aq)
    sg = jax.nn.sigmoid(-af)
    k = (1.0 - lbv) * sg
    return aq * sq, sq, sg, k, jnp.log1p(-k)


def _tri(n, cmp):
    r = lax.broadcasted_iota(jnp.int32, (n, n), 0)
    c = lax.broadcasted_iota(jnp.int32, (n, n), 1)
    return cmp(r, c).astype(BF16)


def _hgrn_intra(q, k, b, b_scr):
    col = lax.broadcasted_iota(jnp.int32, (SUB, CHUNK), 1)
    rows = [jnp.zeros((SUB, CHUNK), F32)]
    facs = [None]
    for i in range(1, CHUNK // SUB):
        r = b_scr[SUB * i - 1:SUB * i, :]
        fq = jnp.exp(b[SUB * i:SUB * (i + 1)] - r)
        fk = jnp.exp(jnp.minimum(r - b, 0.0))
        qs = q[SUB * i:SUB * (i + 1)] * fq
        ks = k * fk
        rows.append(jnp.where(col < SUB * i, _dot(qs, ks, NT), 0.0))
        facs.append((fq, fk, qs, ks))
    return jnp.concatenate(rows, axis=0), facs


def _hgrn_diag_weights(i, k, b, q_scr, b_scr):
    sub = lax.broadcasted_iota(jnp.int32, (SUB, LANES), 0)
    ki = k[SUB * i:SUB * (i + 1)]
    bi = b[SUB * i:SUB * (i + 1)]
    es, ws = [], []
    for tau in range(SUB):
        t = SUB * i + tau
        e = jnp.exp(jnp.where(sub <= tau, b_scr[t:t + 1, :] - bi, NEG_BIG))
        es.append(e)
        ws.append((q_scr[t:t + 1, :] * ki) * e)
    return es, jnp.concatenate(ws, axis=0)


def _hgrn_fwd(proj, lb_row, *, name):
    s = proj.shape[0]
    tb = _row_block(s, 512)
    nch = tb // CHUNK
    wide = HGRN_HEADS_PER_STEP * HEAD_A

    def body(aq_all, af_all, ai_all, lb_all, o_all, st_all, s_all, q_all, b_all, od_all):
        @pl.when(pl.program_id(1) == 0)
        def _():
            s_all[...] = jnp.zeros_like(s_all)

        tril = _tri(CHUNK, lambda r, c: r >= c)
        ones = jnp.ones((LANES, LANES), BF16)

        def head_chunk(hh, ci):
            lanes = slice(HEAD_A * hh, HEAD_A * (hh + 1))
            aq_ref, af_ref, ai_ref, o_ref = aq_all.at[:, lanes], af_all.at[:, lanes], ai_all.at[:, lanes], o_all.at[:, lanes]
            st_ref, s_scr, q_scr, b_scr, od_scr = st_all.at[hh], s_all.at[hh], q_all.at[hh], b_all.at[hh], od_all.at[hh]
            lbv = lb_all[:, lanes]
            r0 = pl.multiple_of(ci * CHUNK, CHUNK)
            q, _, _, k, g = _hgrn_pre(aq_ref[pl.ds(r0, CHUNK), :], af_ref[pl.ds(r0, CHUNK), :], lbv)
            v = ai_ref[pl.ds(r0, CHUNK), :]
            st0 = s_scr[...]
            st_ref[ci] = st0
            b = _dot_exact_lhs(tril, g, 3)
            b_scr[...] = b
            q_scr[...] = q
            bl = b_scr[CHUNK - 1:CHUNK, :]
            o = _dot(q * jnp.exp(b), st0, NT)
            a, _ = _hgrn_intra(q, k, b, b_scr)
            o = o + _dot(a, v, NN)
            for i in range(CHUNK // SUB):
                _, wst = _hgrn_diag_weights(i, k, b, q_scr, b_scr)
                rep = _dot(wst, ones, NN)
                vi = v[SUB * i:SUB * (i + 1)]
                for tau in range(SUB):
                    t = SUB * i + tau
                    od_scr[t:t + 1, :] = jnp.sum(rep[SUB * tau:SUB * (tau + 1)] * vi, axis=0, keepdims=True)
            o_ref[pl.ds(r0, CHUNK), :] = o + od_scr[...]
            s_scr[...] = st0 * jnp.exp(bl) + _dot(v, k * jnp.exp(bl - b), TN)

        def chunk(ci, carry):
            for hh in range(HGRN_HEADS_PER_STEP):
                head_chunk(hh, ci)
            return carry

        lax.fori_loop(0, nch, chunk, 0)

    def col(block0):
        return pl.BlockSpec((tb, wide), lambda h, t: (t, block0 + h))

    steps = A_HEADS // HGRN_HEADS_PER_STEP
    per_head = pltpu.VMEM((HGRN_HEADS_PER_STEP, CHUNK, HEAD_A), F32)
    return pl.pallas_call(
        body, name=name, grid=(steps, s // tb),
        in_specs=[col(0), col(steps), col(2 * steps), pl.BlockSpec((1, wide), lambda h, t: (0, h))],
        out_specs=[pl.BlockSpec((tb, wide), lambda h, t: (t, h)),
                   pl.BlockSpec((HGRN_HEADS_PER_STEP, nch, HEAD_A, HEAD_A), lambda h, t: (h, t, 0, 0))],
        out_shape=[jax.ShapeDtypeStruct((s, A_HEADS * HEAD_A), F32),
                   jax.ShapeDtypeStruct((A_HEADS, s // CHUNK, HEAD_A, HEAD_A), F32)],
        scratch_shapes=[pltpu.VMEM((HGRN_HEADS_PER_STEP, HEAD_A, HEAD_A), F32), per_head, per_head, per_head],
        compiler_params=_params(),
    )(proj, proj, proj, lb_row)


def _hgrn_bwd(proj, lb_row, states, do_a, *, name):
    s = proj.shape[0]
    tb = _row_block(s, 512)
    nch = tb // CHUNK
    nblk = s // tb

    wide = HGRN_HEADS_PER_STEP * HEAD_A

    def body(aq_all, af_all, ai_all, lb_all, st_all, do_all, daq_all, daf_all, dai_all, dlb_all,
             ds_all, q_all, b_all, dq_all, dos_all):
        @pl.when(pl.program_id(1) == 0)
        def _():
            ds_all[...] = jnp.zeros_like(ds_all)
            dlb_all[...] = jnp.zeros_like(dlb_all)

        tril = _tri(CHUNK, lambda r, c: r >= c)
        triu = _tri(CHUNK, lambda r, c: c >= r)
        ones = jnp.ones((LANES, LANES), BF16)
        col = lax.broadcasted_iota(jnp.int32, (SUB, CHUNK), 1)
        row = lax.broadcasted_iota(jnp.int32, (CHUNK, LANES), 0)
        nsub = CHUNK // SUB

        def head_chunk(hh, ci):
            lanes = slice(HEAD_A * hh, HEAD_A * (hh + 1))
            aq_ref, af_ref, ai_ref, do_ref = aq_all.at[:, lanes], af_all.at[:, lanes], ai_all.at[:, lanes], do_all.at[:, lanes]
            daq_ref, daf_ref, dai_ref, dlb_ref = daq_all.at[:, lanes], daf_all.at[:, lanes], dai_all.at[:, lanes], dlb_all.at[:, lanes]
            st_ref, ds_scr, q_scr, b_scr = st_all.at[hh], ds_all.at[hh], q_all.at[hh], b_all.at[hh]
            dq_scr, do_scr = dq_all.at[hh], dos_all.at[hh]
            lbv = lb_all[:, lanes]
            r0 = pl.multiple_of(ci * CHUNK, CHUNK)
            aq = aq_ref[pl.ds(r0, CHUNK), :]
            q, sq, sg, k, g = _hgrn_pre(aq, af_ref[pl.ds(r0, CHUNK), :], lbv)
            v = ai_ref[pl.ds(r0, CHUNK), :]
            do = do_ref[pl.ds(r0, CHUNK), :]
            st0 = st_ref[ci]
            dst1 = ds_scr[...]
            b = _dot_exact_lhs(tril, g, 3)
            b_scr[...] = b
            q_scr[...] = q
            do_scr[...] = do
            bl = b_scr[CHUNK - 1:CHUNK, :]
            eb = jnp.exp(b)
            ebl = jnp.exp(bl)
            ekd = jnp.exp(bl - b)
            kd = k * ekd
            a, facs = _hgrn_intra(q, k, b, b_scr)

            da = _dot(do, v, NT)
            dv = _dot(a, do, TN) + _dot(kd, dst1, NT)
            dq = _dot(do, st0, NN) * eb
            dk_inter = _dot(v, dst1, NN) * ekd
            dk = dk_inter
            dq_rows = [jnp.zeros((SUB, LANES), F32)]
            for i in range(1, nsub):
                fq, fk, qs, ks = facs[i]
                da_i = jnp.where(col < SUB * i, da[SUB * i:SUB * (i + 1)], 0.0)
                dq_rows.append(_dot(da_i, ks, NN) * fq)
                dk = dk + _dot(da_i, qs, TN) * fk
            dq = dq + jnp.concatenate(dq_rows, axis=0)

            dv_rows, dk_rows = [], []
            for i in range(nsub):
                es, wst = _hgrn_diag_weights(i, k, b, q_scr, b_scr)
                ki = k[SUB * i:SUB * (i + 1)]
                vi = v[SUB * i:SUB * (i + 1)]
                rep = _dot(wst, ones, NN)
                mst = jnp.concatenate([do_scr[SUB * i + tau:SUB * i + tau + 1, :] * vi for tau in range(SUB)], axis=0)
                drep = _dot(mst, ones, NN)
                dvi = jnp.zeros((SUB, LANES), F32)
                dki = jnp.zeros((SUB, LANES), F32)
                for tau in range(SUB):
                    t = SUB * i + tau
                    gt = drep[SUB * tau:SUB * (tau + 1)] * es[tau]
                    dvi = dvi + rep[SUB * tau:SUB * (tau + 1)] * do_scr[t:t + 1, :]
                    dki = dki + gt * q_scr[t:t + 1, :]
                    dq_scr[t:t + 1, :] = jnp.sum(gt * ki, axis=0, keepdims=True)
                dv_rows.append(dvi)
                dk_rows.append(dki)
            dv = dv + jnp.concatenate(dv_rows, axis=0)
            dk = dk + jnp.concatenate(dk_rows, axis=0)
            dq = dq + dq_scr[...]

            ds_scr[...] = dst1 * ebl + _dot(do, q * eb, TN)
            last = jnp.sum(k * dk_inter, axis=0, keepdims=True) + ebl * jnp.sum(st0 * dst1, axis=0, keepdims=True)
            db = q * dq - k * dk + jnp.where(row == CHUNK - 1, last, 0.0)
            dg = _dot_exact_lhs(triu, db, 3)

            dkt = dk - dg / (1.0 - k)
            daq_ref[pl.ds(r0, CHUNK), :] = dq * (sq * (1.0 + aq * (1.0 - sq)))
            daf_ref[pl.ds(r0, CHUNK), :] = dkt * (1.0 - lbv) * (-(sg * (1.0 - sg)))
            dai_ref[pl.ds(r0, CHUNK), :] = dv
            dlb_ref[...] += jnp.sum(dkt * (-sg), axis=0, keepdims=True)

        def chunk(cj, carry):
            for hh in range(HGRN_HEADS_PER_STEP):
                head_chunk(hh, nch - 1 - cj)
            return carry

        lax.fori_loop(0, nch, chunk, 0)

    def col_in(block0):
        return pl.BlockSpec((tb, wide), lambda h, t: (nblk - 1 - t, block0 + h))

    steps = A_HEADS // HGRN_HEADS_PER_STEP
    head_col = pl.BlockSpec((tb, wide), lambda h, t: (nblk - 1 - t, h))
    vec = pl.BlockSpec((1, wide), lambda h, t: (0, h))
    big = jax.ShapeDtypeStruct((s, A_HEADS * HEAD_A), F32)
    per_head = pltpu.VMEM((HGRN_HEADS_PER_STEP, CHUNK, HEAD_A), F32)
    return pl.pallas_call(
        body, name=name, grid=(steps, nblk),
        in_specs=[col_in(0), col_in(steps), col_in(2 * steps), vec,
                  pl.BlockSpec((HGRN_HEADS_PER_STEP, nch, HEAD_A, HEAD_A), lambda h, t: (h, nblk - 1 - t, 0, 0)), head_col],
        out_specs=[head_col, head_col, head_col, vec],
        out_shape=[big, big, big, jax.ShapeDtypeStruct((1, A_HEADS * HEAD_A), F32)],
        scratch_shapes=[pltpu.VMEM((HGRN_HEADS_PER_STEP, HEAD_A, HEAD_A), F32), per_head, per_head, per_head, per_head],
        compiler_params=_params(),
    )(proj, proj, proj, lb_row, states, do_a)


_BQ_BLOCK = 16
_BK_BLOCK = 20
_BV_BLOCK = 24
SB_SCALE = HEAD_B ** -0.5
SB_TQ = 512
SB_TK = 256


def _sb_blocks(s):
    tq = min(SB_TQ, s)
    tk = min(SB_TK, s)
    assert s % tq == 0 and tq % tk == 0 and s // tk <= LANES
    return tq, tk, (2 if (tq // tk) % 2 == 0 else 1)


def _sb2_weights(qh, kb, carry, incl, mask):
    z = _dot(qh, kb, NT)
    sp = jnp.maximum(z, 0.0) + jnp.log(1.0 + jnp.exp(-jnp.abs(z)))
    if mask is not None:
        sp = jnp.where(mask, sp, 0.0)
    suffix = _dot(sp, incl, NN)
    w = jnp.exp(z - suffix + carry)
    if mask is not None:
        w = jnp.where(mask, w, 0.0)
    return z, sp, w, suffix[:, 0:1]


def _sb_mask(qi, j, tq, tk, row0):
    row = qi * tq + row0 + lax.broadcasted_iota(jnp.int32, (tq - row0, tk), 0)
    col = j * tk + lax.broadcasted_iota(jnp.int32, (tq - row0, tk), 1)
    return col < row


def _below(x, row0, new_tail):
    return new_tail if row0 == 0 else jnp.concatenate([x[:row0], new_tail], axis=0)


def _ride_along(ex, refs, n_in, n_out, first, last):
    if ex is None:
        return refs, lambda: None
    own = refs[:n_in] + refs[n_in + ex.n:n_in + ex.n + n_out]
    xin = refs[n_in:n_in + ex.n]
    xout = refs[n_in + ex.n + n_out:n_in + 2 * ex.n + n_out]
    rest = refs[n_in + 2 * ex.n + n_out:]
    sems, scratch = rest[len(rest) - 3:], rest[:len(rest) - 3]

    @pl.when(first)
    def _():
        ex.start(xin, xout, sems)

    def finish():
        @pl.when(last)
        def _():
            ex.finish(xin, xout, sems)

    return own + scratch, finish


def _sb2_fwd(proj, *, name, exchange=None):
    s = proj.shape[0]
    tq, tk, unroll = _sb_blocks(s)
    nq, ratio = s // tq, tq // tk
    ex = None if exchange is None else _Exchange(*exchange)

    def body(*refs):
        first = jnp.logical_and(pl.program_id(0) == 0, pl.program_id(1) == 0)
        last = jnp.logical_and(pl.program_id(0) == B_PAIRS - 1, pl.program_id(1) == nq - 1)
        (q_ref, k_ref, v_ref, o_ref, c_ref), finish = _ride_along(ex, refs, 3, 2, first, last)
        qi = pl.program_id(1)
        lane = lax.broadcasted_iota(jnp.int32, (1, LANES), 1)
        low = lane < HEAD_B

        def heads(x):
            return jnp.where(low, x, 0.0).astype(BF16), jnp.where(low, 0.0, x).astype(BF16)

        qh = heads(q_ref[...] * SB_SCALE)
        incl = _tri(tk, lambda r, c: r >= c)
        o_ref[...] = jnp.zeros_like(o_ref)
        c_ref[...] = jnp.zeros_like(c_ref)

        def tile(j, c, masked, row0=0):
            s0 = pl.multiple_of(j * tk, tk)
            kb = k_ref[pl.ds(s0, tk), :].astype(BF16)
            vh = heads(v_ref[pl.ds(s0, tk), :])
            mask = _sb_mask(qi, j, tq, tk, row0) if masked else None
            c_new = []
            acc = jnp.zeros((tq - row0, LANES), F32)
            for h in range(2):
                _, _, w, rowsum = _sb2_weights(qh[h][row0:], kb, c[h][row0:], incl, mask)
                c_ref[h] = jnp.where(lane == j, c[h], c_ref[h])
                c_new.append(_below(c[h], row0, c[h][row0:] - rowsum))
                acc = acc + _dot(w, vh[h], NN)
            o_ref[row0:, :] += acc
            return tuple(c_new)

        zc = jnp.zeros((tq, 1), F32)
        c = (zc, zc)
        for r in reversed(range(ratio)):
            c = tile(qi * ratio + r, c, True, r * tk)
        def several(jj, c):
            for r in range(unroll):
                c = tile(qi * ratio - 1 - (jj * unroll + r), c, False)
            return c

        lax.fori_loop(0, qi * (ratio // unroll), several, c)
        finish()

    def whole(block0):
        return pl.BlockSpec((s, LANES), lambda p, i: (0, block0 + p))

    in_specs = [pl.BlockSpec((tq, LANES), lambda p, i: (i, _BQ_BLOCK + p)), whole(_BK_BLOCK), whole(_BV_BLOCK)]
    out_specs = [pl.BlockSpec((tq, LANES), lambda p, i: (i, p)), pl.BlockSpec((None, 2, tq, LANES), lambda p, i: (p, 0, i, 0))]
    out_shape = [jax.ShapeDtypeStruct((s, B_PAIRS * LANES), F32), jax.ShapeDtypeStruct((B_PAIRS, 2, s, LANES), F32)]
    args, scratch = [proj, proj, proj], []
    if ex is not None:
        in_specs, out_specs, out_shape = in_specs + ex.in_specs, out_specs + ex.out_specs, out_shape + ex.out_shape
        args, scratch = args + list(exchange[0]), ex.scratch_shapes
    out = pl.pallas_call(
        body, name=name, grid=(B_PAIRS, nq), in_specs=in_specs, out_specs=out_specs, out_shape=out_shape,
        scratch_shapes=scratch, compiler_params=_params(),
    )(*args)
    return out[0], out[1], out[2:]


def _sb2_bwd(proj, carries, do_b, *, name, exchange=None):
    s = proj.shape[0]
    tq, tk, unroll = _sb_blocks(s)
    nq, nk, ratio = s // tq, s // tk, tq // tk
    ex = None if exchange is None else _Exchange(*exchange)

    def body(*refs):
        first = jnp.logical_and(pl.program_id(0) == 0, pl.program_id(1) == 0)
        last = jnp.logical_and(pl.program_id(0) == B_PAIRS - 1, pl.program_id(1) == nq - 1)
        own, finish = _ride_along(ex, refs, 5, 3, first, last)
        q_ref, k_ref, v_ref, c_ref, do_ref, dq_ref, dk_ref, dv_ref, dkt_scr, dvt_scr = own
        qi = pl.program_id(1)

        @pl.when(qi == 0)
        def _():
            dkt_scr[...] = jnp.zeros_like(dkt_scr)
            dvt_scr[...] = jnp.zeros_like(dvt_scr)

        lane = lax.broadcasted_iota(jnp.int32, (1, LANES), 1)
        low = lane < HEAD_B
        low_row = lax.broadcasted_iota(jnp.int32, (LANES, 1), 0) < HEAD_B

        def heads(x):
            return jnp.where(low, x, 0.0).astype(BF16), jnp.where(low, 0.0, x).astype(BF16)

        def heads_t(x):
            xt = x.T
            return jnp.where(low_row, xt, 0.0).astype(BF16), jnp.where(low_row, 0.0, xt).astype(BF16)

        q = q_ref[...] * SB_SCALE
        do = do_ref[...]
        qh, qth = heads(q), heads_t(q)
        doh, doth = heads(do), heads_t(do)
        cst = (c_ref[0], c_ref[1])
        incl = _tri(tk, lambda r, c: r >= c)
        pre = _tri(tk, lambda r, c: r <= c)

        def tile(j, carry, masked, row0=0):
            ec, dq = carry
            s0 = pl.multiple_of(j * tk, tk)
            kf = k_ref[pl.ds(s0, tk), :]
            kb = kf.astype(BF16)
            kh = heads(kf)
            vb = v_ref[pl.ds(s0, tk), :].astype(BF16)
            mask = _sb_mask(qi, j, tq, tk, row0) if masked else None
            ec_new = []
            dkt = jnp.zeros((LANES, tk), F32)
            dvt = jnp.zeros((LANES, tk), F32)
            dq_add = jnp.zeros((tq - row0, LANES), F32)
            for h in range(2):
                cin = jnp.sum(jnp.where(lane == j, cst[h][row0:], 0.0), axis=1, keepdims=True)
                z, sp, w, _ = _sb2_weights(qh[h][row0:], kb, cin, incl, mask)
                e = w * _dot(doh[h][row0:], vb, NT)
                prefix = _dot(e, pre, NN)
                dz = e - jnp.exp(z - sp) * (ec[h][row0:] + prefix)
                if mask is not None:
                    dz = jnp.where(mask, dz, 0.0)
                ec_new.append(_below(ec[h], row0, ec[h][row0:] + prefix[:, tk - 1:tk]))
                dzb = dz.astype(BF16)
                wb = w.astype(BF16)
                dq_add = dq_add + _dot(dzb, kh[h], NN)
                dkt = dkt + _dot(qth[h][:, row0:], dzb, NN)
                dvt = dvt + _dot(doth[h][:, row0:], wb, NN)
            dkt_scr[j] += dkt
            dvt_scr[j] += dvt
            return tuple(ec_new), _below(dq, row0, dq[row0:] + dq_add)

        zc = jnp.zeros((tq, 1), F32)
        def several(jj, carry):
            for r in range(unroll):
                carry = tile(jj * unroll + r, carry, False)
            return carry

        carry = lax.fori_loop(0, qi * (ratio // unroll), several, ((zc, zc), jnp.zeros((tq, LANES), F32)))
        for r in range(ratio):
            carry = tile(qi * ratio + r, carry, True, r * tk)
        dq_ref[...] = carry[1] * SB_SCALE

        @pl.when(qi == nq - 1)
        def _():
            for j in range(nk):
                dk_ref[tk * j:tk * (j + 1), :] = dkt_scr[j].T
                dv_ref[tk * j:tk * (j + 1), :] = dvt_scr[j].T

        finish()

    def whole_in(block0):
        return pl.BlockSpec((s, LANES), lambda p, i: (0, block0 + p))

    blk = pl.BlockSpec((tq, LANES), lambda p, i: (i, p))
    whole_out = pl.BlockSpec((s, LANES), lambda p, i: (0, p))
    big = jax.ShapeDtypeStruct((s, B_PAIRS * LANES), F32)
    in_specs = [pl.BlockSpec((tq, LANES), lambda p, i: (i, _BQ_BLOCK + p)), whole_in(_BK_BLOCK), whole_in(_BV_BLOCK),
                pl.BlockSpec((None, 2, tq, LANES), lambda p, i: (p, 0, i, 0)), blk]
    out_specs, out_shape = [blk, whole_out, whole_out], [big, big, big]
    args = [proj, proj, proj, carries, do_b]
    scratch = [pltpu.VMEM((nk, LANES, tk), F32), pltpu.VMEM((nk, LANES, tk), F32)]
    if ex is not None:
        in_specs, out_specs, out_shape = in_specs + ex.in_specs, out_specs + ex.out_specs, out_shape + ex.out_shape
        args, scratch = args + list(exchange[0]), scratch + ex.scratch_shapes
    out = pl.pallas_call(
        body, name=name, grid=(B_PAIRS, nq), in_specs=in_specs, out_specs=out_specs, out_shape=out_shape,
        scratch_shapes=scratch, compiler_params=_params(),
    )(*args)
    return out[0], out[1], out[2], out[3:]


def _my_id():
    return 4 * lax.axis_index("x") + 2 * lax.axis_index("y") + lax.axis_index("c")


def _mesh_pos(d):
    return (d // 4, (d // 2) % 2, d % 2)


class _Exchange:
    def __init__(self, arrays, scatter):
        self.n = len(arrays)
        self.scatter = scatter
        self.in_specs = [pl.BlockSpec(memory_space=pl.ANY)] * self.n
        self.out_specs = [pl.BlockSpec(memory_space=pl.ANY)] * self.n
        self.out_shape = [
            jax.ShapeDtypeStruct(((N_DEV,) + a.shape[1:]) if scatter else ((N_DEV,) + a.shape), a.dtype) for a in arrays]
        self.scratch_shapes = [pltpu.SemaphoreType.DMA((N_DEV - 1, self.n)), pltpu.SemaphoreType.DMA((N_DEV - 1, self.n)),
                               pltpu.SemaphoreType.DMA((self.n,))]

    def _copies(self, ins, outs, sems):
        send_sems, recv_sems, local_sems = sems
        me = _my_id()

        def src(i, to):
            return ins[i].at[to] if self.scatter else ins[i]

        local = [pltpu.make_async_copy(src(i, me), outs[i].at[me], local_sems.at[i]) for i in range(self.n)]
        sends, recvs = [], []
        for step in range(1, N_DEV):
            to = (me + step) % N_DEV
            frm = (me + N_DEV - step) % N_DEV
            for i in range(self.n):
                sends.append(pltpu.make_async_remote_copy(
                    src_ref=src(i, to), dst_ref=outs[i].at[me],
                    send_sem=send_sems.at[step - 1, i], recv_sem=recv_sems.at[step - 1, i],
                    device_id=_mesh_pos(to), device_id_type=pl.DeviceIdType.MESH))
                recvs.append(pltpu.make_async_remote_copy(
                    src_ref=src(i, frm), dst_ref=outs[i].at[frm],
                    send_sem=send_sems.at[step - 1, i], recv_sem=recv_sems.at[step - 1, i],
                    device_id=_mesh_pos(frm), device_id_type=pl.DeviceIdType.MESH))
        return local, sends, recvs

    def start(self, ins, outs, sems):
        local, sends, _ = self._copies(ins, outs, sems)
        for cp in local + sends:
            cp.start()

    def finish(self, ins, outs, sems):
        local, sends, recvs = self._copies(ins, outs, sems)
        for cp in recvs:
            cp.wait_recv()
        for cp in sends:
            cp.wait_send()
        for cp in local:
            cp.wait()


def _exchange(arrays, scatter, *, name):
    ex = _Exchange(arrays, scatter)

    def body(*refs):
        ins, outs, sems = refs[:ex.n], refs[ex.n:2 * ex.n], refs[2 * ex.n:]
        ex.start(ins, outs, sems)
        ex.finish(ins, outs, sems)

    return pl.pallas_call(
        body, name=name, in_specs=ex.in_specs, out_specs=ex.out_specs, out_shape=ex.out_shape,
        scratch_shapes=ex.scratch_shapes,
    )(*arrays)


def _adamw_math(w, m, v, g):
    m2 = ADAM_B1 * m + (1.0 - ADAM_B1) * g
    v2 = ADAM_B2 * v + (1.0 - ADAM_B2) * (g * g)
    m_hat = m2 / (1.0 - ADAM_B1 ** ADAM_STEP)
    v_hat = v2 / (1.0 - ADAM_B2 ** ADAM_STEP)
    delta = -ADAM_LR * (m_hat / (jnp.sqrt(v_hat) + ADAM_EPS) + ADAM_WD * w)
    return delta, m2, v2


def _slot_sum(ref):
    g = ref[0].astype(F32)
    for d in range(1, N_DEV):
        g = g + ref[d].astype(F32)
    return g


def _adamw(w, m, v, slots, *, name):
    r, c = w.shape
    tr = _row_block(r, 256)

    def body(w_ref, m_ref, v_ref, s_ref, g_ref, d_ref, m2_ref, v2_ref):
        g = _slot_sum(s_ref)
        delta, m2, v2 = _adamw_math(w_ref[...], m_ref[...], v_ref[...], g)
        g_ref[...] = g
        d_ref[...] = delta
        m2_ref[...] = m2
        v2_ref[...] = v2

    row = pl.BlockSpec((tr, c), lambda i: (i, 0))
    shp = jax.ShapeDtypeStruct((r, c), F32)
    return pl.pallas_call(
        body, name=name, grid=(r // tr,),
        in_specs=[row, row, row, pl.BlockSpec((N_DEV, tr, c), lambda i: (0, i, 0))],
        out_specs=[row, row, row, row], out_shape=[shp, shp, shp, shp], compiler_params=_params(),
    )(w, m, v, slots)


def _layer_softmax(l_ref):
    l0 = l_ref[0:1, :]
    l1 = l_ref[1:2, :]
    mx = jnp.maximum(l0, l1)
    e0 = jnp.exp(l0 - mx)
    e1 = jnp.exp(l1 - mx)
    return e0 / (e0 + e1), e1 / (e0 + e1)


def _adamw_lb(w, m, v, slots, *, name):
    def body(w_ref, m_ref, v_ref, s_ref, g_ref, d_ref, m2_ref, v2_ref):
        sm0, sm1 = _layer_softmax(w_ref)
        dl1 = _slot_sum(s_ref) * (sm0 * sm1)
        for row, g in ((0, -dl1), (1, dl1)):
            sl = slice(row, row + 1)
            delta, m2, v2 = _adamw_math(w_ref[sl, :], m_ref[sl, :], v_ref[sl, :], g)
            g_ref[sl, :] = g
            d_ref[sl, :] = delta
            m2_ref[sl, :] = m2
            v2_ref[sl, :] = v2

    shp = jax.ShapeDtypeStruct(w.shape, F32)
    return pl.pallas_call(body, name=name, out_shape=[shp, shp, shp, shp])(w, m, v, slots)


def _lower_bound_1(lb_logits, *, name):
    def body(l_ref, o_ref):
        sm0, sm1 = _layer_softmax(l_ref)
        o_ref[...] = (sm0 + sm1) - sm0

    return pl.pallas_call(body, name=name, out_shape=jax.ShapeDtypeStruct((1, lb_logits.shape[1]), F32))(lb_logits)


_SMALL = ("norm_mix", "a_out_norm", "b_out_norm", "ple_gate_norm", "ple_post_norm")


def _pack_rows(vectors):
    offs, rows = [], 0
    for vec in vectors:
        offs.append(rows)
        rows += vec.shape[0] // LANES
    flat = jnp.concatenate([vec.reshape(-1, LANES) for vec in vectors], axis=0)
    pad = (-rows) % 8
    if pad:
        flat = jnp.concatenate([flat, jnp.zeros((pad, LANES), F32)], axis=0)
    return flat, offs


def kernel(x, p, norm_mix, w_in, a_out_norm, b_out_norm, w_out, lb_logits, ple_gate_norm, w_ple_gate, w_ple_proj, ple_post_norm, final_norm, loss_target, m_norm_mix, m_w_in, m_a_out_norm, m_b_out_norm, m_w_out, m_lb_logits, m_ple_gate_norm, m_w_ple_gate, m_w_ple_proj, m_ple_post_norm, m_final_norm, v_norm_mix, v_w_in, v_a_out_norm, v_b_out_norm, v_w_out, v_lb_logits, v_ple_gate_norm, v_w_ple_gate, v_w_ple_proj, v_ple_post_norm, v_final_norm):
    depth = w_in.shape[0]
    assert depth == 2
    s, d = x.shape[1], x.shape[2]
    h = x.reshape(s, d)
    target = loss_target.reshape(s, d)

    (win0,) = _exchange([w_in[0].astype(BF16)], scatter=False, name="gather_w_in0")
    later = [w_in[1].astype(BF16), w_out.astype(BF16), w_ple_gate.astype(BF16), w_ple_proj.astype(BF16)]
    lb1 = _lower_bound_1(lb_logits, name="lower_bound")
    lbs = [jnp.zeros_like(lb1), lb1]

    saved = []
    for l in range(depth):
        win = win0 if l == 0 else win1
        u = _norm_fwd(h, norm_mix[l:l + 1], name="norm_mix")
        proj = _mm_nn(u, win, name="mm_in")
        o_a, states = _hgrn_fwd(proj, lbs[l], name="hgrn_fwd")
        if l == 0:
            o_b, carries, (win1, wout_g, wpg_g, wpp_g) = _sb2_fwd(proj, name="sb_fwd_gather", exchange=(later, False))
        else:
            o_b, carries, _ = _sb2_fwd(proj, name="sb_fwd")
        wout = wout_g[:, l].reshape(1, d, d)
        wpg = wpg_g[:, l].reshape(1, d, d)
        wpp = jnp.transpose(wpp_g[:, l], (1, 0, 2)).reshape(1, p.shape[-1], d)
        y = _mix_fwd(o_a, o_b, proj, a_out_norm[l:l + 1], b_out_norm[l:l + 1], name="mix_fwd")
        h1 = _mm_nn(y, wout, add=h, name="mm_out")
        r1 = _norm_fwd(h1, ple_gate_norm[l:l + 1], name="norm_gate")
        gp = _mm_nn(r1, wpg, name="mm_gate")
        pl_in = p[l].reshape(s, p.shape[-1])
        pp = _mm_nn(pl_in, wpp, name="mm_ple")
        h2 = _ple_fwd(h1, gp, pp, ple_post_norm[l:l + 1], name="ple_fwd")
        saved.append((h, u, proj, o_a, states, o_b, carries, y, h1, r1, gp, pp, pl_in, win, wout, wpg, wpp))
        h = h2

    loss_part, dh, dg_final = _loss_head(h, final_norm.reshape(1, d), target, name="loss_head")

    big = ("w_in", "w_out", "w_ple_gate", "w_ple_proj")
    grads = [None] * depth
    for l in reversed(range(depth)):
        h0, u, proj, o_a, states, o_b, carries, y, h1, r1, gp, pp, pl_in, win, wout, wpg, wpp = saved[l]
        dgp, dpp, dg_post = _ple_bwd(dh, gp, pp, ple_post_norm[l:l + 1], name="ple_bwd")
        dwpp = _mm_tn(pl_in, [dpp], name="mm_ple_dw")
        dwpp = jnp.transpose(dwpp.reshape(p.shape[-1], N_DEV, d // N_DEV), (1, 0, 2))
        dwpg = _mm_tn(r1, [dgp], name="mm_gate_dw").reshape(N_DEV, d // N_DEV, d)
        dr1, _ = _mm_nt([dgp], wpg, name="mm_gate_dx")
        dh1, dg_pg = _norm_bwd(h1, ple_gate_norm[l:l + 1], dr1, dh, name="norm_gate_bwd")
        dwout = _mm_tn(y, [dh1], name="mm_out_dw").reshape(N_DEV, d // N_DEV, d)
        dy, _ = _mm_nt([dh1], wout, name="mm_out_dx")
        do_a, do_b, dag, dbg, dg_a, dg_b = _mix_bwd(dy, o_a, o_b, proj, a_out_norm[l:l + 1], b_out_norm[l:l + 1], name="mix_bwd")
        if l == 0:
            ready = [grads[1][n] for n in big] + [dwout, dwpg, dwpp]
            dbq, dbk, dbv, early_slots = _sb2_bwd(proj, carries, do_b, name="sb_bwd_scatter", exchange=(ready, True))
        else:
            dbq, dbk, dbv, _ = _sb2_bwd(proj, carries, do_b, name="sb_bwd")
        daq, daf, dai, dlb = _hgrn_bwd(proj, lbs[l], states, do_a, name="hgrn_bwd")
        dproj = [daq, daf, dai, dag, dbq, dbk, dbv, dbg]
        dwin = _mm_tn(u, dproj, name="mm_in_dw")
        if l == 0:
            du, (win0_slots,) = _mm_nt(dproj, win, name="mm_in_dx_scatter", exchange=([dwin], True))
        else:
            du, _ = _mm_nt(dproj, win, name="mm_in_dx")
        dh, dg_mix = _norm_bwd(h0, norm_mix[l:l + 1], du, dh1, name="norm_mix_bwd")
        grads[l] = dict(w_in=dwin, w_out=dwout, w_ple_gate=dwpg, w_ple_proj=dwpp, norm_mix=dg_mix, a_out_norm=dg_a,
                        b_out_norm=dg_b, ple_gate_norm=dg_pg, ple_post_norm=dg_post, lb=dlb)

    grad_x = dh.reshape(x.shape)
    loss = lax.psum(loss_part[0, 0], ("x", "y", "c"))

    slots = {("w_in", 0): win0_slots}
    for k, key in enumerate([(n, 1) for n in big] + [(n, 0) for n in big[1:]]):
        slots[key] = early_slots[k]
    results = {}
    for n, w, m, v in zip(big, (w_in, w_out, w_ple_gate, w_ple_proj), (m_w_in, m_w_out, m_w_ple_gate, m_w_ple_proj),
                          (v_w_in, v_w_out, v_w_ple_gate, v_w_ple_proj)):
        per_layer = [_adamw(w[l], m[l], v[l], slots[(n, l)], name="adamw_" + n) for l in range(depth)]
        results[n] = [jnp.stack([per_layer[l][k] for l in range(depth)]) for k in range(4)]

    small_w = dict(norm_mix=norm_mix, a_out_norm=a_out_norm, b_out_norm=b_out_norm, ple_gate_norm=ple_gate_norm,
                   ple_post_norm=ple_post_norm)
    small_m = dict(norm_mix=m_norm_mix, a_out_norm=m_a_out_norm, b_out_norm=m_b_out_norm, ple_gate_norm=m_ple_gate_norm,
                   ple_post_norm=m_ple_post_norm)
    small_v = dict(norm_mix=v_norm_mix, a_out_norm=v_a_out_norm, b_out_norm=v_b_out_norm, ple_gate_norm=v_ple_gate_norm,
                   ple_post_norm=v_ple_post_norm)
    g_vecs = [jnp.concatenate([grads[l][n].reshape(-1) for l in range(depth)]) for n in _SMALL] + [dg_final.reshape(-1)]
    g_pack, offs = _pack_rows(g_vecs)
    w_pack, _ = _pack_rows([small_w[n].reshape(-1) for n in _SMALL] + [final_norm])
    m_pack, _ = _pack_rows([small_m[n].reshape(-1) for n in _SMALL] + [m_final_norm])
    v_pack, _ = _pack_rows([small_v[n].reshape(-1) for n in _SMALL] + [v_final_norm])
    small_slots, lb_slots = _exchange([g_pack, grads[1]["lb"]], scatter=False, name="gather_small_grads")
    packed = _adamw(w_pack, m_pack, v_pack, small_slots, name="adamw_small")
    results["lb_logits"] = _adamw_lb(lb_logits, m_lb_logits, v_lb_logits, lb_slots, name="adamw_lb")
    shapes = [small_w[n].shape for n in _SMALL] + [final_norm.shape]
    for n, off, shp in zip(_SMALL + ("final_norm",), offs, shapes):
        rows = 1
        for dim in shp:
            rows *= dim
        rows //= LANES
        results[n] = [o[off:off + rows].reshape(shp) for o in packed]

    order = ("norm_mix", "w_in", "a_out_norm", "b_out_norm", "w_out", "lb_logits", "ple_gate_norm", "w_ple_gate",
             "w_ple_proj", "ple_post_norm", "final_norm")
    out = [loss, grad_x]
    for k in range(4):
        out += [results[n][k] for n in order]
    return tuple(out)
```

```python
import jax
import jax.numpy as jnp
from jax import lax
from jax.experimental import pallas as pl
from jax.experimental.pallas import tpu as pltpu

F32 = jnp.float32
BF16 = jnp.bfloat16

N_DEV = 8
EPS = 1e-6
A_HEADS = 4
HEAD_A = 128
B_PAIRS = 4
HEAD_B = 64
CHUNK = 64
SUB = 16
HGRN_HEADS_PER_STEP = 2
LANES = 128
NEG_BIG = -1e30

ADAM_LR = 0.001
ADAM_B1 = 0.9
ADAM_B2 = 0.999
ADAM_EPS = 1e-08
ADAM_WD = 0.01
ADAM_STEP = 10

VMEM_LIMIT = 56 * 1024 * 1024

NN = ((1,), (0,))
NT = ((1,), (1,))
TN = ((0,), (0,))


def _dot(a, b, dims):
    return lax.dot_general(a.astype(BF16), b.astype(BF16), (dims, ((), ())), preferred_element_type=F32)


def _split_bf16(x, parts):
    out = []
    r = x
    for i in range(parts):
        p = r.astype(BF16)
        out.append(p)
        if i + 1 < parts:
            r = r - p.astype(F32)
    return out


def _dot_exact_lhs(t, x, parts):
    acc = None
    for p in _split_bf16(x, parts):
        d = lax.dot_general(t, p, (NN, ((), ())), preferred_element_type=F32)
        acc = d if acc is None else acc + d
    return acc


def _params(**kw):
    return pltpu.CompilerParams(vmem_limit_bytes=VMEM_LIMIT, **kw)


def _row_block(m, want):
    t = min(want, m)
    assert m % t == 0
    return t


def _mm_nn(a, w, add=None, *, name):
    m, k = a.shape
    nb, _, nc = w.shape
    tm = _row_block(m, 512)

    def body(*refs):
        rows = pl.ds(pl.multiple_of(pl.program_id(1) * tm, tm), tm)
        if add is None:
            a_ref, w_ref, o_ref = refs
            o_ref[...] = _dot(a_ref[rows, :], w_ref[...], NN)
        else:
            a_ref, w_ref, add_ref, o_ref = refs
            o_ref[...] = _dot(a_ref[rows, :], w_ref[...], NN) + add_ref[...]

    in_specs = [pl.BlockSpec((m, k), lambda j, i: (0, 0)), pl.BlockSpec((None, k, nc), lambda j, i: (j, 0, 0))]
    args = [a, w]
    if add is not None:
        in_specs.append(pl.BlockSpec((tm, nc), lambda j, i: (i, j)))
        args.append(add)
    return pl.pallas_call(
        body, name=name, grid=(nb, m // tm), in_specs=in_specs,
        out_specs=pl.BlockSpec((tm, nc), lambda j, i: (i, j)),
        out_shape=jax.ShapeDtypeStruct((m, nb * nc), F32), compiler_params=_params(),
    )(*args)


def _mm_nt(pieces, w, *, name, exchange=None):
    m = pieces[0].shape[0]
    nb, k, nc = w.shape
    assert len(pieces) == nb
    tm = _row_block(m, 512)
    steps = m // tm
    ex = None if exchange is None else _Exchange(*exchange)

    def body(*refs):
        first = jnp.logical_and(pl.program_id(0) == 0, pl.program_id(1) == 0)
        last = jnp.logical_and(pl.program_id(0) == steps - 1, pl.program_id(1) == nb - 1)
        own, finish = _ride_along(ex, refs, nb + 1, 1, first, last)
        g_refs, w_ref, o_ref = own[:nb], own[nb], own[nb + 1]
        j = pl.program_id(1)
        for p in range(nb):
            @pl.when(j == p)
            def _(p=p):
                r = _dot(g_refs[p][...], w_ref[p], NT)
                if p == 0:
                    o_ref[...] = r
                else:
                    o_ref[...] += r

        finish()

    in_specs = [pl.BlockSpec((tm, nc), lambda i, j: (i, 0)) for _ in range(nb)]
    in_specs.append(pl.BlockSpec((nb, k, nc), lambda i, j: (0, 0, 0)))
    out_specs, out_shape = [pl.BlockSpec((tm, k), lambda i, j: (i, 0))], [jax.ShapeDtypeStruct((m, k), F32)]
    args, scratch = list(pieces) + [w], []
    if ex is not None:
        in_specs, out_specs, out_shape = in_specs + ex.in_specs, out_specs + ex.out_specs, out_shape + ex.out_shape
        args, scratch = args + list(exchange[0]), ex.scratch_shapes
    out = pl.pallas_call(
        body, name=name, grid=(steps, nb), in_specs=in_specs, out_specs=out_specs, out_shape=out_shape,
        scratch_shapes=scratch, compiler_params=_params(),
    )(*args)
    return out[0], out[1:]


def _mm_tn(a, pieces, *, name):
    m, k = a.shape
    nb, nc = len(pieces), pieces[0].shape[1]
    tm = _row_block(m, 512)
    steps = m // tm

    def body(*refs):
        a_ref, g_refs, o_ref, acc_ref = refs[0], refs[1:1 + nb], refs[1 + nb], refs[2 + nb]
        j, i = pl.program_id(0), pl.program_id(1)
        rows = pl.ds(pl.multiple_of(i * tm, tm), tm)
        for p in range(nb):
            @pl.when(j == p)
            def _(p=p):
                r = _dot(a_ref[rows, :], g_refs[p][...], TN)

                @pl.when(i == 0)
                def _():
                    acc_ref[...] = r

                @pl.when(i != 0)
                def _():
                    acc_ref[...] += r

        @pl.when(i == steps - 1)
        def _():
            o_ref[...] = acc_ref[...].astype(BF16)

    def piece_spec(p):
        return pl.BlockSpec((tm, nc), lambda j, i: (jnp.where(j == p, i, 0), 0))

    return pl.pallas_call(
        body, name=name, grid=(nb, steps),
        in_specs=[pl.BlockSpec((m, k), lambda j, i: (0, 0))] + [piece_spec(p) for p in range(nb)],
        out_specs=pl.BlockSpec((None, k, nc), lambda j, i: (j, 0, 0)),
        out_shape=jax.ShapeDtypeStruct((nb, k, nc), BF16), scratch_shapes=[pltpu.VMEM((k, nc), F32)],
        compiler_params=_params(),
    )(a, *pieces)


def _rms(x, g):
    return x * lax.rsqrt(jnp.mean(x * x, axis=-1, keepdims=True) + EPS) * g


def _rms_bwd(x, g, dy):
    r = lax.rsqrt(jnp.mean(x * x, axis=-1, keepdims=True) + EPS)
    xh = x * r
    dxh = dy * g
    dx = r * (dxh - xh * jnp.mean(dxh * xh, axis=-1, keepdims=True))
    return dx, jnp.sum(dy * xh, axis=0, keepdims=True)


def _accumulate(ref, val):
    @pl.when(pl.program_id(0) == 0)
    def _():
        ref[...] = val

    @pl.when(pl.program_id(0) != 0)
    def _():
        ref[...] += val


def _norm_fwd(x, g, *, name):
    m, d = x.shape
    tm = _row_block(m, 512)

    def body(x_ref, g_ref, o_ref):
        o_ref[...] = _rms(x_ref[...], g_ref[...]).astype(BF16)

    return pl.pallas_call(
        body, name=name, grid=(m // tm,),
        in_specs=[pl.BlockSpec((tm, d), lambda i: (i, 0)), pl.BlockSpec((1, d), lambda i: (0, 0))],
        out_specs=pl.BlockSpec((tm, d), lambda i: (i, 0)),
        out_shape=jax.ShapeDtypeStruct((m, d), BF16), compiler_params=_params(),
    )(x, g)


def _norm_bwd(x, g, dy, add, *, name):
    m, d = x.shape
    tm = _row_block(m, 512)

    def body(x_ref, g_ref, dy_ref, add_ref, dx_ref, dg_ref):
        dx, dg = _rms_bwd(x_ref[...], g_ref[...], dy_ref[...])
        dx_ref[...] = dx + add_ref[...]
        _accumulate(dg_ref, dg)

    row = pl.BlockSpec((tm, d), lambda i: (i, 0))
    vec = pl.BlockSpec((1, d), lambda i: (0, 0))
    return pl.pallas_call(
        body, name=name, grid=(m // tm,), in_specs=[row, vec, row, row], out_specs=[row, vec],
        out_shape=[jax.ShapeDtypeStruct((m, d), F32), jax.ShapeDtypeStruct((1, d), F32)], compiler_params=_params(),
    )(x, g, dy, add)


def _silu(x):
    return x * jax.nn.sigmoid(x)


def _mix_block(oa, ob, ag, bg, ga, gb):
    parts = []
    for h in range(A_HEADS):
        o = oa[:, HEAD_A * h:HEAD_A * (h + 1)]
        parts.append(o * lax.rsqrt(jnp.mean(o * o, axis=-1, keepdims=True) + EPS))
    ya = jnp.concatenate(parts, axis=1) * ga * _silu(ag)
    low = lax.broadcasted_iota(jnp.int32, (1, LANES), 1) < HEAD_B
    parts = []
    for p in range(B_PAIRS):
        o = ob[:, LANES * p:LANES * (p + 1)]
        sq = o * o
        s_lo = jnp.sum(jnp.where(low, sq, 0.0), axis=-1, keepdims=True)
        s_hi = jnp.sum(jnp.where(low, 0.0, sq), axis=-1, keepdims=True)
        r = jnp.where(low, lax.rsqrt(s_lo * (1.0 / HEAD_B) + EPS), lax.rsqrt(s_hi * (1.0 / HEAD_B) + EPS))
        parts.append(o * r)
    yb = jnp.concatenate(parts, axis=1) * gb * _silu(bg)
    return jnp.concatenate([ya, yb], axis=1)


_GATE_A_BLOCK = 3
_GATE_B_BLOCK = 7


def _mix_fwd(oa, ob, proj, ga, gb, *, name):
    m, w = oa.shape
    tm = _row_block(m, 512)

    def body(oa_ref, ob_ref, ag_ref, bg_ref, ga_ref, gb_ref, y_ref):
        y_ref[...] = _mix_block(oa_ref[...], ob_ref[...], ag_ref[...], bg_ref[...], ga_ref[...], gb_ref[...]).astype(BF16)

    row = pl.BlockSpec((tm, w), lambda i: (i, 0))
    vec = pl.BlockSpec((1, w), lambda i: (0, 0))
    return pl.pallas_call(
        body, name=name, grid=(m // tm,),
        in_specs=[row, row, pl.BlockSpec((tm, w), lambda i: (i, _GATE_A_BLOCK)),
                  pl.BlockSpec((tm, w), lambda i: (i, _GATE_B_BLOCK)), vec, vec],
        out_specs=pl.BlockSpec((tm, 2 * w), lambda i: (i, 0)),
        out_shape=jax.ShapeDtypeStruct((m, 2 * w), BF16), compiler_params=_params(),
    )(oa, ob, proj, proj, ga, gb)


def _mix_bwd(dy, oa, ob, proj, ga, gb, *, name):
    m, w = oa.shape
    tm = _row_block(m, 256)

    def body(dy_ref, oa_ref, ob_ref, ag_ref, bg_ref, ga_ref, gb_ref, doa_ref, dob_ref, dag_ref, dbg_ref, dga_ref, dgb_ref):
        _, vjp = jax.vjp(_mix_block, oa_ref[...], ob_ref[...], ag_ref[...], bg_ref[...], ga_ref[...], gb_ref[...])
        doa, dob, dag, dbg, dga, dgb = vjp(dy_ref[...])
        doa_ref[...] = doa
        dob_ref[...] = dob.astype(BF16)
        dag_ref[...] = dag.astype(BF16)
        dbg_ref[...] = dbg.astype(BF16)
        _accumulate(dga_ref, dga)
        _accumulate(dgb_ref, dgb)

    row = pl.BlockSpec((tm, w), lambda i: (i, 0))
    vec = pl.BlockSpec((1, w), lambda i: (0, 0))
    big = jax.ShapeDtypeStruct((m, w), F32)
    half = jax.ShapeDtypeStruct((m, w), BF16)
    small = jax.ShapeDtypeStruct((1, w), F32)
    return pl.pallas_call(
        body, name=name, grid=(m // tm,),
        in_specs=[pl.BlockSpec((tm, 2 * w), lambda i: (i, 0)), row, row,
                  pl.BlockSpec((tm, w), lambda i: (i, _GATE_A_BLOCK)), pl.BlockSpec((tm, w), lambda i: (i, _GATE_B_BLOCK)), vec, vec],
        out_specs=[row, row, row, row, vec, vec], out_shape=[big, half, half, half, small, small], compiler_params=_params(),
    )(dy, oa, ob, proj, proj, ga, gb)


def _ple_block(h1, gp, pp, gpost):
    return h1 + jax.nn.sigmoid(gp) * _rms(pp, gpost)


def _ple_fwd(h1, gp, pp, gpost, *, name):
    m, d = h1.shape
    tm = _row_block(m, 512)

    def body(h_ref, gp_ref, pp_ref, g_ref, o_ref):
        o_ref[...] = _ple_block(h_ref[...], gp_ref[...], pp_ref[...], g_ref[...])

    row = pl.BlockSpec((tm, d), lambda i: (i, 0))
    vec = pl.BlockSpec((1, d), lambda i: (0, 0))
    return pl.pallas_call(
        body, name=name, grid=(m // tm,), in_specs=[row, row, row, vec], out_specs=row,
        out_shape=jax.ShapeDtypeStruct((m, d), F32), compiler_params=_params(),
    )(h1, gp, pp, gpost)


def _ple_bwd(dh2, gp, pp, gpost, *, name):
    m, d = dh2.shape
    tm = _row_block(m, 512)

    def body(dh_ref, gp_ref, pp_ref, g_ref, dgp_ref, dpp_ref, dg_ref):
        dh = dh_ref[...]
        gate = jax.nn.sigmoid(gp_ref[...])
        pe = _rms(pp_ref[...], g_ref[...])
        dgp_ref[...] = (dh * pe * (gate * (1.0 - gate))).astype(BF16)
        dpp, dg = _rms_bwd(pp_ref[...], g_ref[...], dh * gate)
        dpp_ref[...] = dpp.astype(BF16)
        _accumulate(dg_ref, dg)

    row = pl.BlockSpec((tm, d), lambda i: (i, 0))
    vec = pl.BlockSpec((1, d), lambda i: (0, 0))
    big = jax.ShapeDtypeStruct((m, d), BF16)
    return pl.pallas_call(
        body, name=name, grid=(m // tm,), in_specs=[row, row, row, vec], out_specs=[row, row, vec],
        out_shape=[big, big, jax.ShapeDtypeStruct((1, d), F32)], compiler_params=_params(),
    )(dh2, gp, pp, gpost)


def _loss_head(h, gf, target, *, name):
    m, d = h.shape
    tm = _row_block(m, 512)

    def body(h_ref, g_ref, t_ref, loss_ref, dh_ref, dg_ref):
        x = h_ref[...]
        g = g_ref[...]
        err = _rms(x, g) - t_ref[...]
        part = 0.5 * jnp.sum(jnp.mean(err * err, axis=-1, keepdims=True), axis=0, keepdims=True)
        dx, dg = _rms_bwd(x, g, err * (1.0 / d))
        dh_ref[...] = dx
        _accumulate(dg_ref, dg)
        _accumulate(loss_ref, jnp.broadcast_to(part, (8, LANES)))

    row = pl.BlockSpec((tm, d), lambda i: (i, 0))
    vec = pl.BlockSpec((1, d), lambda i: (0, 0))
    return pl.pallas_call(
        body, name=name, grid=(m // tm,), in_specs=[row, vec, row],
        out_specs=[pl.BlockSpec((8, LANES), lambda i: (0, 0)), row, vec],
        out_shape=[jax.ShapeDtypeStruct((8, LANES), F32), jax.ShapeDtypeStruct((m, d), F32), jax.ShapeDtypeStruct((1, d), F32)],
        compiler_params=_params(),
    )(h, gf, target)


def _hgrn_pre(aq, af, lbv):
    sq = jax.nn.sigmoid(aq)
    sg = jax.nn.sigmoid(-af)
    k = (1.0 - lbv) * sg
    return aq * sq, sq, sg, k, jnp.log1p(-k)


def _tri(n, cmp):
    r = lax.broadcasted_iota(jnp.int32, (n, n), 0)
    c = lax.broadcasted_iota(jnp.int32, (n, n), 1)
    return cmp(r, c).astype(BF16)


def _hgrn_intra(q, k, b, b_scr):
    col = lax.broadcasted_iota(jnp.int32, (SUB, CHUNK), 1)
    rows = [jnp.zeros((SUB, CHUNK), F32)]
    facs = [None]
    for i in range(1, CHUNK // SUB):
        r = b_scr[SUB * i - 1:SUB * i, :]
        fq = jnp.exp(b[SUB * i:SUB * (i + 1)] - r)
        fk = jnp.exp(jnp.minimum(r - b, 0.0))
        qs = q[SUB * i:SUB * (i + 1)] * fq
        ks = k * fk
        rows.append(jnp.where(col < SUB * i, _dot(qs, ks, NT), 0.0))
        facs.append((fq, fk, qs, ks))
    return jnp.concatenate(rows, axis=0), facs


def _hgrn_diag_weights(i, k, b, q_scr, b_scr):
    sub = lax.broadcasted_iota(jnp.int32, (SUB, LANES), 0)
    ki = k[SUB * i:SUB * (i + 1)]
    bi = b[SUB * i:SUB * (i + 1)]
    es, ws = [], []
    for tau in range(SUB):
        t = SUB * i + tau
        e = jnp.exp(jnp.where(sub <= tau, b_scr[t:t + 1, :] - bi, NEG_BIG))
        es.append(e)
        ws.append((q_scr[t:t + 1, :] * ki) * e)
    return es, jnp.concatenate(ws, axis=0)


def _hgrn_fwd(proj, lb_row, *, name):
    s = proj.shape[0]
    tb = _row_block(s, 512)
    nch = tb // CHUNK
    wide = HGRN_HEADS_PER_STEP * HEAD_A

    def body(aq_all, af_all, ai_all, lb_all, o_all, st_all, s_all, q_all, b_all, od_all):
        @pl.when(pl.program_id(1) == 0)
        def _():
            s_all[...] = jnp.zeros_like(s_all)

        tril = _tri(CHUNK, lambda r, c: r >= c)
        ones = jnp.ones((LANES, LANES), BF16)

        def head_chunk(hh, ci):
            lanes = slice(HEAD_A * hh, HEAD_A * (hh + 1))
            aq_ref, af_ref, ai_ref, o_ref = aq_all.at[:, lanes], af_all.at[:, lanes], ai_all.at[:, lanes], o_all.at[:, lanes]
            st_ref, s_scr, q_scr, b_scr, od_scr = st_all.at[hh], s_all.at[hh], q_all.at[hh], b_all.at[hh], od_all.at[hh]
            lbv = lb_all[:, lanes]
            r0 = pl.multiple_of(ci * CHUNK, CHUNK)
            q, _, _, k, g = _hgrn_pre(aq_ref[pl.ds(r0, CHUNK), :], af_ref[pl.ds(r0, CHUNK), :], lbv)
            v = ai_ref[pl.ds(r0, CHUNK), :]
            st0 = s_scr[...]
            st_ref[ci] = st0
            b = _dot_exact_lhs(tril, g, 3)
            b_scr[...] = b
            q_scr[...] = q
            bl = b_scr[CHUNK - 1:CHUNK, :]
            o = _dot(q * jnp.exp(b), st0, NT)
            a, _ = _hgrn_intra(q, k, b, b_scr)
            o = o + _dot(a, v, NN)
            for i in range(CHUNK // SUB):
                _, wst = _hgrn_diag_weights(i, k, b, q_scr, b_scr)
                rep = _dot(wst, ones, NN)
                vi = v[SUB * i:SUB * (i + 1)]
                for tau in range(SUB):
                    t = SUB * i + tau
                    od_scr[t:t + 1, :] = jnp.sum(rep[SUB * tau:SUB * (tau + 1)] * vi, axis=0, keepdims=True)
            o_ref[pl.ds(r0, CHUNK), :] = o + od_scr[...]
            s_scr[...] = st0 * jnp.exp(bl) + _dot(v, k * jnp.exp(bl - b), TN)

        def chunk(ci, carry):
            for hh in range(HGRN_HEADS_PER_STEP):
                head_chunk(hh, ci)
            return carry

        lax.fori_loop(0, nch, chunk, 0)

    def col(block0):
        return pl.BlockSpec((tb, wide), lambda h, t: (t, block0 + h))

    steps = A_HEADS // HGRN_HEADS_PER_STEP
    per_head = pltpu.VMEM((HGRN_HEADS_PER_STEP, CHUNK, HEAD_A), F32)
    return pl.pallas_call(
        body, name=name, grid=(steps, s // tb),
        in_specs=[col(0), col(steps), col(2 * steps), pl.BlockSpec((1, wide), lambda h, t: (0, h))],
        out_specs=[pl.BlockSpec((tb, wide), lambda h, t: (t, h)),
                   pl.BlockSpec((HGRN_HEADS_PER_STEP, nch, HEAD_A, HEAD_A), lambda h, t: (h, t, 0, 0))],
        out_shape=[jax.ShapeDtypeStruct((s, A_HEADS * HEAD_A), F32),
                   jax.ShapeDtypeStruct((A_HEADS, s // CHUNK, HEAD_A, HEAD_A), F32)],
        scratch_shapes=[pltpu.VMEM((HGRN_HEADS_PER_STEP, HEAD_A, HEAD_A), F32), per_head, per_head, per_head],
        compiler_params=_params(),
    )(proj, proj, proj, lb_row)


def _hgrn_bwd(proj, lb_row, states, do_a, *, name):
    s = proj.shape[0]
    tb = _row_block(s, 512)
    nch = tb // CHUNK
    nblk = s // tb

    wide = HGRN_HEADS_PER_STEP * HEAD_A

    def body(aq_all, af_all, ai_all, lb_all, st_all, do_all, daq_all, daf_all, dai_all, dlb_all,
             ds_all, q_all, b_all, dq_all, dos_all):
        @pl.when(pl.program_id(1) == 0)
        def _():
            ds_all[...] = jnp.zeros_like(ds_all)
            dlb_all[...] = jnp.zeros_like(dlb_all)

        tril = _tri(CHUNK, lambda r, c: r >= c)
        triu = _tri(CHUNK, lambda r, c: c >= r)
        ones = jnp.ones((LANES, LANES), BF16)
        col = lax.broadcasted_iota(jnp.int32, (SUB, CHUNK), 1)
        row = lax.broadcasted_iota(jnp.int32, (CHUNK, LANES), 0)
        nsub = CHUNK // SUB

        def head_chunk(hh, ci):
            lanes = slice(HEAD_A * hh, HEAD_A * (hh + 1))
            aq_ref, af_ref, ai_ref, do_ref = aq_all.at[:, lanes], af_all.at[:, lanes], ai_all.at[:, lanes], do_all.at[:, lanes]
            daq_ref, daf_ref, dai_ref, dlb_ref = daq_all.at[:, lanes], daf_all.at[:, lanes], dai_all.at[:, lanes], dlb_all.at[:, lanes]
            st_ref, ds_scr, q_scr, b_scr = st_all.at[hh], ds_all.at[hh], q_all.at[hh], b_all.at[hh]
            dq_scr, do_scr = dq_all.at[hh], dos_all.at[hh]
            lbv = lb_all[:, lanes]
            r0 = pl.multiple_of(ci * CHUNK, CHUNK)
            aq = aq_ref[pl.ds(r0, CHUNK), :]
            q, sq, sg, k, g = _hgrn_pre(aq, af_ref[pl.ds(r0, CHUNK), :], lbv)
            v = ai_ref[pl.ds(r0, CHUNK), :]
            do = do_ref[pl.ds(r0, CHUNK), :]
            st0 = st_ref[ci]
            dst1 = ds_scr[...]
            b = _dot_exact_lhs(tril, g, 3)
            b_scr[...] = b
            q_scr[...] = q
            do_scr[...] = do
            bl = b_scr[CHUNK - 1:CHUNK, :]
            eb = jnp.exp(b)
            ebl = jnp.exp(bl)
            ekd = jnp.exp(bl - b)
            kd = k * ekd
            a, facs = _hgrn_intra(q, k, b, b_scr)

            da = _dot(do, v, NT)
            dv = _dot(a, do, TN) + _dot(kd, dst1, NT)
            dq = _dot(do, st0, NN) * eb
            dk_inter = _dot(v, dst1, NN) * ekd
            dk = dk_inter
            dq_rows = [jnp.zeros((SUB, LANES), F32)]
            for i in range(1, nsub):
                fq, fk, qs, ks = facs[i]
                da_i = jnp.where(col < SUB * i, da[SUB * i:SUB * (i + 1)], 0.0)
                dq_rows.append(_dot(da_i, ks, NN) * fq)
                dk = dk + _dot(da_i, qs, TN) * fk
            dq = dq + jnp.concatenate(dq_rows, axis=0)

            dv_rows, dk_rows = [], []
            for i in range(nsub):
                es, wst = _hgrn_diag_weights(i, k, b, q_scr, b_scr)
                ki = k[SUB * i:SUB * (i + 1)]
                vi = v[SUB * i:SUB * (i + 1)]
                rep = _dot(wst, ones, NN)
                mst = jnp.concatenate([do_scr[SUB * i + tau:SUB * i + tau + 1, :] * vi for tau in range(SUB)], axis=0)
                drep = _dot(mst, ones, NN)
                dvi = jnp.zeros((SUB, LANES), F32)
                dki = jnp.zeros((SUB, LANES), F32)
                for tau in range(SUB):
                    t = SUB * i + tau
                    gt = drep[SUB * tau:SUB * (tau + 1)] * es[tau]
                    dvi = dvi + rep[SUB * tau:SUB * (tau + 1)] * do_scr[t:t + 1, :]
                    dki = dki + gt * q_scr[t:t + 1, :]
                    dq_scr[t:t + 1, :] = jnp.sum(gt * ki, axis=0, keepdims=True)
                dv_rows.append(dvi)
                dk_rows.append(dki)
            dv = dv + jnp.concatenate(dv_rows, axis=0)
            dk = dk + jnp.concatenate(dk_rows, axis=0)
            dq = dq + dq_scr[...]

            ds_scr[...] = dst1 * ebl + _dot(do, q * eb, TN)
            last = jnp.sum(k * dk_inter, axis=0, keepdims=True) + ebl * jnp.sum(st0 * dst1, axis=0, keepdims=True)
            db = q * dq - k * dk + jnp.where(row == CHUNK - 1, last, 0.0)
            dg = _dot_exact_lhs(triu, db, 3)

            dkt = dk - dg / (1.0 - k)
            daq_ref[pl.ds(r0, CHUNK), :] = (dq * (sq * (1.0 + aq * (1.0 - sq)))).astype(BF16)
            daf_ref[pl.ds(r0, CHUNK), :] = (dkt * (1.0 - lbv) * (-(sg * (1.0 - sg)))).astype(BF16)
            dai_ref[pl.ds(r0, CHUNK), :] = dv.astype(BF16)
            dlb_ref[...] += jnp.sum(dkt * (-sg), axis=0, keepdims=True)

        def chunk(cj, carry):
            for hh in range(HGRN_HEADS_PER_STEP):
                head_chunk(hh, nch - 1 - cj)
            return carry

        lax.fori_loop(0, nch, chunk, 0)

    def col_in(block0):
        return pl.BlockSpec((tb, wide), lambda h, t: (nblk - 1 - t, block0 + h))

    steps = A_HEADS // HGRN_HEADS_PER_STEP
    head_col = pl.BlockSpec((tb, wide), lambda h, t: (nblk - 1 - t, h))
    vec = pl.BlockSpec((1, wide), lambda h, t: (0, h))
    half = jax.ShapeDtypeStruct((s, A_HEADS * HEAD_A), BF16)
    per_head = pltpu.VMEM((HGRN_HEADS_PER_STEP, CHUNK, HEAD_A), F32)
    return pl.pallas_call(
        body, name=name, grid=(steps, nblk),
        in_specs=[col_in(0), col_in(steps), col_in(2 * steps), vec,
                  pl.BlockSpec((HGRN_HEADS_PER_STEP, nch, HEAD_A, HEAD_A), lambda h, t: (h, nblk - 1 - t, 0, 0)), head_col],
        out_specs=[head_col, head_col, head_col, vec],
        out_shape=[half, half, half, jax.ShapeDtypeStruct((1, A_HEADS * HEAD_A), F32)],
        scratch_shapes=[pltpu.VMEM((HGRN_HEADS_PER_STEP, HEAD_A, HEAD_A), F32), per_head, per_head, per_head, per_head],
        compiler_params=_params(),
    )(proj, proj, proj, lb_row, states, do_a)


_BQ_BLOCK = 16
_BK_BLOCK = 20
_BV_BLOCK = 24
SB_SCALE = HEAD_B ** -0.5
LOG2_E = 1.4426950408889634
SB_TQ = 512
SB_TK = 256


def _sb_blocks(s):
    tq = min(SB_TQ, s)
    tk = min(SB_TK, s)
    assert s % tq == 0 and tq % tk == 0 and s // tk <= LANES
    return tq, tk, (2 if (tq // tk) % 2 == 0 else 1)


def _sb2_weights(qh, kb, carry, incl, mask):
    z = _dot(qh, kb, NT) * LOG2_E
    sp = jnp.maximum(z, 0.0) + jnp.log2(1.0 + jnp.exp2(-jnp.abs(z)))
    if mask is not None:
        sp = jnp.where(mask, sp, 0.0)
    suffix = _dot(sp, incl, NN)
    w = jnp.exp2(z - suffix + carry)
    if mask is not None:
        w = jnp.where(mask, w, 0.0)
    return z, sp, w, suffix[:, 0:1]


def _sb_mask(qi, j, tq, tk, row0):
    row = qi * tq + row0 + lax.broadcasted_iota(jnp.int32, (tq - row0, tk), 0)
    col = j * tk + lax.broadcasted_iota(jnp.int32, (tq - row0, tk), 1)
    return col < row


def _below(x, row0, new_tail):
    return new_tail if row0 == 0 else jnp.concatenate([x[:row0], new_tail], axis=0)


def _ride_along(ex, refs, n_in, n_out, first, last):
    if ex is None:
        return refs, lambda: None
    own = refs[:n_in] + refs[n_in + ex.n:n_in + ex.n + n_out]
    xin = refs[n_in:n_in + ex.n]
    xout = refs[n_in + ex.n + n_out:n_in + 2 * ex.n + n_out]
    rest = refs[n_in + 2 * ex.n + n_out:]
    sems, scratch = rest[len(rest) - 3:], rest[:len(rest) - 3]

    @pl.when(first)
    def _():
        ex.start(xin, xout, sems)

    def finish():
        @pl.when(last)
        def _():
            ex.finish(xin, xout, sems)

    return own + scratch, finish


def _sb2_fwd(proj, *, name, exchange=None):
    s = proj.shape[0]
    tq, tk, unroll = _sb_blocks(s)
    nq, ratio = s // tq, tq // tk
    ex = None if exchange is None else _Exchange(*exchange)

    def body(*refs):
        first = jnp.logical_and(pl.program_id(0) == 0, pl.program_id(1) == 0)
        last = jnp.logical_and(pl.program_id(0) == B_PAIRS - 1, pl.program_id(1) == nq - 1)
        (q_ref, k_ref, v_ref, o_ref, c_ref), finish = _ride_along(ex, refs, 3, 2, first, last)
        qi = pl.program_id(1)
        lane = lax.broadcasted_iota(jnp.int32, (1, LANES), 1)
        low = lane < HEAD_B

        def heads(x):
            return jnp.where(low, x, 0.0).astype(BF16), jnp.where(low, 0.0, x).astype(BF16)

        qh = heads(q_ref[...] * SB_SCALE)
        incl = _tri(tk, lambda r, c: r >= c)
        o_ref[...] = jnp.zeros_like(o_ref)
        c_ref[...] = jnp.zeros_like(c_ref)

        def tile(j, c, masked, row0=0):
            s0 = pl.multiple_of(j * tk, tk)
            kb = k_ref[pl.ds(s0, tk), :].astype(BF16)
            vh = heads(v_ref[pl.ds(s0, tk), :])
            mask = _sb_mask(qi, j, tq, tk, row0) if masked else None
            c_new = []
            acc = jnp.zeros((tq - row0, LANES), F32)
            for h in range(2):
                _, _, w, rowsum = _sb2_weights(qh[h][row0:], kb, c[h][row0:], incl, mask)
                c_ref[h] = jnp.where(lane == j, c[h], c_ref[h])
                c_new.append(_below(c[h], row0, c[h][row0:] - rowsum))
                acc = acc + _dot(w, vh[h], NN)
            o_ref[row0:, :] += acc
            return tuple(c_new)

        zc = jnp.zeros((tq, 1), F32)
        c = (zc, zc)
        for r in reversed(range(ratio)):
            c = tile(qi * ratio + r, c, True, r * tk)
        def several(jj, c):
            for r in range(unroll):
                c = tile(qi * ratio - 1 - (jj * unroll + r), c, False)
            return c

        lax.fori_loop(0, qi * (ratio // unroll), several, c)
        finish()

    def whole(block0):
        return pl.BlockSpec((s, LANES), lambda p, i: (0, block0 + p))

    in_specs = [pl.BlockSpec((tq, LANES), lambda p, i: (i, _BQ_BLOCK + p)), whole(_BK_BLOCK), whole(_BV_BLOCK)]
    out_specs = [pl.BlockSpec((tq, LANES), lambda p, i: (i, p)), pl.BlockSpec((None, 2, tq, LANES), lambda p, i: (p, 0, i, 0))]
    out_shape = [jax.ShapeDtypeStruct((s, B_PAIRS * LANES), F32), jax.ShapeDtypeStruct((B_PAIRS, 2, s, LANES), F32)]
    args, scratch = [proj, proj, proj], []
    if ex is not None:
        in_specs, out_specs, out_shape = in_specs + ex.in_specs, out_specs + ex.out_specs, out_shape + ex.out_shape
        args, scratch = args + list(exchange[0]), ex.scratch_shapes
    out = pl.pallas_call(
        body, name=name, grid=(B_PAIRS, nq), in_specs=in_specs, out_specs=out_specs, out_shape=out_shape,
        scratch_shapes=scratch, compiler_params=_params(),
    )(*args)
    return out[0], out[1], out[2:]


def _sb2_bwd(proj, carries, do_b, *, name, exchange=None):
    s = proj.shape[0]
    tq, tk, unroll = _sb_blocks(s)
    nq, nk, ratio = s // tq, s // tk, tq // tk
    ex = None if exchange is None else _Exchange(*exchange)

    def body(*refs):
        first = jnp.logical_and(pl.program_id(0) == 0, pl.program_id(1) == 0)
        last = jnp.logical_and(pl.program_id(0) == B_PAIRS - 1, pl.program_id(1) == nq - 1)
        own, finish = _ride_along(ex, refs, 5, 3, first, last)
        q_ref, k_ref, v_ref, c_ref, do_ref, dq_ref, dk_ref, dv_ref, dkt_scr, dvt_scr = own
        qi = pl.program_id(1)

        @pl.when(qi == 0)
        def _():
            dkt_scr[...] = jnp.zeros_like(dkt_scr)
            dvt_scr[...] = jnp.zeros_like(dvt_scr)

        lane = lax.broadcasted_iota(jnp.int32, (1, LANES), 1)
        low = lane < HEAD_B
        low_row = lax.broadcasted_iota(jnp.int32, (LANES, 1), 0) < HEAD_B

        def heads(x):
            return jnp.where(low, x, 0.0).astype(BF16), jnp.where(low, 0.0, x).astype(BF16)

        def heads_t(x):
            xt = x.T
            return jnp.where(low_row, xt, 0.0).astype(BF16), jnp.where(low_row, 0.0, xt).astype(BF16)

        q = q_ref[...] * SB_SCALE
        do = do_ref[...].astype(F32)
        qh, qth = heads(q), heads_t(q)
        doh, doth = heads(do), heads_t(do)
        cst = (c_ref[0], c_ref[1])
        incl = _tri(tk, lambda r, c: r >= c)
        pre = _tri(tk, lambda r, c: r <= c)

        def tile(j, carry, masked, row0=0):
            ec, dq = carry
            s0 = pl.multiple_of(j * tk, tk)
            kf = k_ref[pl.ds(s0, tk), :]
            kb = kf.astype(BF16)
            kh = heads(kf)
            vb = v_ref[pl.ds(s0, tk), :].astype(BF16)
            mask = _sb_mask(qi, j, tq, tk, row0) if masked else None
            ec_new = []
            dkt = jnp.zeros((LANES, tk), F32)
            dvt = jnp.zeros((LANES, tk), F32)
            dq_add = jnp.zeros((tq - row0, LANES), F32)
            for h in range(2):
                cin = jnp.sum(jnp.where(lane == j, cst[h][row0:], 0.0), axis=1, keepdims=True)
                z, sp, w, _ = _sb2_weights(qh[h][row0:], kb, cin, incl, mask)
                e = w * _dot(doh[h][row0:], vb, NT)
                prefix = _dot(e, pre, NN)
                dz = e - jnp.exp2(z - sp) * (ec[h][row0:] + prefix)
                if mask is not None:
                    dz = jnp.where(mask, dz, 0.0)
                ec_new.append(_below(ec[h], row0, ec[h][row0:] + prefix[:, tk - 1:tk]))
                dzb = dz.astype(BF16)
                wb = w.astype(BF16)
                dq_add = dq_add + _dot(dzb, kh[h], NN)
                dkt = dkt + _dot(qth[h][:, row0:], dzb, NN)
                dvt = dvt + _dot(doth[h][:, row0:], wb, NN)
            dkt_scr[j] += dkt
            dvt_scr[j] += dvt
            return tuple(ec_new), _below(dq, row0, dq[row0:] + dq_add)

        zc = jnp.zeros((tq, 1), F32)
        def several(jj, carry):
            for r in range(unroll):
                carry = tile(jj * unroll + r, carry, False)
            return carry

        carry = lax.fori_loop(0, qi * (ratio // unroll), several, ((zc, zc), jnp.zeros((tq, LANES), F32)))
        for r in range(ratio):
            carry = tile(qi * ratio + r, carry, True, r * tk)
        dq_ref[...] = (carry[1] * SB_SCALE).astype(BF16)

        @pl.when(qi == nq - 1)
        def _():
            for j in range(nk):
                dk_ref[tk * j:tk * (j + 1), :] = dkt_scr[j].T.astype(BF16)
                dv_ref[tk * j:tk * (j + 1), :] = dvt_scr[j].T.astype(BF16)

        finish()

    def whole_in(block0):
        return pl.BlockSpec((s, LANES), lambda p, i: (0, block0 + p))

    blk = pl.BlockSpec((tq, LANES), lambda p, i: (i, p))
    whole_out = pl.BlockSpec((s, LANES), lambda p, i: (0, p))
    big = jax.ShapeDtypeStruct((s, B_PAIRS * LANES), BF16)
    in_specs = [pl.BlockSpec((tq, LANES), lambda p, i: (i, _BQ_BLOCK + p)), whole_in(_BK_BLOCK), whole_in(_BV_BLOCK),
                pl.BlockSpec((None, 2, tq, LANES), lambda p, i: (p, 0, i, 0)), blk]
    out_specs, out_shape = [blk, whole_out, whole_out], [big, big, big]
    args = [proj, proj, proj, carries, do_b]
    scratch = [pltpu.VMEM((nk, LANES, tk), F32), pltpu.VMEM((nk, LANES, tk), F32)]
    if ex is not None:
        in_specs, out_specs, out_shape = in_specs + ex.in_specs, out_specs + ex.out_specs, out_shape + ex.out_shape
        args, scratch = args + list(exchange[0]), scratch + ex.scratch_shapes
    out = pl.pallas_call(
        body, name=name, grid=(B_PAIRS, nq), in_specs=in_specs, out_specs=out_specs, out_shape=out_shape,
        scratch_shapes=scratch, compiler_params=_params(),
    )(*args)
    return out[0], out[1], out[2], out[3:]


def _my_id():
    return 4 * lax.axis_index("x") + 2 * lax.axis_index("y") + lax.axis_index("c")


def _mesh_pos(d):
    return (d // 4, (d // 2) % 2, d % 2)


class _Exchange:
    def __init__(self, arrays, scatter):
        self.n = len(arrays)
        self.scatter = scatter
        self.in_specs = [pl.BlockSpec(memory_space=pl.ANY)] * self.n
        self.out_specs = [pl.BlockSpec(memory_space=pl.ANY)] * self.n
        self.out_shape = [
            jax.ShapeDtypeStruct(((N_DEV,) + a.shape[1:]) if scatter else ((N_DEV,) + a.shape), a.dtype) for a in arrays]
        self.scratch_shapes = [pltpu.SemaphoreType.DMA((N_DEV - 1, self.n)), pltpu.SemaphoreType.DMA((N_DEV - 1, self.n)),
                               pltpu.SemaphoreType.DMA((self.n,))]

    def _copies(self, ins, outs, sems):
        send_sems, recv_sems, local_sems = sems
        me = _my_id()

        def src(i, to):
            return ins[i].at[to] if self.scatter else ins[i]

        local = [pltpu.make_async_copy(src(i, me), outs[i].at[me], local_sems.at[i]) for i in range(self.n)]
        sends, recvs = [], []
        for step in range(1, N_DEV):
            to = (me + step) % N_DEV
            frm = (me + N_DEV - step) % N_DEV
            for i in range(self.n):
                sends.append(pltpu.make_async_remote_copy(
                    src_ref=src(i, to), dst_ref=outs[i].at[me],
                    send_sem=send_sems.at[step - 1, i], recv_sem=recv_sems.at[step - 1, i],
                    device_id=_mesh_pos(to), device_id_type=pl.DeviceIdType.MESH))
                recvs.append(pltpu.make_async_remote_copy(
                    src_ref=src(i, frm), dst_ref=outs[i].at[frm],
                    send_sem=send_sems.at[step - 1, i], recv_sem=recv_sems.at[step - 1, i],
                    device_id=_mesh_pos(frm), device_id_type=pl.DeviceIdType.MESH))
        return local, sends, recvs

    def start(self, ins, outs, sems):
        local, sends, _ = self._copies(ins, outs, sems)
        for cp in local + sends:
            cp.start()

    def finish(self, ins, outs, sems):
        local, sends, recvs = self._copies(ins, outs, sems)
        for cp in recvs:
            cp.wait_recv()
        for cp in sends:
            cp.wait_send()
        for cp in local:
            cp.wait()


def _exchange(arrays, scatter, *, name):
    ex = _Exchange(arrays, scatter)

    def body(*refs):
        ins, outs, sems = refs[:ex.n], refs[ex.n:2 * ex.n], refs[2 * ex.n:]
        ex.start(ins, outs, sems)
        ex.finish(ins, outs, sems)

    return pl.pallas_call(
        body, name=name, in_specs=ex.in_specs, out_specs=ex.out_specs, out_shape=ex.out_shape,
        scratch_shapes=ex.scratch_shapes,
    )(*arrays)


def _adamw_math(w, m, v, g):
    m2 = ADAM_B1 * m + (1.0 - ADAM_B1) * g
    v2 = ADAM_B2 * v + (1.0 - ADAM_B2) * (g * g)
    m_hat = m2 / (1.0 - ADAM_B1 ** ADAM_STEP)
    v_hat = v2 / (1.0 - ADAM_B2 ** ADAM_STEP)
    delta = -ADAM_LR * (m_hat / (jnp.sqrt(v_hat) + ADAM_EPS) + ADAM_WD * w)
    return delta, m2, v2


def _slot_sum(ref):
    g = ref[0].astype(F32)
    for d in range(1, N_DEV):
        g = g + ref[d].astype(F32)
    return g


def _adamw(w, m, v, slots, *, name):
    r, c = w.shape
    tr = _row_block(r, 256)

    def body(w_ref, m_ref, v_ref, s_ref, g_ref, d_ref, m2_ref, v2_ref):
        g = _slot_sum(s_ref)
        delta, m2, v2 = _adamw_math(w_ref[...], m_ref[...], v_ref[...], g)
        g_ref[...] = g
        d_ref[...] = delta
        m2_ref[...] = m2
        v2_ref[...] = v2

    row = pl.BlockSpec((tr, c), lambda i: (i, 0))
    shp = jax.ShapeDtypeStruct((r, c), F32)
    return pl.pallas_call(
        body, name=name, grid=(r // tr,),
        in_specs=[row, row, row, pl.BlockSpec((N_DEV, tr, c), lambda i: (0, i, 0))],
        out_specs=[row, row, row, row], out_shape=[shp, shp, shp, shp], compiler_params=_params(),
    )(w, m, v, slots)


def _layer_softmax(l_ref):
    l0 = l_ref[0:1, :]
    l1 = l_ref[1:2, :]
    mx = jnp.maximum(l0, l1)
    e0 = jnp.exp(l0 - mx)
    e1 = jnp.exp(l1 - mx)
    return e0 / (e0 + e1), e1 / (e0 + e1)


def _adamw_lb(w, m, v, slots, *, name):
    def body(w_ref, m_ref, v_ref, s_ref, g_ref, d_ref, m2_ref, v2_ref):
        sm0, sm1 = _layer_softmax(w_ref)
        dl1 = _slot_sum(s_ref) * (sm0 * sm1)
        for row, g in ((0, -dl1), (1, dl1)):
            sl = slice(row, row + 1)
            delta, m2, v2 = _adamw_math(w_ref[sl, :], m_ref[sl, :], v_ref[sl, :], g)
            g_ref[sl, :] = g
            d_ref[sl, :] = delta
            m2_ref[sl, :] = m2
            v2_ref[sl, :] = v2

    shp = jax.ShapeDtypeStruct(w.shape, F32)
    return pl.pallas_call(body, name=name, out_shape=[shp, shp, shp, shp])(w, m, v, slots)


def _lower_bound_1(lb_logits, *, name):
    def body(l_ref, o_ref):
        sm0, sm1 = _layer_softmax(l_ref)
        o_ref[...] = (sm0 + sm1) - sm0

    return pl.pallas_call(body, name=name, out_shape=jax.ShapeDtypeStruct((1, lb_logits.shape[1]), F32))(lb_logits)


_SMALL = ("norm_mix", "a_out_norm", "b_out_norm", "ple_gate_norm", "ple_post_norm")


def _pack_rows(vectors):
    offs, rows = [], 0
    for vec in vectors:
        offs.append(rows)
        rows += vec.shape[0] // LANES
    flat = jnp.concatenate([vec.reshape(-1, LANES) for vec in vectors], axis=0)
    pad = (-rows) % 8
    if pad:
        flat = jnp.concatenate([flat, jnp.zeros((pad, LANES), F32)], axis=0)
    return flat, offs


def kernel(x, p, norm_mix, w_in, a_out_norm, b_out_norm, w_out, lb_logits, ple_gate_norm, w_ple_gate, w_ple_proj, ple_post_norm, final_norm, loss_target, m_norm_mix, m_w_in, m_a_out_norm, m_b_out_norm, m_w_out, m_lb_logits, m_ple_gate_norm, m_w_ple_gate, m_w_ple_proj, m_ple_post_norm, m_final_norm, v_norm_mix, v_w_in, v_a_out_norm, v_b_out_norm, v_w_out, v_lb_logits, v_ple_gate_norm, v_w_ple_gate, v_w_ple_proj, v_ple_post_norm, v_final_norm):
    depth = w_in.shape[0]
    assert depth == 2
    s, d = x.shape[1], x.shape[2]
    h = x.reshape(s, d)
    target = loss_target.reshape(s, d)

    (win0,) = _exchange([w_in[0].astype(BF16)], scatter=False, name="gather_w_in0")
    later = [w_in[1].astype(BF16), w_out.astype(BF16), w_ple_gate.astype(BF16), w_ple_proj.astype(BF16)]
    lb1 = _lower_bound_1(lb_logits, name="lower_bound")
    lbs = [jnp.zeros_like(lb1), lb1]

    saved = []
    for l in range(depth):
        win = win0 if l == 0 else win1
        u = _norm_fwd(h, norm_mix[l:l + 1], name="norm_mix")
        proj = _mm_nn(u, win, name="mm_in")
        o_a, states = _hgrn_fwd(proj, lbs[l], name="hgrn_fwd")
        if l == 0:
            o_b, carries, (win1, wout_g, wpg_g, wpp_g) = _sb2_fwd(proj, name="sb_fwd_gather", exchange=(later, False))
        else:
            o_b, carries, _ = _sb2_fwd(proj, name="sb_fwd")
        wout = wout_g[:, l].reshape(1, d, d)
        wpg = wpg_g[:, l].reshape(1, d, d)
        wpp = jnp.transpose(wpp_g[:, l], (1, 0, 2)).reshape(1, p.shape[-1], d)
        y = _mix_fwd(o_a, o_b, proj, a_out_norm[l:l + 1], b_out_norm[l:l + 1], name="mix_fwd")
        h1 = _mm_nn(y, wout, add=h, name="mm_out")
        r1 = _norm_fwd(h1, ple_gate_norm[l:l + 1], name="norm_gate")
        gp = _mm_nn(r1, wpg, name="mm_gate")
        pl_in = p[l].reshape(s, p.shape[-1])
        pp = _mm_nn(pl_in, wpp, name="mm_ple")
        h2 = _ple_fwd(h1, gp, pp, ple_post_norm[l:l + 1], name="ple_fwd")
        saved.append((h, u, proj, o_a, states, o_b, carries, y, h1, r1, gp, pp, pl_in, win, wout, wpg, wpp))
        h = h2

    loss_part, dh, dg_final = _loss_head(h, final_norm.reshape(1, d), target, name="loss_head")

    big = ("w_in", "w_out", "w_ple_gate", "w_ple_proj")
    grads = [None] * depth
    for l in reversed(range(depth)):
        h0, u, proj, o_a, states, o_b, carries, y, h1, r1, gp, pp, pl_in, win, wout, wpg, wpp = saved[l]
        dgp, dpp, dg_post = _ple_bwd(dh, gp, pp, ple_post_norm[l:l + 1], name="ple_bwd")
        dwpp = _mm_tn(pl_in, [dpp], name="mm_ple_dw")
        dwpp = jnp.transpose(dwpp.reshape(p.shape[-1], N_DEV, d // N_DEV), (1, 0, 2))
        dwpg = _mm_tn(r1, [dgp], name="mm_gate_dw").reshape(N_DEV, d // N_DEV, d)
        dr1, _ = _mm_nt([dgp], wpg, name="mm_gate_dx")
        dh1, dg_pg = _norm_bwd(h1, ple_gate_norm[l:l + 1], dr1, dh, name="norm_gate_bwd")
        dwout = _mm_tn(y, [dh1], name="mm_out_dw").reshape(N_DEV, d // N_DEV, d)
        dy, _ = _mm_nt([dh1], wout, name="mm_out_dx")
        do_a, do_b, dag, dbg, dg_a, dg_b = _mix_bwd(dy, o_a, o_b, proj, a_out_norm[l:l + 1], b_out_norm[l:l + 1], name="mix_bwd")
        if l == 0:
            ready = [grads[1][n] for n in big] + [dwout, dwpg, dwpp]
            dbq, dbk, dbv, early_slots = _sb2_bwd(proj, carries, do_b, name="sb_bwd_scatter", exchange=(ready, True))
        else:
            dbq, dbk, dbv, _ = _sb2_bwd(proj, carries, do_b, name="sb_bwd")
        daq, daf, dai, dlb = _hgrn_bwd(proj, lbs[l], states, do_a, name="hgrn_bwd")
        dproj = [daq, daf, dai, dag, dbq, dbk, dbv, dbg]
        dwin = _mm_tn(u, dproj, name="mm_in_dw")
        if l == 0:
            du, (win0_slots,) = _mm_nt(dproj, win, name="mm_in_dx_scatter", exchange=([dwin], True))
        else:
            du, _ = _mm_nt(dproj, win, name="mm_in_dx")
        dh, dg_mix = _norm_bwd(h0, norm_mix[l:l + 1], du, dh1, name="norm_mix_bwd")
        grads[l] = dict(w_in=dwin, w_out=dwout, w_ple_gate=dwpg, w_ple_proj=dwpp, norm_mix=dg_mix, a_out_norm=dg_a,
                        b_out_norm=dg_b, ple_gate_norm=dg_pg, ple_post_norm=dg_post, lb=dlb)

    grad_x = dh.reshape(x.shape)
    loss = lax.psum(loss_part[0, 0], ("x", "y", "c"))

    slots = {("w_in", 0): win0_slots}
    for k, key in enumerate([(n, 1) for n in big] + [(n, 0) for n in big[1:]]):
        slots[key] = early_slots[k]
    results = {}
    for n, w, m, v in zip(big, (w_in, w_out, w_ple_gate, w_ple_proj), (m_w_in, m_w_out, m_w_ple_gate, m_w_ple_proj),
                          (v_w_in, v_w_out, v_w_ple_gate, v_w_ple_proj)):
        per_layer = [_adamw(w[l], m[l], v[l], slots[(n, l)], name="adamw_" + n) for l in range(depth)]
        results[n] = [jnp.stack([per_layer[l][k] for l in range(depth)]) for k in range(4)]

    small_w = dict(norm_mix=norm_mix, a_out_norm=a_out_norm, b_out_norm=b_out_norm, ple_gate_norm=ple_gate_norm,
                   ple_post_norm=ple_post_norm)
    small_m = dict(norm_mix=m_norm_mix, a_out_norm=m_a_out_norm, b_out_norm=m_b_out_norm, ple_gate_norm=m_ple_gate_norm,
                   ple_post_norm=m_ple_post_norm)
    small_v = dict(norm_mix=v_norm_mix, a_out_norm=v_a_out_norm, b_out_norm=v_b_out_norm, ple_gate_norm=v_ple_gate_norm,
                   ple_post_norm=v_ple_post_norm)
    g_vecs = [jnp.concatenate([grads[l][n].reshape(-1) for l in range(depth)]) for n in _SMALL] + [dg_final.reshape(-1)]
    g_pack, offs = _pack_rows(g_vecs)
    w_pack, _ = _pack_rows([small_w[n].reshape(-1) for n in _SMALL] + [final_norm])
    m_pack, _ = _pack_rows([small_m[n].reshape(-1) for n in _SMALL] + [m_final_norm])
    v_pack, _ = _pack_rows([small_v[n].reshape(-1) for n in _SMALL] + [v_final_norm])
    small_slots, lb_slots = _exchange([g_pack, grads[1]["lb"]], scatter=False, name="gather_small_grads")
    packed = _adamw(w_pack, m_pack, v_pack, small_slots, name="adamw_small")
    results["lb_logits"] = _adamw_lb(lb_logits, m_lb_logits, v_lb_logits, lb_slots, name="adamw_lb")
    shapes = [small_w[n].shape for n in _SMALL] + [final_norm.shape]
    for n, off, shp in zip(_SMALL + ("final_norm",), offs, shapes):
        rows = 1
        for dim in shp:
            rows *= dim
        rows //= LANES
        results[n] = [o[off:off + rows].reshape(shp) for o in packed]

    order = ("norm_mix", "w_in", "a_out_norm", "b_out_norm", "w_out", "lb_logits", "ple_gate_norm", "w_ple_gate",
             "w_ple_proj", "ple_post_norm", "final_norm")
    out = [loss, grad_x]
    for k in range(4):
        out += [results[n][k] for n in order]
    return tuple(out)
```

```python
import jax
import jax.numpy as jnp
from jax import lax
from jax.experimental import pallas as pl
from jax.experimental.pallas import tpu as pltpu

F32 = jnp.float32
BF16 = jnp.bfloat16

N_DEV = 8
EPS = 1e-6
A_HEADS = 4
HEAD_A = 128
B_PAIRS = 4
HEAD_B = 64
CHUNK = 64
SUB = 16
HGRN_HEADS_PER_STEP = 2
LANES = 128
NEG_BIG = -1e30

ADAM_LR = 0.001
ADAM_B1 = 0.9
ADAM_B2 = 0.999
ADAM_EPS = 1e-08
ADAM_WD = 0.01
ADAM_STEP = 10

VMEM_LIMIT = 56 * 1024 * 1024

NN = ((1,), (0,))
NT = ((1,), (1,))
TN = ((0,), (0,))


def _dot(a, b, dims):
    return lax.dot_general(a.astype(BF16), b.astype(BF16), (dims, ((), ())), preferred_element_type=F32)


def _split_bf16(x, parts):
    out = []
    r = x
    for i in range(parts):
        p = r.astype(BF16)
        out.append(p)
        if i + 1 < parts:
            r = r - p.astype(F32)
    return out


def _dot_exact_lhs(t, x, parts):
    acc = None
    for p in _split_bf16(x, parts):
        d = lax.dot_general(t, p, (NN, ((), ())), preferred_element_type=F32)
        acc = d if acc is None else acc + d
    return acc


def _params(**kw):
    return pltpu.CompilerParams(vmem_limit_bytes=VMEM_LIMIT, **kw)


def _row_block(m, want):
    t = min(want, m)
    assert m % t == 0
    return t


def _mm_nn(a, w, add=None, *, name):
    m, k = a.shape
    nb, _, nc = w.shape
    tm = _row_block(m, 512)

    def body(*refs):
        rows = pl.ds(pl.multiple_of(pl.program_id(1) * tm, tm), tm)
        if add is None:
            a_ref, w_ref, o_ref = refs
            o_ref[...] = _dot(a_ref[rows, :], w_ref[...], NN)
        else:
            a_ref, w_ref, add_ref, o_ref = refs
            o_ref[...] = _dot(a_ref[rows, :], w_ref[...], NN) + add_ref[...]

    in_specs = [pl.BlockSpec((m, k), lambda j, i: (0, 0)), pl.BlockSpec((None, k, nc), lambda j, i: (j, 0, 0))]
    args = [a, w]
    if add is not None:
        in_specs.append(pl.BlockSpec((tm, nc), lambda j, i: (i, j)))
        args.append(add)
    return pl.pallas_call(
        body, name=name, grid=(nb, m // tm), in_specs=in_specs,
        out_specs=pl.BlockSpec((tm, nc), lambda j, i: (i, j)),
        out_shape=jax.ShapeDtypeStruct((m, nb * nc), F32), compiler_params=_params(),
    )(*args)


def _mm_nt(pieces, w, *, name, exchange=None):
    m = pieces[0].shape[0]
    nb, k, nc = w.shape
    assert len(pieces) == nb
    tm = _row_block(m, 512)
    steps = m // tm
    ex = None if exchange is None else _Exchange(*exchange)

    def body(*refs):
        first = jnp.logical_and(pl.program_id(0) == 0, pl.program_id(1) == 0)
        last = jnp.logical_and(pl.program_id(0) == steps - 1, pl.program_id(1) == nb - 1)
        own, finish = _ride_along(ex, refs, nb + 1, 1, first, last)
        g_refs, w_ref, o_ref = own[:nb], own[nb], own[nb + 1]
        j = pl.program_id(1)
        for p in range(nb):
            @pl.when(j == p)
            def _(p=p):
                r = _dot(g_refs[p][...], w_ref[p], NT)
                if p == 0:
                    o_ref[...] = r
                else:
                    o_ref[...] += r

        finish()

    in_specs = [pl.BlockSpec((tm, nc), lambda i, j: (i, 0)) for _ in range(nb)]
    in_specs.append(pl.BlockSpec((nb, k, nc), lambda i, j: (0, 0, 0)))
    out_specs, out_shape = [pl.BlockSpec((tm, k), lambda i, j: (i, 0))], [jax.ShapeDtypeStruct((m, k), F32)]
    args, scratch = list(pieces) + [w], []
    if ex is not None:
        in_specs, out_specs, out_shape = in_specs + ex.in_specs, out_specs + ex.out_specs, out_shape + ex.out_shape
        args, scratch = args + list(exchange[0]), ex.scratch_shapes
    out = pl.pallas_call(
        body, name=name, grid=(steps, nb), in_specs=in_specs, out_specs=out_specs, out_shape=out_shape,
        scratch_shapes=scratch, compiler_params=_params(),
    )(*args)
    return out[0], out[1:]


def _mm_tn(a, pieces, *, name):
    m, k = a.shape
    nb, nc = len(pieces), pieces[0].shape[1]
    tm = _row_block(m, 512)
    steps = m // tm

    def body(*refs):
        a_ref, g_refs, o_ref, acc_ref = refs[0], refs[1:1 + nb], refs[1 + nb], refs[2 + nb]
        j, i = pl.program_id(0), pl.program_id(1)
        rows = pl.ds(pl.multiple_of(i * tm, tm), tm)
        for p in range(nb):
            @pl.when(j == p)
            def _(p=p):
                r = _dot(a_ref[rows, :], g_refs[p][...], TN)

                @pl.when(i == 0)
                def _():
                    acc_ref[...] = r

                @pl.when(i != 0)
                def _():
                    acc_ref[...] += r

        @pl.when(i == steps - 1)
        def _():
            o_ref[...] = acc_ref[...].astype(BF16)

    def piece_spec(p):
        return pl.BlockSpec((tm, nc), lambda j, i: (jnp.where(j == p, i, 0), 0))

    return pl.pallas_call(
        body, name=name, grid=(nb, steps),
        in_specs=[pl.BlockSpec((m, k), lambda j, i: (0, 0))] + [piece_spec(p) for p in range(nb)],
        out_specs=pl.BlockSpec((None, k, nc), lambda j, i: (j, 0, 0)),
        out_shape=jax.ShapeDtypeStruct((nb, k, nc), BF16), scratch_shapes=[pltpu.VMEM((k, nc), F32)],
        compiler_params=_params(),
    )(a, *pieces)


def _rms(x, g):
    return x * lax.rsqrt(jnp.mean(x * x, axis=-1, keepdims=True) + EPS) * g


def _rms_bwd(x, g, dy):
    r = lax.rsqrt(jnp.mean(x * x, axis=-1, keepdims=True) + EPS)
    xh = x * r
    dxh = dy * g
    dx = r * (dxh - xh * jnp.mean(dxh * xh, axis=-1, keepdims=True))
    return dx, jnp.sum(dy * xh, axis=0, keepdims=True)


def _accumulate(ref, val):
    @pl.when(pl.program_id(0) == 0)
    def _():
        ref[...] = val

    @pl.when(pl.program_id(0) != 0)
    def _():
        ref[...] += val


def _norm_fwd(x, g, *, name):
    m, d = x.shape
    tm = _row_block(m, 512)

    def body(x_ref, g_ref, o_ref):
        o_ref[...] = _rms(x_ref[...], g_ref[...]).astype(BF16)

    return pl.pallas_call(
        body, name=name, grid=(m // tm,),
        in_specs=[pl.BlockSpec((tm, d), lambda i: (i, 0)), pl.BlockSpec((1, d), lambda i: (0, 0))],
        out_specs=pl.BlockSpec((tm, d), lambda i: (i, 0)),
        out_shape=jax.ShapeDtypeStruct((m, d), BF16), compiler_params=_params(),
    )(x, g)


def _norm_bwd(x, g, dy, add, *, name):
    m, d = x.shape
    tm = _row_block(m, 512)

    def body(x_ref, g_ref, dy_ref, add_ref, dx_ref, dg_ref):
        dx, dg = _rms_bwd(x_ref[...], g_ref[...], dy_ref[...])
        dx_ref[...] = dx + add_ref[...]
        _accumulate(dg_ref, dg)

    row = pl.BlockSpec((tm, d), lambda i: (i, 0))
    vec = pl.BlockSpec((1, d), lambda i: (0, 0))
    return pl.pallas_call(
        body, name=name, grid=(m // tm,), in_specs=[row, vec, row, row], out_specs=[row, vec],
        out_shape=[jax.ShapeDtypeStruct((m, d), F32), jax.ShapeDtypeStruct((1, d), F32)], compiler_params=_params(),
    )(x, g, dy, add)


def _silu(x):
    return x * jax.nn.sigmoid(x)


def _mix_block(oa, ob, ag, bg, ga, gb):
    parts = []
    for h in range(A_HEADS):
        o = oa[:, HEAD_A * h:HEAD_A * (h + 1)]
        parts.append(o * lax.rsqrt(jnp.mean(o * o, axis=-1, keepdims=True) + EPS))
    ya = jnp.concatenate(parts, axis=1) * ga * _silu(ag)
    low = lax.broadcasted_iota(jnp.int32, (1, LANES), 1) < HEAD_B
    parts = []
    for p in range(B_PAIRS):
        o = ob[:, LANES * p:LANES * (p + 1)]
        sq = o * o
        s_lo = jnp.sum(jnp.where(low, sq, 0.0), axis=-1, keepdims=True)
        s_hi = jnp.sum(jnp.where(low, 0.0, sq), axis=-1, keepdims=True)
        r = jnp.where(low, lax.rsqrt(s_lo * (1.0 / HEAD_B) + EPS), lax.rsqrt(s_hi * (1.0 / HEAD_B) + EPS))
        parts.append(o * r)
    yb = jnp.concatenate(parts, axis=1) * gb * _silu(bg)
    return jnp.concatenate([ya, yb], axis=1)


_GATE_A_BLOCK = 3
_GATE_B_BLOCK = 7


def _mix_fwd(oa, ob, proj, ga, gb, *, name):
    m, w = oa.shape
    tm = _row_block(m, 512)

    def body(oa_ref, ob_ref, ag_ref, bg_ref, ga_ref, gb_ref, y_ref):
        y_ref[...] = _mix_block(oa_ref[...], ob_ref[...], ag_ref[...], bg_ref[...], ga_ref[...], gb_ref[...]).astype(BF16)

    row = pl.BlockSpec((tm, w), lambda i: (i, 0))
    vec = pl.BlockSpec((1, w), lambda i: (0, 0))
    return pl.pallas_call(
        body, name=name, grid=(m // tm,),
        in_specs=[row, row, pl.BlockSpec((tm, w), lambda i: (i, _GATE_A_BLOCK)),
                  pl.BlockSpec((tm, w), lambda i: (i, _GATE_B_BLOCK)), vec, vec],
        out_specs=pl.BlockSpec((tm, 2 * w), lambda i: (i, 0)),
        out_shape=jax.ShapeDtypeStruct((m, 2 * w), BF16), compiler_params=_params(),
    )(oa, ob, proj, proj, ga, gb)


def _mix_bwd(dy, oa, ob, proj, ga, gb, *, name):
    m, w = oa.shape
    tm = _row_block(m, 256)

    def body(dy_ref, oa_ref, ob_ref, ag_ref, bg_ref, ga_ref, gb_ref, doa_ref, dob_ref, dag_ref, dbg_ref, dga_ref, dgb_ref):
        _, vjp = jax.vjp(_mix_block, oa_ref[...], ob_ref[...], ag_ref[...], bg_ref[...], ga_ref[...], gb_ref[...])
        doa, dob, dag, dbg, dga, dgb = vjp(dy_ref[...])
        doa_ref[...] = doa
        dob_ref[...] = dob.astype(BF16)
        dag_ref[...] = dag.astype(BF16)
        dbg_ref[...] = dbg.astype(BF16)
        _accumulate(dga_ref, dga)
        _accumulate(dgb_ref, dgb)

    row = pl.BlockSpec((tm, w), lambda i: (i, 0))
    vec = pl.BlockSpec((1, w), lambda i: (0, 0))
    big = jax.ShapeDtypeStruct((m, w), F32)
    half = jax.ShapeDtypeStruct((m, w), BF16)
    small = jax.ShapeDtypeStruct((1, w), F32)
    return pl.pallas_call(
        body, name=name, grid=(m // tm,),
        in_specs=[pl.BlockSpec((tm, 2 * w), lambda i: (i, 0)), row, row,
                  pl.BlockSpec((tm, w), lambda i: (i, _GATE_A_BLOCK)), pl.BlockSpec((tm, w), lambda i: (i, _GATE_B_BLOCK)), vec, vec],
        out_specs=[row, row, row, row, vec, vec], out_shape=[big, half, half, half, small, small], compiler_params=_params(),
    )(dy, oa, ob, proj, proj, ga, gb)


def _ple_block(h1, gp, pp, gpost):
    return h1 + jax.nn.sigmoid(gp) * _rms(pp, gpost)


def _ple_fwd(h1, gp, pp, gpost, *, name):
    m, d = h1.shape
    tm = _row_block(m, 512)

    def body(h_ref, gp_ref, pp_ref, g_ref, o_ref):
        o_ref[...] = _ple_block(h_ref[...], gp_ref[...], pp_ref[...], g_ref[...])

    row = pl.BlockSpec((tm, d), lambda i: (i, 0))
    vec = pl.BlockSpec((1, d), lambda i: (0, 0))
    return pl.pallas_call(
        body, name=name, grid=(m // tm,), in_specs=[row, row, row, vec], out_specs=row,
        out_shape=jax.ShapeDtypeStruct((m, d), F32), compiler_params=_params(),
    )(h1, gp, pp, gpost)


def _ple_bwd(dh2, gp, pp, gpost, *, name):
    m, d = dh2.shape
    tm = _row_block(m, 512)

    def body(dh_ref, gp_ref, pp_ref, g_ref, dgp_ref, dpp_ref, dg_ref):
        dh = dh_ref[...]
        gate = jax.nn.sigmoid(gp_ref[...])
        pe = _rms(pp_ref[...], g_ref[...])
        dgp_ref[...] = (dh * pe * (gate * (1.0 - gate))).astype(BF16)
        dpp, dg = _rms_bwd(pp_ref[...], g_ref[...], dh * gate)
        dpp_ref[...] = dpp.astype(BF16)
        _accumulate(dg_ref, dg)

    row = pl.BlockSpec((tm, d), lambda i: (i, 0))
    vec = pl.BlockSpec((1, d), lambda i: (0, 0))
    big = jax.ShapeDtypeStruct((m, d), BF16)
    return pl.pallas_call(
        body, name=name, grid=(m // tm,), in_specs=[row, row, row, vec], out_specs=[row, row, vec],
        out_shape=[big, big, jax.ShapeDtypeStruct((1, d), F32)], compiler_params=_params(),
    )(dh2, gp, pp, gpost)


def _loss_head(h, gf, target, *, name):
    m, d = h.shape
    tm = _row_block(m, 512)

    def body(h_ref, g_ref, t_ref, loss_ref, dh_ref, dg_ref):
        x = h_ref[...]
        g = g_ref[...]
        err = _rms(x, g) - t_ref[...]
        part = 0.5 * jnp.sum(jnp.mean(err * err, axis=-1, keepdims=True), axis=0, keepdims=True)
        dx, dg = _rms_bwd(x, g, err * (1.0 / d))
        dh_ref[...] = dx
        _accumulate(dg_ref, dg)
        _accumulate(loss_ref, jnp.broadcast_to(part, (8, LANES)))

    row = pl.BlockSpec((tm, d), lambda i: (i, 0))
    vec = pl.BlockSpec((1, d), lambda i: (0, 0))
    return pl.pallas_call(
        body, name=name, grid=(m // tm,), in_specs=[row, vec, row],
        out_specs=[pl.BlockSpec((8, LANES), lambda i: (0, 0)), row, vec],
        out_shape=[jax.ShapeDtypeStruct((8, LANES), F32), jax.ShapeDtypeStruct((m, d), F32), jax.ShapeDtypeStruct((1, d), F32)],
        compiler_params=_params(),
    )(h, gf, target)


def _hgrn_pre(aq, af, lbv):
    sq = jax.nn.sigmoid(aq)
    sg = jax.nn.sigmoid(-af)
    k = (1.0 - lbv) * sg
    return aq * sq, sq, sg, k, jnp.log1p(-k)


def _tri(n, cmp):
    r = lax.broadcasted_iota(jnp.int32, (n, n), 0)
    c = lax.broadcasted_iota(jnp.int32, (n, n), 1)
    return cmp(r, c).astype(BF16)


def _hgrn_intra(q, k, b, b_scr):
    col = lax.broadcasted_iota(jnp.int32, (SUB, CHUNK), 1)
    rows = [jnp.zeros((SUB, CHUNK), F32)]
    facs = [None]
    for i in range(1, CHUNK // SUB):
        r = b_scr[SUB * i - 1:SUB * i, :]
        fq = jnp.exp(b[SUB * i:SUB * (i + 1)] - r)
        fk = jnp.exp(jnp.minimum(r - b, 0.0))
        qs = q[SUB * i:SUB * (i + 1)] * fq
        ks = k * fk
        rows.append(jnp.where(col < SUB * i, _dot(qs, ks, NT), 0.0))
        facs.append((fq, fk, qs, ks))
    return jnp.concatenate(rows, axis=0), facs


def _hgrn_diag_weights(i, k, b, q_scr, b_scr):
    sub = lax.broadcasted_iota(jnp.int32, (SUB, LANES), 0)
    ki = k[SUB * i:SUB * (i + 1)]
    bi = b[SUB * i:SUB * (i + 1)]
    es, ws = [], []
    for tau in range(SUB):
        t = SUB * i + tau
        e = jnp.exp(jnp.where(sub <= tau, b_scr[t:t + 1, :] - bi, NEG_BIG))
        es.append(e)
        ws.append((q_scr[t:t + 1, :] * ki) * e)
    return es, jnp.concatenate(ws, axis=0)


def _hgrn_fwd(proj, lb_row, *, name):
    s = proj.shape[0]
    tb = _row_block(s, 512)
    nch = tb // CHUNK
    wide = HGRN_HEADS_PER_STEP * HEAD_A

    def body(aq_all, af_all, ai_all, lb_all, o_all, st_all, s_all, q_all, b_all, od_all):
        @pl.when(pl.program_id(1) == 0)
        def _():
            s_all[...] = jnp.zeros_like(s_all)

        tril = _tri(CHUNK, lambda r, c: r >= c)
        ones = jnp.ones((LANES, LANES), BF16)

        def head_chunk(hh, ci):
            lanes = slice(HEAD_A * hh, HEAD_A * (hh + 1))
            aq_ref, af_ref, ai_ref, o_ref = aq_all.at[:, lanes], af_all.at[:, lanes], ai_all.at[:, lanes], o_all.at[:, lanes]
            st_ref, s_scr, q_scr, b_scr, od_scr = st_all.at[hh], s_all.at[hh], q_all.at[hh], b_all.at[hh], od_all.at[hh]
            lbv = lb_all[:, lanes]
            r0 = pl.multiple_of(ci * CHUNK, CHUNK)
            q, _, _, k, g = _hgrn_pre(aq_ref[pl.ds(r0, CHUNK), :], af_ref[pl.ds(r0, CHUNK), :], lbv)
            v = ai_ref[pl.ds(r0, CHUNK), :]
            st0 = s_scr[...]
            st_ref[ci] = st0
            b = _dot_exact_lhs(tril, g, 3)
            b_scr[...] = b
            q_scr[...] = q
            bl = b_scr[CHUNK - 1:CHUNK, :]
            o = _dot(q * jnp.exp(b), st0, NT)
            a, _ = _hgrn_intra(q, k, b, b_scr)
            o = o + _dot(a, v, NN)
            for i in range(CHUNK // SUB):
                _, wst = _hgrn_diag_weights(i, k, b, q_scr, b_scr)
                rep = _dot(wst, ones, NN)
                vi = v[SUB * i:SUB * (i + 1)]
                for tau in range(SUB):
                    t = SUB * i + tau
                    od_scr[t:t + 1, :] = jnp.sum(rep[SUB * tau:SUB * (tau + 1)] * vi, axis=0, keepdims=True)
            o_ref[pl.ds(r0, CHUNK), :] = o + od_scr[...]
            s_scr[...] = st0 * jnp.exp(bl) + _dot(v, k * jnp.exp(bl - b), TN)

        def chunk(ci, carry):
            for hh in range(HGRN_HEADS_PER_STEP):
                head_chunk(hh, ci)
            return carry

        lax.fori_loop(0, nch, chunk, 0)

    def col(block0):
        return pl.BlockSpec((tb, wide), lambda h, t: (t, block0 + h))

    steps = A_HEADS // HGRN_HEADS_PER_STEP
    per_head = pltpu.VMEM((HGRN_HEADS_PER_STEP, CHUNK, HEAD_A), F32)
    return pl.pallas_call(
        body, name=name, grid=(steps, s // tb),
        in_specs=[col(0), col(steps), col(2 * steps), pl.BlockSpec((1, wide), lambda h, t: (0, h))],
        out_specs=[pl.BlockSpec((tb, wide), lambda h, t: (t, h)),
                   pl.BlockSpec((HGRN_HEADS_PER_STEP, nch, HEAD_A, HEAD_A), lambda h, t: (h, t, 0, 0))],
        out_shape=[jax.ShapeDtypeStruct((s, A_HEADS * HEAD_A), F32),
                   jax.ShapeDtypeStruct((A_HEADS, s // CHUNK, HEAD_A, HEAD_A), F32)],
        scratch_shapes=[pltpu.VMEM((HGRN_HEADS_PER_STEP, HEAD_A, HEAD_A), F32), per_head, per_head, per_head],
        compiler_params=_params(),
    )(proj, proj, proj, lb_row)


def _hgrn_bwd(proj, lb_row, states, do_a, *, name):
    s = proj.shape[0]
    tb = _row_block(s, 512)
    nch = tb // CHUNK
    nblk = s // tb

    wide = HGRN_HEADS_PER_STEP * HEAD_A

    def body(aq_all, af_all, ai_all, lb_all, st_all, do_all, daq_all, daf_all, dai_all, dlb_all,
             ds_all, q_all, b_all, dq_all, dos_all):
        @pl.when(pl.program_id(1) == 0)
        def _():
            ds_all[...] = jnp.zeros_like(ds_all)
            dlb_all[...] = jnp.zeros_like(dlb_all)

        tril = _tri(CHUNK, lambda r, c: r >= c)
        triu = _tri(CHUNK, lambda r, c: c >= r)
        ones = jnp.ones((LANES, LANES), BF16)
        col = lax.broadcasted_iota(jnp.int32, (SUB, CHUNK), 1)
        row = lax.broadcasted_iota(jnp.int32, (CHUNK, LANES), 0)
        nsub = CHUNK // SUB

        def head_chunk(hh, ci):
            lanes = slice(HEAD_A * hh, HEAD_A * (hh + 1))
            aq_ref, af_ref, ai_ref, do_ref = aq_all.at[:, lanes], af_all.at[:, lanes], ai_all.at[:, lanes], do_all.at[:, lanes]
            daq_ref, daf_ref, dai_ref, dlb_ref = daq_all.at[:, lanes], daf_all.at[:, lanes], dai_all.at[:, lanes], dlb_all.at[:, lanes]
            st_ref, ds_scr, q_scr, b_scr = st_all.at[hh], ds_all.at[hh], q_all.at[hh], b_all.at[hh]
            dq_scr, do_scr = dq_all.at[hh], dos_all.at[hh]
            lbv = lb_all[:, lanes]
            r0 = pl.multiple_of(ci * CHUNK, CHUNK)
            aq = aq_ref[pl.ds(r0, CHUNK), :]
            q, sq, sg, k, g = _hgrn_pre(aq, af_ref[pl.ds(r0, CHUNK), :], lbv)
            v = ai_ref[pl.ds(r0, CHUNK), :]
            do = do_ref[pl.ds(r0, CHUNK), :]
            st0 = st_ref[ci]
            dst1 = ds_scr[...]
            b = _dot_exact_lhs(tril, g, 3)
            b_scr[...] = b
            q_scr[...] = q
            do_scr[...] = do
            bl = b_scr[CHUNK - 1:CHUNK, :]
            eb = jnp.exp(b)
            ebl = jnp.exp(bl)
            ekd = jnp.exp(bl - b)
            kd = k * ekd
            a, facs = _hgrn_intra(q, k, b, b_scr)

            da = _dot(do, v, NT)
            dv = _dot(a, do, TN) + _dot(kd, dst1, NT)
            dq = _dot(do, st0, NN) * eb
            dk_inter = _dot(v, dst1, NN) * ekd
            dk = dk_inter
            dq_rows = [jnp.zeros((SUB, LANES), F32)]
            for i in range(1, nsub):
                fq, fk, qs, ks = facs[i]
                da_i = jnp.where(col < SUB * i, da[SUB * i:SUB * (i + 1)], 0.0)
                dq_rows.append(_dot(da_i, ks, NN) * fq)
                dk = dk + _dot(da_i, qs, TN) * fk
            dq = dq + jnp.concatenate(dq_rows, axis=0)

            dv_rows, dk_rows = [], []
            for i in range(nsub):
                es, wst = _hgrn_diag_weights(i, k, b, q_scr, b_scr)
                ki = k[SUB * i:SUB * (i + 1)]
                vi = v[SUB * i:SUB * (i + 1)]
                rep = _dot(wst, ones, NN)
                mst = jnp.concatenate([do_scr[SUB * i + tau:SUB * i + tau + 1, :] * vi for tau in range(SUB)], axis=0)
                drep = _dot(mst, ones, NN)
                dvi = jnp.zeros((SUB, LANES), F32)
                dki = jnp.zeros((SUB, LANES), F32)
                for tau in range(SUB):
                    t = SUB * i + tau
                    gt = drep[SUB * tau:SUB * (tau + 1)] * es[tau]
                    dvi = dvi + rep[SUB * tau:SUB * (tau + 1)] * do_scr[t:t + 1, :]
                    dki = dki + gt * q_scr[t:t + 1, :]
                    dq_scr[t:t + 1, :] = jnp.sum(gt * ki, axis=0, keepdims=True)
                dv_rows.append(dvi)
                dk_rows.append(dki)
            dv = dv + jnp.concatenate(dv_rows, axis=0)
            dk = dk + jnp.concatenate(dk_rows, axis=0)
            dq = dq + dq_scr[...]

            ds_scr[...] = dst1 * ebl + _dot(do, q * eb, TN)
            last = jnp.sum(k * dk_inter, axis=0, keepdims=True) + ebl * jnp.sum(st0 * dst1, axis=0, keepdims=True)
            db = q * dq - k * dk + jnp.where(row == CHUNK - 1, last, 0.0)
            dg = _dot_exact_lhs(triu, db, 3)

            dkt = dk - dg / (1.0 - k)
            daq_ref[pl.ds(r0, CHUNK), :] = (dq * (sq * (1.0 + aq * (1.0 - sq)))).astype(BF16)
            daf_ref[pl.ds(r0, CHUNK), :] = (dkt * (1.0 - lbv) * (-(sg * (1.0 - sg)))).astype(BF16)
            dai_ref[pl.ds(r0, CHUNK), :] = dv.astype(BF16)
            dlb_ref[...] += jnp.sum(dkt * (-sg), axis=0, keepdims=True)

        def chunk(cj, carry):
            for hh in range(HGRN_HEADS_PER_STEP):
                head_chunk(hh, nch - 1 - cj)
            return carry

        lax.fori_loop(0, nch, chunk, 0)

    def col_in(block0):
        return pl.BlockSpec((tb, wide), lambda h, t: (nblk - 1 - t, block0 + h))

    steps = A_HEADS // HGRN_HEADS_PER_STEP
    head_col = pl.BlockSpec((tb, wide), lambda h, t: (nblk - 1 - t, h))
    vec = pl.BlockSpec((1, wide), lambda h, t: (0, h))
    half = jax.ShapeDtypeStruct((s, A_HEADS * HEAD_A), BF16)
    per_head = pltpu.VMEM((HGRN_HEADS_PER_STEP, CHUNK, HEAD_A), F32)
    return pl.pallas_call(
        body, name=name, grid=(steps, nblk),
        in_specs=[col_in(0), col_in(steps), col_in(2 * steps), vec,
                  pl.BlockSpec((HGRN_HEADS_PER_STEP, nch, HEAD_A, HEAD_A), lambda h, t: (h, nblk - 1 - t, 0, 0)), head_col],
        out_specs=[head_col, head_col, head_col, vec],
        out_shape=[half, half, half, jax.ShapeDtypeStruct((1, A_HEADS * HEAD_A), F32)],
        scratch_shapes=[pltpu.VMEM((HGRN_HEADS_PER_STEP, HEAD_A, HEAD_A), F32), per_head, per_head, per_head, per_head],
        compiler_params=_params(),
    )(proj, proj, proj, lb_row, states, do_a)


_BQ_BLOCK = 16
_BK_BLOCK = 20
_BV_BLOCK = 24
SB_SCALE = HEAD_B ** -0.5
SB_TQ = 1024
SB_TK = 256


def _sb_blocks(s):
    tq = min(SB_TQ, s)
    tk = min(SB_TK, s)
    assert s % tq == 0 and tq % tk == 0 and s // tk <= LANES
    return tq, tk, (2 if (tq // tk) % 2 == 0 else 1)


def _sb2_weights(qh, kb, carry, incl, mask):
    z = _dot(qh, kb, NT)
    sp = jnp.maximum(z, 0.0) + jnp.log(1.0 + jnp.exp(-jnp.abs(z)))
    if mask is not None:
        sp = jnp.where(mask, sp, 0.0)
    suffix = _dot(sp, incl, NN)
    w = jnp.exp(z - suffix + carry)
    if mask is not None:
        w = jnp.where(mask, w, 0.0)
    return z, sp, w, suffix[:, 0:1]


def _sb_mask(qi, j, tq, tk, row0):
    row = qi * tq + row0 + lax.broadcasted_iota(jnp.int32, (tq - row0, tk), 0)
    col = j * tk + lax.broadcasted_iota(jnp.int32, (tq - row0, tk), 1)
    return col < row


def _below(x, row0, new_tail):
    return new_tail if row0 == 0 else jnp.concatenate([x[:row0], new_tail], axis=0)


def _ride_along(ex, refs, n_in, n_out, first, last):
    if ex is None:
        return refs, lambda: None
    own = refs[:n_in] + refs[n_in + ex.n:n_in + ex.n + n_out]
    xin = refs[n_in:n_in + ex.n]
    xout = refs[n_in + ex.n + n_out:n_in + 2 * ex.n + n_out]
    rest = refs[n_in + 2 * ex.n + n_out:]
    sems, scratch = rest[len(rest) - 3:], rest[:len(rest) - 3]

    @pl.when(first)
    def _():
        ex.start(xin, xout, sems)

    def finish():
        @pl.when(last)
        def _():
            ex.finish(xin, xout, sems)

    return own + scratch, finish


def _sb2_fwd(proj, *, name, exchange=None):
    s = proj.shape[0]
    tq, tk, unroll = _sb_blocks(s)
    nq, ratio = s // tq, tq // tk
    ex = None if exchange is None else _Exchange(*exchange)

    def body(*refs):
        first = jnp.logical_and(pl.program_id(0) == 0, pl.program_id(1) == 0)
        last = jnp.logical_and(pl.program_id(0) == B_PAIRS - 1, pl.program_id(1) == nq - 1)
        (q_ref, k_ref, v_ref, o_ref, c_ref), finish = _ride_along(ex, refs, 3, 2, first, last)
        qi = pl.program_id(1)
        lane = lax.broadcasted_iota(jnp.int32, (1, LANES), 1)
        low = lane < HEAD_B

        def heads(x):
            return jnp.where(low, x, 0.0).astype(BF16), jnp.where(low, 0.0, x).astype(BF16)

        qh = heads(q_ref[...] * SB_SCALE)
        incl = _tri(tk, lambda r, c: r >= c)
        o_ref[...] = jnp.zeros_like(o_ref)
        c_ref[...] = jnp.zeros_like(c_ref)

        def tile(j, c, masked, row0=0):
            s0 = pl.multiple_of(j * tk, tk)
            kb = k_ref[pl.ds(s0, tk), :].astype(BF16)
            vh = heads(v_ref[pl.ds(s0, tk), :])
            mask = _sb_mask(qi, j, tq, tk, row0) if masked else None
            c_new = []
            acc = jnp.zeros((tq - row0, LANES), F32)
            for h in range(2):
                _, _, w, rowsum = _sb2_weights(qh[h][row0:], kb, c[h][row0:], incl, mask)
                c_ref[h] = jnp.where(lane == j, c[h], c_ref[h])
                c_new.append(_below(c[h], row0, c[h][row0:] - rowsum))
                acc = acc + _dot(w, vh[h], NN)
            o_ref[row0:, :] += acc
            return tuple(c_new)

        zc = jnp.zeros((tq, 1), F32)
        c = (zc, zc)
        for r in reversed(range(ratio)):
            c = tile(qi * ratio + r, c, True, r * tk)
        def several(jj, c):
            for r in range(unroll):
                c = tile(qi * ratio - 1 - (jj * unroll + r), c, False)
            return c

        lax.fori_loop(0, qi * (ratio // unroll), several, c)
        finish()

    def whole(block0):
        return pl.BlockSpec((s, LANES), lambda p, i: (0, block0 + p))

    in_specs = [pl.BlockSpec((tq, LANES), lambda p, i: (i, _BQ_BLOCK + p)), whole(_BK_BLOCK), whole(_BV_BLOCK)]
    out_specs = [pl.BlockSpec((tq, LANES), lambda p, i: (i, p)), pl.BlockSpec((None, 2, tq, LANES), lambda p, i: (p, 0, i, 0))]
    out_shape = [jax.ShapeDtypeStruct((s, B_PAIRS * LANES), F32), jax.ShapeDtypeStruct((B_PAIRS, 2, s, LANES), F32)]
    args, scratch = [proj, proj, proj], []
    if ex is not None:
        in_specs, out_specs, out_shape = in_specs + ex.in_specs, out_specs + ex.out_specs, out_shape + ex.out_shape
        args, scratch = args + list(exchange[0]), ex.scratch_shapes
    out = pl.pallas_call(
        body, name=name, grid=(B_PAIRS, nq), in_specs=in_specs, out_specs=out_specs, out_shape=out_shape,
        scratch_shapes=scratch, compiler_params=_params(),
    )(*args)
    return out[0], out[1], out[2:]


def _sb2_bwd(proj, carries, do_b, *, name, exchange=None):
    s = proj.shape[0]
    tq, tk, unroll = _sb_blocks(s)
    nq, nk, ratio = s // tq, s // tk, tq // tk
    ex = None if exchange is None else _Exchange(*exchange)

    def body(*refs):
        first = jnp.logical_and(pl.program_id(0) == 0, pl.program_id(1) == 0)
        last = jnp.logical_and(pl.program_id(0) == B_PAIRS - 1, pl.program_id(1) == nq - 1)
        own, finish = _ride_along(ex, refs, 5, 3, first, last)
        q_ref, k_ref, v_ref, c_ref, do_ref, dq_ref, dk_ref, dv_ref, dkt_scr, dvt_scr = own
        qi = pl.program_id(1)

        @pl.when(qi == 0)
        def _():
            dkt_scr[...] = jnp.zeros_like(dkt_scr)
            dvt_scr[...] = jnp.zeros_like(dvt_scr)

        lane = lax.broadcasted_iota(jnp.int32, (1, LANES), 1)
        low = lane < HEAD_B
        low_row = lax.broadcasted_iota(jnp.int32, (LANES, 1), 0) < HEAD_B

        def heads(x):
            return jnp.where(low, x, 0.0).astype(BF16), jnp.where(low, 0.0, x).astype(BF16)

        def heads_t(x):
            xt = x.T
            return jnp.where(low_row, xt, 0.0).astype(BF16), jnp.where(low_row, 0.0, xt).astype(BF16)

        q = q_ref[...] * SB_SCALE
        do = do_ref[...].astype(F32)
        qh, qth = heads(q), heads_t(q)
        doh, doth = heads(do), heads_t(do)
        cst = (c_ref[0], c_ref[1])
        incl = _tri(tk, lambda r, c: r >= c)
        pre = _tri(tk, lambda r, c: r <= c)

        def tile(j, carry, masked, row0=0):
            ec, dq = carry
            s0 = pl.multiple_of(j * tk, tk)
            kf = k_ref[pl.ds(s0, tk), :]
            kb = kf.astype(BF16)
            kh = heads(kf)
            vb = v_ref[pl.ds(s0, tk), :].astype(BF16)
            mask = _sb_mask(qi, j, tq, tk, row0) if masked else None
            ec_new = []
            dkt = jnp.zeros((LANES, tk), F32)
            dvt = jnp.zeros((LANES, tk), F32)
            dq_add = jnp.zeros((tq - row0, LANES), F32)
            for h in range(2):
                cin = jnp.sum(jnp.where(lane == j, cst[h][row0:], 0.0), axis=1, keepdims=True)
                z, sp, w, _ = _sb2_weights(qh[h][row0:], kb, cin, incl, mask)
                e = w * _dot(doh[h][row0:], vb, NT)
                prefix = _dot(e, pre, NN)
                dz = e - jnp.exp(z - sp) * (ec[h][row0:] + prefix)
                if mask is not None:
                    dz = jnp.where(mask, dz, 0.0)
                ec_new.append(_below(ec[h], row0, ec[h][row0:] + prefix[:, tk - 1:tk]))
                dzb = dz.astype(BF16)
                wb = w.astype(BF16)
                dq_add = dq_add + _dot(dzb, kh[h], NN)
                dkt = dkt + _dot(qth[h][:, row0:], dzb, NN)
                dvt = dvt + _dot(doth[h][:, row0:], wb, NN)
            dkt_scr[j] += dkt
            dvt_scr[j] += dvt
            return tuple(ec_new), _below(dq, row0, dq[row0:] + dq_add)

        zc = jnp.zeros((tq, 1), F32)
        def several(jj, carry):
            for r in range(unroll):
                carry = tile(jj * unroll + r, carry, False)
            return carry

        carry = lax.fori_loop(0, qi * (ratio // unroll), several, ((zc, zc), jnp.zeros((tq, LANES), F32)))
        for r in range(ratio):
            carry = tile(qi * ratio + r, carry, True, r * tk)
        dq_ref[...] = (carry[1] * SB_SCALE).astype(BF16)

        @pl.when(qi == nq - 1)
        def _():
            for j in range(nk):
                dk_ref[tk * j:tk * (j + 1), :] = dkt_scr[j].T.astype(BF16)
                dv_ref[tk * j:tk * (j + 1), :] = dvt_scr[j].T.astype(BF16)

        finish()

    def whole_in(block0):
        return pl.BlockSpec((s, LANES), lambda p, i: (0, block0 + p))

    blk = pl.BlockSpec((tq, LANES), lambda p, i: (i, p))
    whole_out = pl.BlockSpec((s, LANES), lambda p, i: (0, p))
    big = jax.ShapeDtypeStruct((s, B_PAIRS * LANES), BF16)
    in_specs = [pl.BlockSpec((tq, LANES), lambda p, i: (i, _BQ_BLOCK + p)), whole_in(_BK_BLOCK), whole_in(_BV_BLOCK),
                pl.BlockSpec((None, 2, tq, LANES), lambda p, i: (p, 0, i, 0)), blk]
    out_specs, out_shape = [blk, whole_out, whole_out], [big, big, big]
    args = [proj, proj, proj, carries, do_b]
    scratch = [pltpu.VMEM((nk, LANES, tk), F32), pltpu.VMEM((nk, LANES, tk), F32)]
    if ex is not None:
        in_specs, out_specs, out_shape = in_specs + ex.in_specs, out_specs + ex.out_specs, out_shape + ex.out_shape
        args, scratch = args + list(exchange[0]), scratch + ex.scratch_shapes
    out = pl.pallas_call(
        body, name=name, grid=(B_PAIRS, nq), in_specs=in_specs, out_specs=out_specs, out_shape=out_shape,
        scratch_shapes=scratch, compiler_params=_params(),
    )(*args)
    return out[0], out[1], out[2], out[3:]


def _my_id():
    return 4 * lax.axis_index("x") + 2 * lax.axis_index("y") + lax.axis_index("c")


def _mesh_pos(d):
    return (d // 4, (d // 2) % 2, d % 2)


class _Exchange:
    def __init__(self, arrays, scatter):
        self.n = len(arrays)
        self.scatter = scatter
        self.in_specs = [pl.BlockSpec(memory_space=pl.ANY)] * self.n
        self.out_specs = [pl.BlockSpec(memory_space=pl.ANY)] * self.n
        self.out_shape = [
            jax.ShapeDtypeStruct(((N_DEV,) + a.shape[1:]) if scatter else ((N_DEV,) + a.shape), a.dtype) for a in arrays]
        self.scratch_shapes = [pltpu.SemaphoreType.DMA((N_DEV - 1, self.n)), pltpu.SemaphoreType.DMA((N_DEV - 1, self.n)),
                               pltpu.SemaphoreType.DMA((self.n,))]

    def _copies(self, ins, outs, sems):
        send_sems, recv_sems, local_sems = sems
        me = _my_id()

        def src(i, to):
            return ins[i].at[to] if self.scatter else ins[i]

        local = [pltpu.make_async_copy(src(i, me), outs[i].at[me], local_sems.at[i]) for i in range(self.n)]
        sends, recvs = [], []
        for step in range(1, N_DEV):
            to = (me + step) % N_DEV
            frm = (me + N_DEV - step) % N_DEV
            for i in range(self.n):
                sends.append(pltpu.make_async_remote_copy(
                    src_ref=src(i, to), dst_ref=outs[i].at[me],
                    send_sem=send_sems.at[step - 1, i], recv_sem=recv_sems.at[step - 1, i],
                    device_id=_mesh_pos(to), device_id_type=pl.DeviceIdType.MESH))
                recvs.append(pltpu.make_async_remote_copy(
                    src_ref=src(i, frm), dst_ref=outs[i].at[frm],
                    send_sem=send_sems.at[step - 1, i], recv_sem=recv_sems.at[step - 1, i],
                    device_id=_mesh_pos(frm), device_id_type=pl.DeviceIdType.MESH))
        return local, sends, recvs

    def start(self, ins, outs, sems):
        local, sends, _ = self._copies(ins, outs, sems)
        for cp in local + sends:
            cp.start()

    def finish(self, ins, outs, sems):
        local, sends, recvs = self._copies(ins, outs, sems)
        for cp in recvs:
            cp.wait_recv()
        for cp in sends:
            cp.wait_send()
        for cp in local:
            cp.wait()


def _exchange(arrays, scatter, *, name):
    ex = _Exchange(arrays, scatter)

    def body(*refs):
        ins, outs, sems = refs[:ex.n], refs[ex.n:2 * ex.n], refs[2 * ex.n:]
        ex.start(ins, outs, sems)
        ex.finish(ins, outs, sems)

    return pl.pallas_call(
        body, name=name, in_specs=ex.in_specs, out_specs=ex.out_specs, out_shape=ex.out_shape,
        scratch_shapes=ex.scratch_shapes,
    )(*arrays)


def _adamw_math(w, m, v, g):
    m2 = ADAM_B1 * m + (1.0 - ADAM_B1) * g
    v2 = ADAM_B2 * v + (1.0 - ADAM_B2) * (g * g)
    m_hat = m2 / (1.0 - ADAM_B1 ** ADAM_STEP)
    v_hat = v2 / (1.0 - ADAM_B2 ** ADAM_STEP)
    delta = -ADAM_LR * (m_hat / (jnp.sqrt(v_hat) + ADAM_EPS) + ADAM_WD * w)
    return delta, m2, v2


def _slot_sum(ref):
    g = ref[0].astype(F32)
    for d in range(1, N_DEV):
        g = g + ref[d].astype(F32)
    return g


def _adamw(w, m, v, slots, *, name):
    r, c = w.shape
    tr = _row_block(r, 256)

    def body(w_ref, m_ref, v_ref, s_ref, g_ref, d_ref, m2_ref, v2_ref):
        g = _slot_sum(s_ref)
        delta, m2, v2 = _adamw_math(w_ref[...], m_ref[...], v_ref[...], g)
        g_ref[...] = g
        d_ref[...] = delta
        m2_ref[...] = m2
        v2_ref[...] = v2

    row = pl.BlockSpec((tr, c), lambda i: (i, 0))
    shp = jax.ShapeDtypeStruct((r, c), F32)
    return pl.pallas_call(
        body, name=name, grid=(r // tr,),
        in_specs=[row, row, row, pl.BlockSpec((N_DEV, tr, c), lambda i: (0, i, 0))],
        out_specs=[row, row, row, row], out_shape=[shp, shp, shp, shp], compiler_params=_params(),
    )(w, m, v, slots)


def _layer_softmax(l_ref):
    l0 = l_ref[0:1, :]
    l1 = l_ref[1:2, :]
    mx = jnp.maximum(l0, l1)
    e0 = jnp.exp(l0 - mx)
    e1 = jnp.exp(l1 - mx)
    return e0 / (e0 + e1), e1 / (e0 + e1)


def _adamw_lb(w, m, v, slots, *, name):
    def body(w_ref, m_ref, v_ref, s_ref, g_ref, d_ref, m2_ref, v2_ref):
        sm0, sm1 = _layer_softmax(w_ref)
        dl1 = _slot_sum(s_ref) * (sm0 * sm1)
        for row, g in ((0, -dl1), (1, dl1)):
            sl = slice(row, row + 1)
            delta, m2, v2 = _adamw_math(w_ref[sl, :], m_ref[sl, :], v_ref[sl, :], g)
            g_ref[sl, :] = g
            d_ref[sl, :] = delta
            m2_ref[sl, :] = m2
            v2_ref[sl, :] = v2

    shp = jax.ShapeDtypeStruct(w.shape, F32)
    return pl.pallas_call(body, name=name, out_shape=[shp, shp, shp, shp])(w, m, v, slots)


def _lower_bound_1(lb_logits, *, name):
    def body(l_ref, o_ref):
        sm0, sm1 = _layer_softmax(l_ref)
        o_ref[...] = (sm0 + sm1) - sm0

    return pl.pallas_call(body, name=name, out_shape=jax.ShapeDtypeStruct((1, lb_logits.shape[1]), F32))(lb_logits)


_SMALL = ("norm_mix", "a_out_norm", "b_out_norm", "ple_gate_norm", "ple_post_norm")


def _pack_rows(vectors):
    offs, rows = [], 0
    for vec in vectors:
        offs.append(rows)
        rows += vec.shape[0] // LANES
    flat = jnp.concatenate([vec.reshape(-1, LANES) for vec in vectors], axis=0)
    pad = (-rows) % 8
    if pad:
        flat = jnp.concatenate([flat, jnp.zeros((pad, LANES), F32)], axis=0)
    return flat, offs


def kernel(x, p, norm_mix, w_in, a_out_norm, b_out_norm, w_out, lb_logits, ple_gate_norm, w_ple_gate, w_ple_proj, ple_post_norm, final_norm, loss_target, m_norm_mix, m_w_in, m_a_out_norm, m_b_out_norm, m_w_out, m_lb_logits, m_ple_gate_norm, m_w_ple_gate, m_w_ple_proj, m_ple_post_norm, m_final_norm, v_norm_mix, v_w_in, v_a_out_norm, v_b_out_norm, v_w_out, v_lb_logits, v_ple_gate_norm, v_w_ple_gate, v_w_ple_proj, v_ple_post_norm, v_final_norm):
    depth = w_in.shape[0]
    assert depth == 2
    s, d = x.shape[1], x.shape[2]
    h = x.reshape(s, d)
    target = loss_target.reshape(s, d)

    (win0,) = _exchange([w_in[0].astype(BF16)], scatter=False, name="gather_w_in0")
    later = [w_in[1].astype(BF16), w_out.astype(BF16), w_ple_gate.astype(BF16), w_ple_proj.astype(BF16)]
    lb1 = _lower_bound_1(lb_logits, name="lower_bound")
    lbs = [jnp.zeros_like(lb1), lb1]

    saved = []
    for l in range(depth):
        win = win0 if l == 0 else win1
        u = _norm_fwd(h, norm_mix[l:l + 1], name="norm_mix")
        proj = _mm_nn(u, win, name="mm_in")
        o_a, states = _hgrn_fwd(proj, lbs[l], name="hgrn_fwd")
        if l == 0:
            o_b, carries, (win1, wout_g, wpg_g, wpp_g) = _sb2_fwd(proj, name="sb_fwd_gather", exchange=(later, False))
        else:
            o_b, carries, _ = _sb2_fwd(proj, name="sb_fwd")
        wout = wout_g[:, l].reshape(1, d, d)
        wpg = wpg_g[:, l].reshape(1, d, d)
        wpp = jnp.transpose(wpp_g[:, l], (1, 0, 2)).reshape(1, p.shape[-1], d)
        y = _mix_fwd(o_a, o_b, proj, a_out_norm[l:l + 1], b_out_norm[l:l + 1], name="mix_fwd")
        h1 = _mm_nn(y, wout, add=h, name="mm_out")
        r1 = _norm_fwd(h1, ple_gate_norm[l:l + 1], name="norm_gate")
        gp = _mm_nn(r1, wpg, name="mm_gate")
        pl_in = p[l].reshape(s, p.shape[-1])
        pp = _mm_nn(pl_in, wpp, name="mm_ple")
        h2 = _ple_fwd(h1, gp, pp, ple_post_norm[l:l + 1], name="ple_fwd")
        saved.append((h, u, proj, o_a, states, o_b, carries, y, h1, r1, gp, pp, pl_in, win, wout, wpg, wpp))
        h = h2

    loss_part, dh, dg_final = _loss_head(h, final_norm.reshape(1, d), target, name="loss_head")

    big = ("w_in", "w_out", "w_ple_gate", "w_ple_proj")
    grads = [None] * depth
    for l in reversed(range(depth)):
        h0, u, proj, o_a, states, o_b, carries, y, h1, r1, gp, pp, pl_in, win, wout, wpg, wpp = saved[l]
        dgp, dpp, dg_post = _ple_bwd(dh, gp, pp, ple_post_norm[l:l + 1], name="ple_bwd")
        dwpp = _mm_tn(pl_in, [dpp], name="mm_ple_dw")
        dwpp = jnp.transpose(dwpp.reshape(p.shape[-1], N_DEV, d // N_DEV), (1, 0, 2))
        dwpg = _mm_tn(r1, [dgp], name="mm_gate_dw").reshape(N_DEV, d // N_DEV, d)
        dr1, _ = _mm_nt([dgp], wpg, name="mm_gate_dx")
        dh1, dg_pg = _norm_bwd(h1, ple_gate_norm[l:l + 1], dr1, dh, name="norm_gate_bwd")
        dwout = _mm_tn(y, [dh1], name="mm_out_dw").reshape(N_DEV, d // N_DEV, d)
        dy, _ = _mm_nt([dh1], wout, name="mm_out_dx")
        do_a, do_b, dag, dbg, dg_a, dg_b = _mix_bwd(dy, o_a, o_b, proj, a_out_norm[l:l + 1], b_out_norm[l:l + 1], name="mix_bwd")
        if l == 0:
            ready = [grads[1][n] for n in big] + [dwout, dwpg, dwpp]
            dbq, dbk, dbv, early_slots = _sb2_bwd(proj, carries, do_b, name="sb_bwd_scatter", exchange=(ready, True))
        else:
            dbq, dbk, dbv, _ = _sb2_bwd(proj, carries, do_b, name="sb_bwd")
        daq, daf, dai, dlb = _hgrn_bwd(proj, lbs[l], states, do_a, name="hgrn_bwd")
        dproj = [daq, daf, dai, dag, dbq, dbk, dbv, dbg]
        dwin = _mm_tn(u, dproj, name="mm_in_dw")
        if l == 0:
            du, (win0_slots,) = _mm_nt(dproj, win, name="mm_in_dx_scatter", exchange=([dwin], True))
        else:
            du, _ = _mm_nt(dproj, win, name="mm_in_dx")
        dh, dg_mix = _norm_bwd(h0, norm_mix[l:l + 1], du, dh1, name="norm_mix_bwd")
        grads[l] = dict(w_in=dwin, w_out=dwout, w_ple_gate=dwpg, w_ple_proj=dwpp, norm_mix=dg_mix, a_out_norm=dg_a,
                        b_out_norm=dg_b, ple_gate_norm=dg_pg, ple_post_norm=dg_post, lb=dlb)

    grad_x = dh.reshape(x.shape)
    loss = lax.psum(loss_part[0, 0], ("x", "y", "c"))

    slots = {("w_in", 0): win0_slots}
    for k, key in enumerate([(n, 1) for n in big] + [(n, 0) for n in big[1:]]):
        slots[key] = early_slots[k]
    results = {}
    for n, w, m, v in zip(big, (w_in, w_out, w_ple_gate, w_ple_proj), (m_w_in, m_w_out, m_w_ple_gate, m_w_ple_proj),
                          (v_w_in, v_w_out, v_w_ple_gate, v_w_ple_proj)):
        per_layer = [_adamw(w[l], m[l], v[l], slots[(n, l)], name="adamw_" + n) for l in range(depth)]
        results[n] = [jnp.stack([per_layer[l][k] for l in range(depth)]) for k in range(4)]

    small_w = dict(norm_mix=norm_mix, a_out_norm=a_out_norm, b_out_norm=b_out_norm, ple_gate_norm=ple_gate_norm,
                   ple_post_norm=ple_post_norm)
    small_m = dict(norm_mix=m_norm_mix, a_out_norm=m_a_out_norm, b_out_norm=m_b_out_norm, ple_gate_norm=m_ple_gate_norm,
                   ple_post_norm=m_ple_post_norm)
    small_v = dict(norm_mix=v_norm_mix, a_out_norm=v_a_out_norm, b_out_norm=v_b_out_norm, ple_gate_norm=v_ple_gate_norm,
                   ple_post_norm=v_ple_post_norm)
    g_vecs = [jnp.concatenate([grads[l][n].reshape(-1) for l in range(depth)]) for n in _SMALL] + [dg_final.reshape(-1)]
    g_pack, offs = _pack_rows(g_vecs)
    w_pack, _ = _pack_rows([small_w[n].reshape(-1) for n in _SMALL] + [final_norm])
    m_pack, _ = _pack_rows([small_m[n].reshape(-1) for n in _SMALL] + [m_final_norm])
    v_pack, _ = _pack_rows([small_v[n].reshape(-1) for n in _SMALL] + [v_final_norm])
    small_slots, lb_slots = _exchange([g_pack, grads[1]["lb"]], scatter=False, name="gather_small_grads")
    packed = _adamw(w_pack, m_pack, v_pack, small_slots, name="adamw_small")
    results["lb_logits"] = _adamw_lb(lb_logits, m_lb_logits, v_lb_logits, lb_slots, name="adamw_lb")
    shapes = [small_w[n].shape for n in _SMALL] + [final_norm.shape]
    for n, off, shp in zip(_SMALL + ("final_norm",), offs, shapes):
        rows = 1
        for dim in shp:
            rows *= dim
        rows //= LANES
        results[n] = [o[off:off + rows].reshape(shp) for o in packed]

    order = ("norm_mix", "w_in", "a_out_norm", "b_out_norm", "w_out", "lb_logits", "ple_gate_norm", "w_ple_gate",
             "w_ple_proj", "ple_post_norm", "final_norm")
    out = [loss, grad_x]
    for k in range(4):
        out += [results[n][k] for n in order]
    return tuple(out)
```

```python
import jax
import jax.numpy as jnp
from jax import lax
from jax.experimental import pallas as pl
from jax.experimental.pallas import tpu as pltpu

F32 = jnp.float32
BF16 = jnp.bfloat16

N_DEV = 8
EPS = 1e-6
A_HEADS = 4
HEAD_A = 128
B_PAIRS = 4
HEAD_B = 64
CHUNK = 64
SUB = 16
HGRN_HEADS_PER_STEP = 4
LANES = 128
NEG_BIG = -1e30

ADAM_LR = 0.001
ADAM_B1 = 0.9
ADAM_B2 = 0.999
ADAM_EPS = 1e-08
ADAM_WD = 0.01
ADAM_STEP = 10

VMEM_LIMIT = 56 * 1024 * 1024

NN = ((1,), (0,))
NT = ((1,), (1,))
TN = ((0,), (0,))


def _dot(a, b, dims):
    return lax.dot_general(a.astype(BF16), b.astype(BF16), (dims, ((), ())), preferred_element_type=F32)


def _split_bf16(x, parts):
    out = []
    r = x
    for i in range(parts):
        p = r.astype(BF16)
        out.append(p)
        if i + 1 < parts:
            r = r - p.astype(F32)
    return out


def _dot_exact_lhs(t, x, parts):
    acc = None
    for p in _split_bf16(x, parts):
        d = lax.dot_general(t, p, (NN, ((), ())), preferred_element_type=F32)
        acc = d if acc is None else acc + d
    return acc


def _params(**kw):
    return pltpu.CompilerParams(vmem_limit_bytes=VMEM_LIMIT, **kw)


def _row_block(m, want):
    t = min(want, m)
    assert m % t == 0
    return t


def _mm_nn(a, w, add=None, *, name):
    m, k = a.shape
    nb, _, nc = w.shape
    tm = _row_block(m, 512)

    def body(*refs):
        rows = pl.ds(pl.multiple_of(pl.program_id(1) * tm, tm), tm)
        if add is None:
            a_ref, w_ref, o_ref = refs
            o_ref[...] = _dot(a_ref[rows, :], w_ref[...], NN)
        else:
            a_ref, w_ref, add_ref, o_ref = refs
            o_ref[...] = _dot(a_ref[rows, :], w_ref[...], NN) + add_ref[...]

    in_specs = [pl.BlockSpec((m, k), lambda j, i: (0, 0)), pl.BlockSpec((None, k, nc), lambda j, i: (j, 0, 0))]
    args = [a, w]
    if add is not None:
        in_specs.append(pl.BlockSpec((tm, nc), lambda j, i: (i, j)))
        args.append(add)
    return pl.pallas_call(
        body, name=name, grid=(nb, m // tm), in_specs=in_specs,
        out_specs=pl.BlockSpec((tm, nc), lambda j, i: (i, j)),
        out_shape=jax.ShapeDtypeStruct((m, nb * nc), F32), compiler_params=_params(),
    )(*args)


def _mm_nt(pieces, w, *, name, exchange=None):
    m = pieces[0].shape[0]
    nb, k, nc = w.shape
    assert len(pieces) == nb
    tm = _row_block(m, 512)
    steps = m // tm
    ex = None if exchange is None else _Exchange(*exchange)

    def body(*refs):
        first = jnp.logical_and(pl.program_id(0) == 0, pl.program_id(1) == 0)
        last = jnp.logical_and(pl.program_id(0) == steps - 1, pl.program_id(1) == nb - 1)
        own, finish = _ride_along(ex, refs, nb + 1, 1, first, last)
        g_refs, w_ref, o_ref = own[:nb], own[nb], own[nb + 1]
        j = pl.program_id(1)
        for p in range(nb):
            @pl.when(j == p)
            def _(p=p):
                r = _dot(g_refs[p][...], w_ref[p], NT)
                if p == 0:
                    o_ref[...] = r
                else:
                    o_ref[...] += r

        finish()

    in_specs = [pl.BlockSpec((tm, nc), lambda i, j: (i, 0)) for _ in range(nb)]
    in_specs.append(pl.BlockSpec((nb, k, nc), lambda i, j: (0, 0, 0)))
    out_specs, out_shape = [pl.BlockSpec((tm, k), lambda i, j: (i, 0))], [jax.ShapeDtypeStruct((m, k), F32)]
    args, scratch = list(pieces) + [w], []
    if ex is not None:
        in_specs, out_specs, out_shape = in_specs + ex.in_specs, out_specs + ex.out_specs, out_shape + ex.out_shape
        args, scratch = args + list(exchange[0]), ex.scratch_shapes
    out = pl.pallas_call(
        body, name=name, grid=(steps, nb), in_specs=in_specs, out_specs=out_specs, out_shape=out_shape,
        scratch_shapes=scratch, compiler_params=_params(),
    )(*args)
    return out[0], out[1:]


def _mm_tn(a, pieces, *, name):
    m, k = a.shape
    nb, nc = len(pieces), pieces[0].shape[1]
    tm = _row_block(m, 512)
    steps = m // tm

    def body(*refs):
        a_ref, g_refs, o_ref, acc_ref = refs[0], refs[1:1 + nb], refs[1 + nb], refs[2 + nb]
        j, i = pl.program_id(0), pl.program_id(1)
        rows = pl.ds(pl.multiple_of(i * tm, tm), tm)
        for p in range(nb):
            @pl.when(j == p)
            def _(p=p):
                r = _dot(a_ref[rows, :], g_refs[p][...], TN)

                @pl.when(i == 0)
                def _():
                    acc_ref[...] = r

                @pl.when(i != 0)
                def _():
                    acc_ref[...] += r

        @pl.when(i == steps - 1)
        def _():
            o_ref[...] = acc_ref[...].astype(BF16)

    def piece_spec(p):
        return pl.BlockSpec((tm, nc), lambda j, i: (jnp.where(j == p, i, 0), 0))

    return pl.pallas_call(
        body, name=name, grid=(nb, steps),
        in_specs=[pl.BlockSpec((m, k), lambda j, i: (0, 0))] + [piece_spec(p) for p in range(nb)],
        out_specs=pl.BlockSpec((None, k, nc), lambda j, i: (j, 0, 0)),
        out_shape=jax.ShapeDtypeStruct((nb, k, nc), BF16), scratch_shapes=[pltpu.VMEM((k, nc), F32)],
        compiler_params=_params(),
    )(a, *pieces)


def _rms(x, g):
    return x * lax.rsqrt(jnp.mean(x * x, axis=-1, keepdims=True) + EPS) * g


def _rms_bwd(x, g, dy):
    r = lax.rsqrt(jnp.mean(x * x, axis=-1, keepdims=True) + EPS)
    xh = x * r
    dxh = dy * g
    dx = r * (dxh - xh * jnp.mean(dxh * xh, axis=-1, keepdims=True))
    return dx, jnp.sum(dy * xh, axis=0, keepdims=True)


def _accumulate(ref, val):
    @pl.when(pl.program_id(0) == 0)
    def _():
        ref[...] = val

    @pl.when(pl.program_id(0) != 0)
    def _():
        ref[...] += val


def _norm_fwd(x, g, *, name):
    m, d = x.shape
    tm = _row_block(m, 512)

    def body(x_ref, g_ref, o_ref):
        o_ref[...] = _rms(x_ref[...], g_ref[...]).astype(BF16)

    return pl.pallas_call(
        body, name=name, grid=(m // tm,),
        in_specs=[pl.BlockSpec((tm, d), lambda i: (i, 0)), pl.BlockSpec((1, d), lambda i: (0, 0))],
        out_specs=pl.BlockSpec((tm, d), lambda i: (i, 0)),
        out_shape=jax.ShapeDtypeStruct((m, d), BF16), compiler_params=_params(),
    )(x, g)


def _norm_bwd(x, g, dy, add, *, name):
    m, d = x.shape
    tm = _row_block(m, 512)

    def body(x_ref, g_ref, dy_ref, add_ref, dx_ref, dg_ref):
        dx, dg = _rms_bwd(x_ref[...], g_ref[...], dy_ref[...])
        dx_ref[...] = dx + add_ref[...]
        _accumulate(dg_ref, dg)

    row = pl.BlockSpec((tm, d), lambda i: (i, 0))
    vec = pl.BlockSpec((1, d), lambda i: (0, 0))
    return pl.pallas_call(
        body, name=name, grid=(m // tm,), in_specs=[row, vec, row, row], out_specs=[row, vec],
        out_shape=[jax.ShapeDtypeStruct((m, d), F32), jax.ShapeDtypeStruct((1, d), F32)], compiler_params=_params(),
    )(x, g, dy, add)


def _silu(x):
    return x * jax.nn.sigmoid(x)


def _mix_block(oa, ob, ag, bg, ga, gb):
    parts = []
    for h in range(A_HEADS):
        o = oa[:, HEAD_A * h:HEAD_A * (h + 1)]
        parts.append(o * lax.rsqrt(jnp.mean(o * o, axis=-1, keepdims=True) + EPS))
    ya = jnp.concatenate(parts, axis=1) * ga * _silu(ag)
    low = lax.broadcasted_iota(jnp.int32, (1, LANES), 1) < HEAD_B
    parts = []
    for p in range(B_PAIRS):
        o = ob[:, LANES * p:LANES * (p + 1)]
        sq = o * o
        s_lo = jnp.sum(jnp.where(low, sq, 0.0), axis=-1, keepdims=True)
        s_hi = jnp.sum(jnp.where(low, 0.0, sq), axis=-1, keepdims=True)
        r = jnp.where(low, lax.rsqrt(s_lo * (1.0 / HEAD_B) + EPS), lax.rsqrt(s_hi * (1.0 / HEAD_B) + EPS))
        parts.append(o * r)
    yb = jnp.concatenate(parts, axis=1) * gb * _silu(bg)
    return jnp.concatenate([ya, yb], axis=1)


_GATE_A_BLOCK = 3
_GATE_B_BLOCK = 7


def _mix_fwd(oa, ob, proj, ga, gb, *, name):
    m, w = oa.shape
    tm = _row_block(m, 512)

    def body(oa_ref, ob_ref, ag_ref, bg_ref, ga_ref, gb_ref, y_ref):
        y_ref[...] = _mix_block(oa_ref[...], ob_ref[...], ag_ref[...], bg_ref[...], ga_ref[...], gb_ref[...]).astype(BF16)

    row = pl.BlockSpec((tm, w), lambda i: (i, 0))
    vec = pl.BlockSpec((1, w), lambda i: (0, 0))
    return pl.pallas_call(
        body, name=name, grid=(m // tm,),
        in_specs=[row, row, pl.BlockSpec((tm, w), lambda i: (i, _GATE_A_BLOCK)),
                  pl.BlockSpec((tm, w), lambda i: (i, _GATE_B_BLOCK)), vec, vec],
        out_specs=pl.BlockSpec((tm, 2 * w), lambda i: (i, 0)),
        out_shape=jax.ShapeDtypeStruct((m, 2 * w), BF16), compiler_params=_params(),
    )(oa, ob, proj, proj, ga, gb)


def _mix_bwd(dy, oa, ob, proj, ga, gb, *, name):
    m, w = oa.shape
    tm = _row_block(m, 256)

    def body(dy_ref, oa_ref, ob_ref, ag_ref, bg_ref, ga_ref, gb_ref, doa_ref, dob_ref, dag_ref, dbg_ref, dga_ref, dgb_ref):
        _, vjp = jax.vjp(_mix_block, oa_ref[...], ob_ref[...], ag_ref[...], bg_ref[...], ga_ref[...], gb_ref[...])
        doa, dob, dag, dbg, dga, dgb = vjp(dy_ref[...])
        doa_ref[...] = doa
        dob_ref[...] = dob.astype(BF16)
        dag_ref[...] = dag.astype(BF16)
        dbg_ref[...] = dbg.astype(BF16)
        _accumulate(dga_ref, dga)
        _accumulate(dgb_ref, dgb)

    row = pl.BlockSpec((tm, w), lambda i: (i, 0))
    vec = pl.BlockSpec((1, w), lambda i: (0, 0))
    big = jax.ShapeDtypeStruct((m, w), F32)
    half = jax.ShapeDtypeStruct((m, w), BF16)
    small = jax.ShapeDtypeStruct((1, w), F32)
    return pl.pallas_call(
        body, name=name, grid=(m // tm,),
        in_specs=[pl.BlockSpec((tm, 2 * w), lambda i: (i, 0)), row, row,
                  pl.BlockSpec((tm, w), lambda i: (i, _GATE_A_BLOCK)), pl.BlockSpec((tm, w), lambda i: (i, _GATE_B_BLOCK)), vec, vec],
        out_specs=[row, row, row, row, vec, vec], out_shape=[big, half, half, half, small, small], compiler_params=_params(),
    )(dy, oa, ob, proj, proj, ga, gb)


def _ple_block(h1, gp, pp, gpost):
    return h1 + jax.nn.sigmoid(gp) * _rms(pp, gpost)


def _ple_fwd(h1, gp, pp, gpost, *, name):
    m, d = h1.shape
    tm = _row_block(m, 512)

    def body(h_ref, gp_ref, pp_ref, g_ref, o_ref):
        o_ref[...] = _ple_block(h_ref[...], gp_ref[...], pp_ref[...], g_ref[...])

    row = pl.BlockSpec((tm, d), lambda i: (i, 0))
    vec = pl.BlockSpec((1, d), lambda i: (0, 0))
    return pl.pallas_call(
        body, name=name, grid=(m // tm,), in_specs=[row, row, row, vec], out_specs=row,
        out_shape=jax.ShapeDtypeStruct((m, d), F32), compiler_params=_params(),
    )(h1, gp, pp, gpost)


def _ple_bwd(dh2, gp, pp, gpost, *, name):
    m, d = dh2.shape
    tm = _row_block(m, 512)

    def body(dh_ref, gp_ref, pp_ref, g_ref, dgp_ref, dpp_ref, dg_ref):
        dh = dh_ref[...]
        gate = jax.nn.sigmoid(gp_ref[...])
        pe = _rms(pp_ref[...], g_ref[...])
        dgp_ref[...] = (dh * pe * (gate * (1.0 - gate))).astype(BF16)
        dpp, dg = _rms_bwd(pp_ref[...], g_ref[...], dh * gate)
        dpp_ref[...] = dpp.astype(BF16)
        _accumulate(dg_ref, dg)

    row = pl.BlockSpec((tm, d), lambda i: (i, 0))
    vec = pl.BlockSpec((1, d), lambda i: (0, 0))
    big = jax.ShapeDtypeStruct((m, d), BF16)
    return pl.pallas_call(
        body, name=name, grid=(m // tm,), in_specs=[row, row, row, vec], out_specs=[row, row, vec],
        out_shape=[big, big, jax.ShapeDtypeStruct((1, d), F32)], compiler_params=_params(),
    )(dh2, gp, pp, gpost)


def _loss_head(h, gf, target, *, name):
    m, d = h.shape
    tm = _row_block(m, 512)

    def body(h_ref, g_ref, t_ref, loss_ref, dh_ref, dg_ref):
        x = h_ref[...]
        g = g_ref[...]
        err = _rms(x, g) - t_ref[...]
        part = 0.5 * jnp.sum(jnp.mean(err * err, axis=-1, keepdims=True), axis=0, keepdims=True)
        dx, dg = _rms_bwd(x, g, err * (1.0 / d))
        dh_ref[...] = dx
        _accumulate(dg_ref, dg)
        _accumulate(loss_ref, jnp.broadcast_to(part, (8, LANES)))

    row = pl.BlockSpec((tm, d), lambda i: (i, 0))
    vec = pl.BlockSpec((1, d), lambda i: (0, 0))
    return pl.pallas_call(
        body, name=name, grid=(m // tm,), in_specs=[row, vec, row],
        out_specs=[pl.BlockSpec((8, LANES), lambda i: (0, 0)), row, vec],
        out_shape=[jax.ShapeDtypeStruct((8, LANES), F32), jax.ShapeDtypeStruct((m, d), F32), jax.ShapeDtypeStruct((1, d), F32)],
        compiler_params=_params(),
    )(h, gf, target)


def _hgrn_pre(aq, af, lbv):
    sq = jax.nn.sigmoid(aq)
    sg = jax.nn.sigmoid(-af)
    k = (1.0 - lbv) * sg
    return aq * sq, sq, sg, k, jnp.log1p(-k)


def _tri(n, cmp):
    r = lax.broadcasted_iota(jnp.int32, (n, n), 0)
    c = lax.broadcasted_iota(jnp.int32, (n, n), 1)
    return cmp(r, c).astype(BF16)


def _hgrn_intra(q, k, b, b_scr):
    col = lax.broadcasted_iota(jnp.int32, (SUB, CHUNK), 1)
    rows = [jnp.zeros((SUB, CHUNK), F32)]
    facs = [None]
    for i in range(1, CHUNK // SUB):
        r = b_scr[SUB * i - 1:SUB * i, :]
        fq = jnp.exp(b[SUB * i:SUB * (i + 1)] - r)
        fk = jnp.exp(jnp.minimum(r - b, 0.0))
        qs = q[SUB * i:SUB * (i + 1)] * fq
        ks = k * fk
        rows.append(jnp.where(col < SUB * i, _dot(qs, ks, NT), 0.0))
        facs.append((fq, fk, qs, ks))
    return jnp.concatenate(rows, axis=0), facs


def _hgrn_diag_weights(i, k, b, q_scr, b_scr):
    sub = lax.broadcasted_iota(jnp.int32, (SUB, LANES), 0)
    ki = k[SUB * i:SUB * (i + 1)]
    bi = b[SUB * i:SUB * (i + 1)]
    es, ws = [], []
    for tau in range(SUB):
        t = SUB * i + tau
        e = jnp.exp(jnp.where(sub <= tau, b_scr[t:t + 1, :] - bi, NEG_BIG))
        es.append(e)
        ws.append((q_scr[t:t + 1, :] * ki) * e)
    return es, jnp.concatenate(ws, axis=0)


def _hgrn_fwd(proj, lb_row, *, name):
    s = proj.shape[0]
    tb = _row_block(s, 512)
    nch = tb // CHUNK
    wide = HGRN_HEADS_PER_STEP * HEAD_A

    def body(aq_all, af_all, ai_all, lb_all, o_all, st_all, s_all, q_all, b_all, od_all):
        @pl.when(pl.program_id(1) == 0)
        def _():
            s_all[...] = jnp.zeros_like(s_all)

        tril = _tri(CHUNK, lambda r, c: r >= c)
        ones = jnp.ones((LANES, LANES), BF16)

        def head_chunk(hh, ci):
            lanes = slice(HEAD_A * hh, HEAD_A * (hh + 1))
            aq_ref, af_ref, ai_ref, o_ref = aq_all.at[:, lanes], af_all.at[:, lanes], ai_all.at[:, lanes], o_all.at[:, lanes]
            st_ref, s_scr, q_scr, b_scr, od_scr = st_all.at[hh], s_all.at[hh], q_all.at[hh], b_all.at[hh], od_all.at[hh]
            lbv = lb_all[:, lanes]
            r0 = pl.multiple_of(ci * CHUNK, CHUNK)
            q, _, _, k, g = _hgrn_pre(aq_ref[pl.ds(r0, CHUNK), :], af_ref[pl.ds(r0, CHUNK), :], lbv)
            v = ai_ref[pl.ds(r0, CHUNK), :]
            st0 = s_scr[...]
            st_ref[ci] = st0
            b = _dot_exact_lhs(tril, g, 3)
            b_scr[...] = b
            q_scr[...] = q
            bl = b_scr[CHUNK - 1:CHUNK, :]
            o = _dot(q * jnp.exp(b), st0, NT)
            a, _ = _hgrn_intra(q, k, b, b_scr)
            o = o + _dot(a, v, NN)
            for i in range(CHUNK // SUB):
                _, wst = _hgrn_diag_weights(i, k, b, q_scr, b_scr)
                rep = _dot(wst, ones, NN)
                vi = v[SUB * i:SUB * (i + 1)]
                for tau in range(SUB):
                    t = SUB * i + tau
                    od_scr[t:t + 1, :] = jnp.sum(rep[SUB * tau:SUB * (tau + 1)] * vi, axis=0, keepdims=True)
            o_ref[pl.ds(r0, CHUNK), :] = o + od_scr[...]
            s_scr[...] = st0 * jnp.exp(bl) + _dot(v, k * jnp.exp(bl - b), TN)

        def chunk(ci, carry):
            for hh in range(HGRN_HEADS_PER_STEP):
                head_chunk(hh, ci)
            return carry

        lax.fori_loop(0, nch, chunk, 0)

    def col(block0):
        return pl.BlockSpec((tb, wide), lambda h, t: (t, block0 + h))

    steps = A_HEADS // HGRN_HEADS_PER_STEP
    per_head = pltpu.VMEM((HGRN_HEADS_PER_STEP, CHUNK, HEAD_A), F32)
    return pl.pallas_call(
        body, name=name, grid=(steps, s // tb),
        in_specs=[col(0), col(steps), col(2 * steps), pl.BlockSpec((1, wide), lambda h, t: (0, h))],
        out_specs=[pl.BlockSpec((tb, wide), lambda h, t: (t, h)),
                   pl.BlockSpec((HGRN_HEADS_PER_STEP, nch, HEAD_A, HEAD_A), lambda h, t: (h, t, 0, 0))],
        out_shape=[jax.ShapeDtypeStruct((s, A_HEADS * HEAD_A), F32),
                   jax.ShapeDtypeStruct((A_HEADS, s // CHUNK, HEAD_A, HEAD_A), F32)],
        scratch_shapes=[pltpu.VMEM((HGRN_HEADS_PER_STEP, HEAD_A, HEAD_A), F32), per_head, per_head, per_head],
        compiler_params=_params(),
    )(proj, proj, proj, lb_row)


def _hgrn_bwd(proj, lb_row, states, do_a, *, name):
    s = proj.shape[0]
    tb = _row_block(s, 512)
    nch = tb // CHUNK
    nblk = s // tb

    wide = HGRN_HEADS_PER_STEP * HEAD_A

    def body(aq_all, af_all, ai_all, lb_all, st_all, do_all, daq_all, daf_all, dai_all, dlb_all,
             ds_all, q_all, b_all, dq_all, dos_all):
        @pl.when(pl.program_id(1) == 0)
        def _():
            ds_all[...] = jnp.zeros_like(ds_all)
            dlb_all[...] = jnp.zeros_like(dlb_all)

        tril = _tri(CHUNK, lambda r, c: r >= c)
        triu = _tri(CHUNK, lambda r, c: c >= r)
        ones = jnp.ones((LANES, LANES), BF16)
        col = lax.broadcasted_iota(jnp.int32, (SUB, CHUNK), 1)
        row = lax.broadcasted_iota(jnp.int32, (CHUNK, LANES), 0)
        nsub = CHUNK // SUB

        def head_chunk(hh, ci):
            lanes = slice(HEAD_A * hh, HEAD_A * (hh + 1))
            aq_ref, af_ref, ai_ref, do_ref = aq_all.at[:, lanes], af_all.at[:, lanes], ai_all.at[:, lanes], do_all.at[:, lanes]
            daq_ref, daf_ref, dai_ref, dlb_ref = daq_all.at[:, lanes], daf_all.at[:, lanes], dai_all.at[:, lanes], dlb_all.at[:, lanes]
            st_ref, ds_scr, q_scr, b_scr = st_all.at[hh], ds_all.at[hh], q_all.at[hh], b_all.at[hh]
            dq_scr, do_scr = dq_all.at[hh], dos_all.at[hh]
            lbv = lb_all[:, lanes]
            r0 = pl.multiple_of(ci * CHUNK, CHUNK)
            aq = aq_ref[pl.ds(r0, CHUNK), :]
            q, sq, sg, k, g = _hgrn_pre(aq, af_ref[pl.ds(r0, CHUNK), :], lbv)
            v = ai_ref[pl.ds(r0, CHUNK), :]
            do = do_ref[pl.ds(r0, CHUNK), :]
            st0 = st_ref[ci]
            dst1 = ds_scr[...]
            b = _dot_exact_lhs(tril, g, 3)
            b_scr[...] = b
            q_scr[...] = q
            do_scr[...] = do
            bl = b_scr[CHUNK - 1:CHUNK, :]
            eb = jnp.exp(b)
            ebl = jnp.exp(bl)
            ekd = jnp.exp(bl - b)
            kd = k * ekd
            a, facs = _hgrn_intra(q, k, b, b_scr)

            da = _dot(do, v, NT)
            dv = _dot(a, do, TN) + _dot(kd, dst1, NT)
            dq = _dot(do, st0, NN) * eb
            dk_inter = _dot(v, dst1, NN) * ekd
            dk = dk_inter
            dq_rows = [jnp.zeros((SUB, LANES), F32)]
            for i in range(1, nsub):
                fq, fk, qs, ks = facs[i]
                da_i = jnp.where(col < SUB * i, da[SUB * i:SUB * (i + 1)], 0.0)
                dq_rows.append(_dot(da_i, ks, NN) * fq)
                dk = dk + _dot(da_i, qs, TN) * fk
            dq = dq + jnp.concatenate(dq_rows, axis=0)

            dv_rows, dk_rows = [], []
            for i in range(nsub):
                es, wst = _hgrn_diag_weights(i, k, b, q_scr, b_scr)
                ki = k[SUB * i:SUB * (i + 1)]
                vi = v[SUB * i:SUB * (i + 1)]
                rep = _dot(wst, ones, NN)
                mst = jnp.concatenate([do_scr[SUB * i + tau:SUB * i + tau + 1, :] * vi for tau in range(SUB)], axis=0)
                drep = _dot(mst, ones, NN)
                dvi = jnp.zeros((SUB, LANES), F32)
                dki = jnp.zeros((SUB, LANES), F32)
                for tau in range(SUB):
                    t = SUB * i + tau
                    gt = drep[SUB * tau:SUB * (tau + 1)] * es[tau]
                    dvi = dvi + rep[SUB * tau:SUB * (tau + 1)] * do_scr[t:t + 1, :]
                    dki = dki + gt * q_scr[t:t + 1, :]
                    dq_scr[t:t + 1, :] = jnp.sum(gt * ki, axis=0, keepdims=True)
                dv_rows.append(dvi)
                dk_rows.append(dki)
            dv = dv + jnp.concatenate(dv_rows, axis=0)
            dk = dk + jnp.concatenate(dk_rows, axis=0)
            dq = dq + dq_scr[...]

            ds_scr[...] = dst1 * ebl + _dot(do, q * eb, TN)
            last = jnp.sum(k * dk_inter, axis=0, keepdims=True) + ebl * jnp.sum(st0 * dst1, axis=0, keepdims=True)
            db = q * dq - k * dk + jnp.where(row == CHUNK - 1, last, 0.0)
            dg = _dot_exact_lhs(triu, db, 3)

            dkt = dk - dg / (1.0 - k)
            daq_ref[pl.ds(r0, CHUNK), :] = (dq * (sq * (1.0 + aq * (1.0 - sq)))).astype(BF16)
            daf_ref[pl.ds(r0, CHUNK), :] = (dkt * (1.0 - lbv) * (-(sg * (1.0 - sg)))).astype(BF16)
            dai_ref[pl.ds(r0, CHUNK), :] = dv.astype(BF16)
            dlb_ref[...] += jnp.sum(dkt * (-sg), axis=0, keepdims=True)

        def chunk(cj, carry):
            for hh in range(HGRN_HEADS_PER_STEP):
                head_chunk(hh, nch - 1 - cj)
            return carry

        lax.fori_loop(0, nch, chunk, 0)

    def col_in(block0):
        return pl.BlockSpec((tb, wide), lambda h, t: (nblk - 1 - t, block0 + h))

    steps = A_HEADS // HGRN_HEADS_PER_STEP
    head_col = pl.BlockSpec((tb, wide), lambda h, t: (nblk - 1 - t, h))
    vec = pl.BlockSpec((1, wide), lambda h, t: (0, h))
    half = jax.ShapeDtypeStruct((s, A_HEADS * HEAD_A), BF16)
    per_head = pltpu.VMEM((HGRN_HEADS_PER_STEP, CHUNK, HEAD_A), F32)
    return pl.pallas_call(
        body, name=name, grid=(steps, nblk),
        in_specs=[col_in(0), col_in(steps), col_in(2 * steps), vec,
                  pl.BlockSpec((HGRN_HEADS_PER_STEP, nch, HEAD_A, HEAD_A), lambda h, t: (h, nblk - 1 - t, 0, 0)), head_col],
        out_specs=[head_col, head_col, head_col, vec],
        out_shape=[half, half, half, jax.ShapeDtypeStruct((1, A_HEADS * HEAD_A), F32)],
        scratch_shapes=[pltpu.VMEM((HGRN_HEADS_PER_STEP, HEAD_A, HEAD_A), F32), per_head, per_head, per_head, per_head],
        compiler_params=_params(),
    )(proj, proj, proj, lb_row, states, do_a)


_BQ_BLOCK = 16
_BK_BLOCK = 20
_BV_BLOCK = 24
SB_SCALE = HEAD_B ** -0.5
SB_TQ = 1024
SB_TK = 256


def _sb_blocks(s):
    tq = min(SB_TQ, s)
    tk = min(SB_TK, s)
    assert s % tq == 0 and tq % tk == 0 and s // tk <= LANES
    return tq, tk, (2 if (tq // tk) % 2 == 0 else 1)


def _sb2_weights(qh, kb, carry, incl, mask):
    z = _dot(qh, kb, NT)
    sp = jnp.maximum(z, 0.0) + jnp.log(1.0 + jnp.exp(-jnp.abs(z)))
    if mask is not None:
        sp = jnp.where(mask, sp, 0.0)
    suffix = _dot(sp, incl, NN)
    w = jnp.exp(z - suffix + carry)
    if mask is not None:
        w = jnp.where(mask, w, 0.0)
    return z, sp, w, suffix[:, 0:1]


def _sb_mask(qi, j, tq, tk, row0):
    row = qi * tq + row0 + lax.broadcasted_iota(jnp.int32, (tq - row0, tk), 0)
    col = j * tk + lax.broadcasted_iota(jnp.int32, (tq - row0, tk), 1)
    return col < row


def _below(x, row0, new_tail):
    return new_tail if row0 == 0 else jnp.concatenate([x[:row0], new_tail], axis=0)


def _ride_along(ex, refs, n_in, n_out, first, last):
    if ex is None:
        return refs, lambda: None
    own = refs[:n_in] + refs[n_in + ex.n:n_in + ex.n + n_out]
    xin = refs[n_in:n_in + ex.n]
    xout = refs[n_in + ex.n + n_out:n_in + 2 * ex.n + n_out]
    rest = refs[n_in + 2 * ex.n + n_out:]
    sems, scratch = rest[len(rest) - 3:], rest[:len(rest) - 3]

    @pl.when(first)
    def _():
        ex.start(xin, xout, sems)

    def finish():
        @pl.when(last)
        def _():
            ex.finish(xin, xout, sems)

    return own + scratch, finish


def _sb2_fwd(proj, *, name, exchange=None):
    s = proj.shape[0]
    tq, tk, unroll = _sb_blocks(s)
    nq, ratio = s // tq, tq // tk
    ex = None if exchange is None else _Exchange(*exchange)

    def body(*refs):
        first = jnp.logical_and(pl.program_id(0) == 0, pl.program_id(1) == 0)
        last = jnp.logical_and(pl.program_id(0) == B_PAIRS - 1, pl.program_id(1) == nq - 1)
        (q_ref, k_ref, v_ref, o_ref, c_ref), finish = _ride_along(ex, refs, 3, 2, first, last)
        qi = pl.program_id(1)
        lane = lax.broadcasted_iota(jnp.int32, (1, LANES), 1)
        low = lane < HEAD_B

        def heads(x):
            return jnp.where(low, x, 0.0).astype(BF16), jnp.where(low, 0.0, x).astype(BF16)

        qh = heads(q_ref[...] * SB_SCALE)
        incl = _tri(tk, lambda r, c: r >= c)
        o_ref[...] = jnp.zeros_like(o_ref)
        c_ref[...] = jnp.zeros_like(c_ref)

        def tile(j, c, masked, row0=0):
            s0 = pl.multiple_of(j * tk, tk)
            kb = k_ref[pl.ds(s0, tk), :].astype(BF16)
            vh = heads(v_ref[pl.ds(s0, tk), :])
            mask = _sb_mask(qi, j, tq, tk, row0) if masked else None
            c_new = []
            acc = jnp.zeros((tq - row0, LANES), F32)
            for h in range(2):
                _, _, w, rowsum = _sb2_weights(qh[h][row0:], kb, c[h][row0:], incl, mask)
                c_ref[h] = jnp.where(lane == j, c[h], c_ref[h])
                c_new.append(_below(c[h], row0, c[h][row0:] - rowsum))
                acc = acc + _dot(w, vh[h], NN)
            o_ref[row0:, :] += acc
            return tuple(c_new)

        zc = jnp.zeros((tq, 1), F32)
        c = (zc, zc)
        for r in reversed(range(ratio)):
            c = tile(qi * ratio + r, c, True, r * tk)
        def several(jj, c):
            for r in range(unroll):
                c = tile(qi * ratio - 1 - (jj * unroll + r), c, False)
            return c

        lax.fori_loop(0, qi * (ratio // unroll), several, c)
        finish()

    def whole(block0):
        return pl.BlockSpec((s, LANES), lambda p, i: (0, block0 + p))

    in_specs = [pl.BlockSpec((tq, LANES), lambda p, i: (i, _BQ_BLOCK + p)), whole(_BK_BLOCK), whole(_BV_BLOCK)]
    out_specs = [pl.BlockSpec((tq, LANES), lambda p, i: (i, p)), pl.BlockSpec((None, 2, tq, LANES), lambda p, i: (p, 0, i, 0))]
    out_shape = [jax.ShapeDtypeStruct((s, B_PAIRS * LANES), F32), jax.ShapeDtypeStruct((B_PAIRS, 2, s, LANES), F32)]
    args, scratch = [proj, proj, proj], []
    if ex is not None:
        in_specs, out_specs, out_shape = in_specs + ex.in_specs, out_specs + ex.out_specs, out_shape + ex.out_shape
        args, scratch = args + list(exchange[0]), ex.scratch_shapes
    out = pl.pallas_call(
        body, name=name, grid=(B_PAIRS, nq), in_specs=in_specs, out_specs=out_specs, out_shape=out_shape,
        scratch_shapes=scratch, compiler_params=_params(),
    )(*args)
    return out[0], out[1], out[2:]


def _sb2_bwd(proj, carries, do_b, *, name, exchange=None):
    s = proj.shape[0]
    tq, tk, unroll = _sb_blocks(s)
    nq, nk, ratio = s // tq, s // tk, tq // tk
    ex = None if exchange is None else _Exchange(*exchange)

    def body(*refs):
        first = jnp.logical_and(pl.program_id(0) == 0, pl.program_id(1) == 0)
        last = jnp.logical_and(pl.program_id(0) == B_PAIRS - 1, pl.program_id(1) == nq - 1)
        own, finish = _ride_along(ex, refs, 5, 3, first, last)
        q_ref, k_ref, v_ref, c_ref, do_ref, dq_ref, dk_ref, dv_ref, dkt_scr, dvt_scr = own
        qi = pl.program_id(1)

        @pl.when(qi == 0)
        def _():
            dkt_scr[...] = jnp.zeros_like(dkt_scr)
            dvt_scr[...] = jnp.zeros_like(dvt_scr)

        lane = lax.broadcasted_iota(jnp.int32, (1, LANES), 1)
        low = lane < HEAD_B
        low_row = lax.broadcasted_iota(jnp.int32, (LANES, 1), 0) < HEAD_B

        def heads(x):
            return jnp.where(low, x, 0.0).astype(BF16), jnp.where(low, 0.0, x).astype(BF16)

        def heads_t(x):
            xt = x.T
            return jnp.where(low_row, xt, 0.0).astype(BF16), jnp.where(low_row, 0.0, xt).astype(BF16)

        q = q_ref[...] * SB_SCALE
        do = do_ref[...].astype(F32)
        qh, qth = heads(q), heads_t(q)
        doh, doth = heads(do), heads_t(do)
        cst = (c_ref[0], c_ref[1])
        incl = _tri(tk, lambda r, c: r >= c)
        pre = _tri(tk, lambda r, c: r <= c)

        def tile(j, carry, masked, row0=0):
            ec, dq = carry
            s0 = pl.multiple_of(j * tk, tk)
            kf = k_ref[pl.ds(s0, tk), :]
            kb = kf.astype(BF16)
            kh = heads(kf)
            vb = v_ref[pl.ds(s0, tk), :].astype(BF16)
            mask = _sb_mask(qi, j, tq, tk, row0) if masked else None
            ec_new = []
            dkt = jnp.zeros((LANES, tk), F32)
            dvt = jnp.zeros((LANES, tk), F32)
            dq_add = jnp.zeros((tq - row0, LANES), F32)
            for h in range(2):
                cin = jnp.sum(jnp.where(lane == j, cst[h][row0:], 0.0), axis=1, keepdims=True)
                z, sp, w, _ = _sb2_weights(qh[h][row0:], kb, cin, incl, mask)
                e = w * _dot(doh[h][row0:], vb, NT)
                prefix = _dot(e, pre, NN)
                dz = e - jnp.exp(z - sp) * (ec[h][row0:] + prefix)
                if mask is not None:
                    dz = jnp.where(mask, dz, 0.0)
                ec_new.append(_below(ec[h], row0, ec[h][row0:] + prefix[:, tk - 1:tk]))
                dzb = dz.astype(BF16)
                wb = w.astype(BF16)
                dq_add = dq_add + _dot(dzb, kh[h], NN)
                dkt = dkt + _dot(qth[h][:, row0:], dzb, NN)
                dvt = dvt + _dot(doth[h][:, row0:], wb, NN)
            dkt_scr[j] += dkt
            dvt_scr[j] += dvt
            return tuple(ec_new), _below(dq, row0, dq[row0:] + dq_add)

        zc = jnp.zeros((tq, 1), F32)
        def several(jj, carry):
            for r in range(unroll):
                carry = tile(jj * unroll + r, carry, False)
            return carry

        carry = lax.fori_loop(0, qi * (ratio // unroll), several, ((zc, zc), jnp.zeros((tq, LANES), F32)))
        for r in range(ratio):
            carry = tile(qi * ratio + r, carry, True, r * tk)
        dq_ref[...] = (carry[1] * SB_SCALE).astype(BF16)

        @pl.when(qi == nq - 1)
        def _():
            for j in range(nk):
                dk_ref[tk * j:tk * (j + 1), :] = dkt_scr[j].T.astype(BF16)
                dv_ref[tk * j:tk * (j + 1), :] = dvt_scr[j].T.astype(BF16)

        finish()

    def whole_in(block0):
        return pl.BlockSpec((s, LANES), lambda p, i: (0, block0 + p))

    blk = pl.BlockSpec((tq, LANES), lambda p, i: (i, p))
    whole_out = pl.BlockSpec((s, LANES), lambda p, i: (0, p))
    big = jax.ShapeDtypeStruct((s, B_PAIRS * LANES), BF16)
    in_specs = [pl.BlockSpec((tq, LANES), lambda p, i: (i, _BQ_BLOCK + p)), whole_in(_BK_BLOCK), whole_in(_BV_BLOCK),
                pl.BlockSpec((None, 2, tq, LANES), lambda p, i: (p, 0, i, 0)), blk]
    out_specs, out_shape = [blk, whole_out, whole_out], [big, big, big]
    args = [proj, proj, proj, carries, do_b]
    scratch = [pltpu.VMEM((nk, LANES, tk), F32), pltpu.VMEM((nk, LANES, tk), F32)]
    if ex is not None:
        in_specs, out_specs, out_shape = in_specs + ex.in_specs, out_specs + ex.out_specs, out_shape + ex.out_shape
        args, scratch = args + list(exchange[0]), scratch + ex.scratch_shapes
    out = pl.pallas_call(
        body, name=name, grid=(B_PAIRS, nq), in_specs=in_specs, out_specs=out_specs, out_shape=out_shape,
        scratch_shapes=scratch, compiler_params=_params(),
    )(*args)
    return out[0], out[1], out[2], out[3:]


def _my_id():
    return 4 * lax.axis_index("x") + 2 * lax.axis_index("y") + lax.axis_index("c")


def _mesh_pos(d):
    return (d // 4, (d // 2) % 2, d % 2)


class _Exchange:
    def __init__(self, arrays, scatter):
        self.n = len(arrays)
        self.scatter = scatter
        self.in_specs = [pl.BlockSpec(memory_space=pl.ANY)] * self.n
        self.out_specs = [pl.BlockSpec(memory_space=pl.ANY)] * self.n
        self.out_shape = [
            jax.ShapeDtypeStruct(((N_DEV,) + a.shape[1:]) if scatter else ((N_DEV,) + a.shape), a.dtype) for a in arrays]
        self.scratch_shapes = [pltpu.SemaphoreType.DMA((N_DEV - 1, self.n)), pltpu.SemaphoreType.DMA((N_DEV - 1, self.n)),
                               pltpu.SemaphoreType.DMA((self.n,))]

    def _copies(self, ins, outs, sems):
        send_sems, recv_sems, local_sems = sems
        me = _my_id()

        def src(i, to):
            return ins[i].at[to] if self.scatter else ins[i]

        local = [pltpu.make_async_copy(src(i, me), outs[i].at[me], local_sems.at[i]) for i in range(self.n)]
        sends, recvs = [], []
        for step in range(1, N_DEV):
            to = (me + step) % N_DEV
            frm = (me + N_DEV - step) % N_DEV
            for i in range(self.n):
                sends.append(pltpu.make_async_remote_copy(
                    src_ref=src(i, to), dst_ref=outs[i].at[me],
                    send_sem=send_sems.at[step - 1, i], recv_sem=recv_sems.at[step - 1, i],
                    device_id=_mesh_pos(to), device_id_type=pl.DeviceIdType.MESH))
                recvs.append(pltpu.make_async_remote_copy(
                    src_ref=src(i, frm), dst_ref=outs[i].at[frm],
                    send_sem=send_sems.at[step - 1, i], recv_sem=recv_sems.at[step - 1, i],
                    device_id=_mesh_pos(frm), device_id_type=pl.DeviceIdType.MESH))
        return local, sends, recvs

    def start(self, ins, outs, sems):
        local, sends, _ = self._copies(ins, outs, sems)
        for cp in local + sends:
            cp.start()

    def finish(self, ins, outs, sems):
        local, sends, recvs = self._copies(ins, outs, sems)
        for cp in recvs:
            cp.wait_recv()
        for cp in sends:
            cp.wait_send()
        for cp in local:
            cp.wait()


def _exchange(arrays, scatter, *, name):
    ex = _Exchange(arrays, scatter)

    def body(*refs):
        ins, outs, sems = refs[:ex.n], refs[ex.n:2 * ex.n], refs[2 * ex.n:]
        ex.start(ins, outs, sems)
        ex.finish(ins, outs, sems)

    return pl.pallas_call(
        body, name=name, in_specs=ex.in_specs, out_specs=ex.out_specs, out_shape=ex.out_shape,
        scratch_shapes=ex.scratch_shapes,
    )(*arrays)


def _adamw_math(w, m, v, g):
    m2 = ADAM_B1 * m + (1.0 - ADAM_B1) * g
    v2 = ADAM_B2 * v + (1.0 - ADAM_B2) * (g * g)
    m_hat = m2 / (1.0 - ADAM_B1 ** ADAM_STEP)
    v_hat = v2 / (1.0 - ADAM_B2 ** ADAM_STEP)
    delta = -ADAM_LR * (m_hat / (jnp.sqrt(v_hat) + ADAM_EPS) + ADAM_WD * w)
    return delta, m2, v2


def _slot_sum(ref):
    g = ref[0].astype(F32)
    for d in range(1, N_DEV):
        g = g + ref[d].astype(F32)
    return g


def _adamw(w, m, v, slots, *, name):
    r, c = w.shape
    tr = _row_block(r, 256)

    def body(w_ref, m_ref, v_ref, s_ref, g_ref, d_ref, m2_ref, v2_ref):
        g = _slot_sum(s_ref)
        delta, m2, v2 = _adamw_math(w_ref[...], m_ref[...], v_ref[...], g)
        g_ref[...] = g
        d_ref[...] = delta
        m2_ref[...] = m2
        v2_ref[...] = v2

    row = pl.BlockSpec((tr, c), lambda i: (i, 0))
    shp = jax.ShapeDtypeStruct((r, c), F32)
    return pl.pallas_call(
        body, name=name, grid=(r // tr,),
        in_specs=[row, row, row, pl.BlockSpec((N_DEV, tr, c), lambda i: (0, i, 0))],
        out_specs=[row, row, row, row], out_shape=[shp, shp, shp, shp], compiler_params=_params(),
    )(w, m, v, slots)


def _layer_softmax(l_ref):
    l0 = l_ref[0:1, :]
    l1 = l_ref[1:2, :]
    mx = jnp.maximum(l0, l1)
    e0 = jnp.exp(l0 - mx)
    e1 = jnp.exp(l1 - mx)
    return e0 / (e0 + e1), e1 / (e0 + e1)


def _adamw_lb(w, m, v, slots, *, name):
    def body(w_ref, m_ref, v_ref, s_ref, g_ref, d_ref, m2_ref, v2_ref):
        sm0, sm1 = _layer_softmax(w_ref)
        dl1 = _slot_sum(s_ref) * (sm0 * sm1)
        for row, g in ((0, -dl1), (1, dl1)):
            sl = slice(row, row + 1)
            delta, m2, v2 = _adamw_math(w_ref[sl, :], m_ref[sl, :], v_ref[sl, :], g)
            g_ref[sl, :] = g
            d_ref[sl, :] = delta
            m2_ref[sl, :] = m2
            v2_ref[sl, :] = v2

    shp = jax.ShapeDtypeStruct(w.shape, F32)
    return pl.pallas_call(body, name=name, out_shape=[shp, shp, shp, shp])(w, m, v, slots)


def _lower_bound_1(lb_logits, *, name):
    def body(l_ref, o_ref):
        sm0, sm1 = _layer_softmax(l_ref)
        o_ref[...] = (sm0 + sm1) - sm0

    return pl.pallas_call(body, name=name, out_shape=jax.ShapeDtypeStruct((1, lb_logits.shape[1]), F32))(lb_logits)


_SMALL = ("norm_mix", "a_out_norm", "b_out_norm", "ple_gate_norm", "ple_post_norm")


def _pack_rows(vectors):
    offs, rows = [], 0
    for vec in vectors:
        offs.append(rows)
        rows += vec.shape[0] // LANES
    flat = jnp.concatenate([vec.reshape(-1, LANES) for vec in vectors], axis=0)
    pad = (-rows) % 8
    if pad:
        flat = jnp.concatenate([flat, jnp.zeros((pad, LANES), F32)], axis=0)
    return flat, offs


def kernel(x, p, norm_mix, w_in, a_out_norm, b_out_norm, w_out, lb_logits, ple_gate_norm, w_ple_gate, w_ple_proj, ple_post_norm, final_norm, loss_target, m_norm_mix, m_w_in, m_a_out_norm, m_b_out_norm, m_w_out, m_lb_logits, m_ple_gate_norm, m_w_ple_gate, m_w_ple_proj, m_ple_post_norm, m_final_norm, v_norm_mix, v_w_in, v_a_out_norm, v_b_out_norm, v_w_out, v_lb_logits, v_ple_gate_norm, v_w_ple_gate, v_w_ple_proj, v_ple_post_norm, v_final_norm):
    depth = w_in.shape[0]
    assert depth == 2
    s, d = x.shape[1], x.shape[2]
    h = x.reshape(s, d)
    target = loss_target.reshape(s, d)

    (win0,) = _exchange([w_in[0].astype(BF16)], scatter=False, name="gather_w_in0")
    later = [w_in[1].astype(BF16), w_out.astype(BF16), w_ple_gate.astype(BF16), w_ple_proj.astype(BF16)]
    lb1 = _lower_bound_1(lb_logits, name="lower_bound")
    lbs = [jnp.zeros_like(lb1), lb1]

    saved = []
    for l in range(depth):
        win = win0 if l == 0 else win1
        u = _norm_fwd(h, norm_mix[l:l + 1], name="norm_mix")
        proj = _mm_nn(u, win, name="mm_in")
        o_a, states = _hgrn_fwd(proj, lbs[l], name="hgrn_fwd")
        if l == 0:
            o_b, carries, (win1, wout_g, wpg_g, wpp_g) = _sb2_fwd(proj, name="sb_fwd_gather", exchange=(later, False))
        else:
            o_b, carries, _ = _sb2_fwd(proj, name="sb_fwd")
        wout = wout_g[:, l].reshape(1, d, d)
        wpg = wpg_g[:, l].reshape(1, d, d)
        wpp = jnp.transpose(wpp_g[:, l], (1, 0, 2)).reshape(1, p.shape[-1], d)
        y = _mix_fwd(o_a, o_b, proj, a_out_norm[l:l + 1], b_out_norm[l:l + 1], name="mix_fwd")
        h1 = _mm_nn(y, wout, add=h, name="mm_out")
        r1 = _norm_fwd(h1, ple_gate_norm[l:l + 1], name="norm_gate")
        gp = _mm_nn(r1, wpg, name="mm_gate")
        pl_in = p[l].reshape(s, p.shape[-1])
        pp = _mm_nn(pl_in, wpp, name="mm_ple")
        h2 = _ple_fwd(h1, gp, pp, ple_post_norm[l:l + 1], name="ple_fwd")
        saved.append((h, u, proj, o_a, states, o_b, carries, y, h1, r1, gp, pp, pl_in, win, wout, wpg, wpp))
        h = h2

    loss_part, dh, dg_final = _loss_head(h, final_norm.reshape(1, d), target, name="loss_head")

    big = ("w_in", "w_out", "w_ple_gate", "w_ple_proj")
    grads = [None] * depth
    for l in reversed(range(depth)):
        h0, u, proj, o_a, states, o_b, carries, y, h1, r1, gp, pp, pl_in, win, wout, wpg, wpp = saved[l]
        dgp, dpp, dg_post = _ple_bwd(dh, gp, pp, ple_post_norm[l:l + 1], name="ple_bwd")
        dwpp = _mm_tn(pl_in, [dpp], name="mm_ple_dw")
        dwpp = jnp.transpose(dwpp.reshape(p.shape[-1], N_DEV, d // N_DEV), (1, 0, 2))
        dwpg = _mm_tn(r1, [dgp], name="mm_gate_dw").reshape(N_DEV, d // N_DEV, d)
        dr1, _ = _mm_nt([dgp], wpg, name="mm_gate_dx")
        dh1, dg_pg = _norm_bwd(h1, ple_gate_norm[l:l + 1], dr1, dh, name="norm_gate_bwd")
        dwout = _mm_tn(y, [dh1], name="mm_out_dw").reshape(N_DEV, d // N_DEV, d)
        dy, _ = _mm_nt([dh1], wout, name="mm_out_dx")
        do_a, do_b, dag, dbg, dg_a, dg_b = _mix_bwd(dy, o_a, o_b, proj, a_out_norm[l:l + 1], b_out_norm[l:l + 1], name="mix_bwd")
        if l == 0:
            ready = [grads[1][n] for n in big] + [dwout, dwpg, dwpp]
            dbq, dbk, dbv, early_slots = _sb2_bwd(proj, carries, do_b, name="sb_bwd_scatter", exchange=(ready, True))
        else:
            dbq, dbk, dbv, _ = _sb2_bwd(proj, carries, do_b, name="sb_bwd")
        daq, daf, dai, dlb = _hgrn_bwd(proj, lbs[l], states, do_a, name="hgrn_bwd")
        dproj = [daq, daf, dai, dag, dbq, dbk, dbv, dbg]
        dwin = _mm_tn(u, dproj, name="mm_in_dw")
        if l == 0:
            du, (win0_slots,) = _mm_nt(dproj, win, name="mm_in_dx_scatter", exchange=([dwin], True))
        else:
            du, _ = _mm_nt(dproj, win, name="mm_in_dx")
        dh, dg_mix = _norm_bwd(h0, norm_mix[l:l + 1], du, dh1, name="norm_mix_bwd")
        grads[l] = dict(w_in=dwin, w_out=dwout, w_ple_gate=dwpg, w_ple_proj=dwpp, norm_mix=dg_mix, a_out_norm=dg_a,
                        b_out_norm=dg_b, ple_gate_norm=dg_pg, ple_post_norm=dg_post, lb=dlb)

    grad_x = dh.reshape(x.shape)
    loss = lax.psum(loss_part[0, 0], ("x", "y", "c"))

    slots = {("w_in", 0): win0_slots}
    for k, key in enumerate([(n, 1) for n in big] + [(n, 0) for n in big[1:]]):
        slots[key] = early_slots[k]
    results = {}
    for n, w, m, v in zip(big, (w_in, w_out, w_ple_gate, w_ple_proj), (m_w_in, m_w_out, m_w_ple_gate, m_w_ple_proj),
                          (v_w_in, v_w_out, v_w_ple_gate, v_w_ple_proj)):
        per_layer = [_adamw(w[l], m[l], v[l], slots[(n, l)], name="adamw_" + n) for l in range(depth)]
        results[n] = [jnp.stack([per_layer[l][k] for l in range(depth)]) for k in range(4)]

    small_w = dict(norm_mix=norm_mix, a_out_norm=a_out_norm, b_out_norm=b_out_norm, ple_gate_norm=ple_gate_norm,
                   ple_post_norm=ple_post_norm)
    small_m = dict(norm_mix=m_norm_mix, a_out_norm=m_a_out_norm, b_out_norm=m_b_out_norm, ple_gate_norm=m_ple_gate_norm,
                   ple_post_norm=m_ple_post_norm)
    small_v = dict(norm_mix=v_norm_mix, a_out_norm=v_a_out_norm, b_out_norm=v_b_out_norm, ple_gate_norm=v_ple_gate_norm,
                   ple_post_norm=v_ple_post_norm)
    g_vecs = [jnp.concatenate([grads[l][n].reshape(-1) for l in range(depth)]) for n in _SMALL] + [dg_final.reshape(-1)]
    g_pack, offs = _pack_rows(g_vecs)
    w_pack, _ = _pack_rows([small_w[n].reshape(-1) for n in _SMALL] + [final_norm])
    m_pack, _ = _pack_rows([small_m[n].reshape(-1) for n in _SMALL] + [m_final_norm])
    v_pack, _ = _pack_rows([small_v[n].reshape(-1) for n in _SMALL] + [v_final_norm])
    small_slots, lb_slots = _exchange([g_pack, grads[1]["lb"]], scatter=False, name="gather_small_grads")
    packed = _adamw(w_pack, m_pack, v_pack, small_slots, name="adamw_small")
    results["lb_logits"] = _adamw_lb(lb_logits, m_lb_logits, v_lb_logits, lb_slots, name="adamw_lb")
    shapes = [small_w[n].shape for n in _SMALL] + [final_norm.shape]
    for n, off, shp in zip(_SMALL + ("final_norm",), offs, shapes):
        rows = 1
        for dim in shp:
            rows *= dim
        rows //= LANES
        results[n] = [o[off:off + rows].reshape(shp) for o in packed]

    order = ("norm_mix", "w_in", "a_out_norm", "b_out_norm", "w_out", "lb_logits", "ple_gate_norm", "w_ple_gate",
             "w_ple_proj", "ple_post_norm", "final_norm")
    out = [loss, grad_x]
    for k in range(4):
        out += [results[n][k] for n in order]
    return tuple(out)
```
